```python
import jax
import jax.numpy as jnp
from jax import lax
import numpy as np

D_MODEL = 2048
BATCH = 4
SEQ = 4096
DEPTH = 1

RWKV_HEAD = 64
RWKV_DIM = D_MODEL
RWKV_HEADS = RWKV_DIM // RWKV_HEAD
DECAY_LORA = 96
AAA_LORA = 96
GATE_LORA = 256
GN_EPS = 64e-5
DIL_GROUPS = ((128, 1), (512, 4), (2048, 16))
N_GROUPS = len(DIL_GROUPS)
DIL_HEADS = 8
DIL_HEAD_DIM = 64
DIL_GROUP_W = DIL_HEADS * DIL_HEAD_DIM
DIL_DIM = N_GROUPS * DIL_GROUP_W
BLK = 128
ROPE_THETA = 10000.0
NEG_INF = -1e30
RWKV_COLS = 3 * RWKV_DIM + DECAY_LORA + AAA_LORA + GATE_LORA
ATTN_COLS = 3 * DIL_DIM
GATE_COLS = 2 * D_MODEL
IN_COLS = RWKV_COLS + ATTN_COLS + GATE_COLS
MEM_LEN = 256
CA_HEADS = 4
CA_HEAD_DIM = D_MODEL // CA_HEADS
N_EXPERTS = 32
TOP_K = 4
D_FF = D_MODEL
SWIGLU_LIMIT = 7.0
SWIGLU_ALPHA = 1.702
MOE_BLK = 128
LN_EPS = 1e-5
DEEPNORM_ALPHA = (2 * DEPTH) ** 0.25
DEEPNORM_BETA = (8 * DEPTH) ** -0.25

kernel_name = 'hybrid_rwkv7_dilated_attn_moe_layer'


def layer_norm(t, g, b):
    tf = t.astype(jnp.float32)
    mu = jnp.mean(tf, axis=-1, keepdims=True)
    var = jnp.mean(jnp.square(tf - mu), axis=-1, keepdims=True)
    return ((tf - mu) * lax.rsqrt(var + LN_EPS) * g + b).astype(t.dtype)


def token_shift(z, mix):
    prev = jnp.pad(z, ((0, 0), (1, 0), (0, 0)))[:, :-1]
    return z + (prev - z) * mix


def rotary(t):
    s, e = t.shape[1], t.shape[-1]
    half = e // 2
    inv = ROPE_THETA ** (-jnp.arange(half, dtype=jnp.float32) * 2.0 / e)
    ang = jnp.arange(s, dtype=jnp.float32)[:, None] * inv[None, :]
    shape = (1, s) + (1,) * (t.ndim - 3) + (half,)
    cos, sin = jnp.cos(ang).reshape(shape), jnp.sin(ang).reshape(shape)
    tf = t.astype(jnp.float32)
    t1, t2 = tf[..., :half], tf[..., half:]
    return jnp.concatenate([t1 * cos - t2 * sin, t2 * cos + t1 * sin], axis=-1).astype(t.dtype)


def rwkv7_scan(r, w, k, v, a, b):
    bsz, _, h, n = r.shape

    def step(state, inp):
        r_t, w_t, k_t, v_t, a_t, b_t = inp
        sa = jnp.einsum('bhij,bhj->bhi', state, a_t)
        state = (state * w_t[:, :, None, :] + sa[..., None] * b_t[:, :, None, :]
                 + v_t[..., None] * k_t[:, :, None, :])
        return state, jnp.einsum('bhij,bhj->bhi', state, r_t)

    xs = tuple(jnp.moveaxis(t, 1, 0) for t in (r, w, k, v, a, b))
    state0 = jnp.zeros((bsz, h, n, n), jnp.float32)
    _, ys = lax.scan(step, state0, xs)
    return jnp.moveaxis(ys, 0, 1)


def rwkv7_branch(z, w0, w_up, a0, a_up, g_up, k_k, k_a, r_k, lnx_g, lnx_b):
    bsz, s, _ = z.shape
    f32 = jnp.float32
    cut = np.cumsum([RWKV_DIM, RWKV_DIM, RWKV_DIM, DECAY_LORA, AAA_LORA]).tolist()
    r, k, v, wd, ad, gd = jnp.split(z, cut, axis=-1)
    heads = lambda t: t.reshape(bsz, s, RWKV_HEADS, RWKV_HEAD).astype(f32)
    w = -jax.nn.softplus(-(w0 + jnp.tanh(wd) @ w_up).astype(f32)) - 0.5
    decay = jnp.exp(-jnp.exp(w))
    a = jax.nn.sigmoid(a0 + ad @ a_up)
    g = jax.nn.sigmoid(gd) @ g_up
    kk = heads(k * k_k)
    kk = kk / jnp.maximum(jnp.sqrt(jnp.sum(kk * kk, axis=-1, keepdims=True)), 1e-12)
    k = k * (1.0 + (a - 1.0) * k_a)
    rh, kh, vh, ah = heads(r), heads(k), heads(v), heads(a)
    y = rwkv7_scan(rh, heads(decay), kh, vh, -kk, kk * ah)
    mu = jnp.mean(y, axis=-1, keepdims=True)
    var = jnp.mean(jnp.square(y - mu), axis=-1, keepdims=True)
    y = ((y - mu) * lax.rsqrt(var + GN_EPS)).reshape(bsz, s, RWKV_DIM) * lnx_g + lnx_b
    bonus = jnp.sum(rh * kh * r_k, axis=-1, keepdims=True) * vh
    y = y + bonus.reshape(bsz, s, RWKV_DIM)
    return y.astype(z.dtype) * g


def dilated_causal_attention(q, k, v, window, dilation):
    b, s, h, e = q.shape
    span = window // dilation
    unit = dilation * BLK
    L = -(-s // unit) * unit
    n = L // dilation
    nb = n // BLK

    def to_sub(t):
        t = jnp.pad(t, ((0, 0), (0, L - s), (0, 0), (0, 0)))
        t = t.reshape(b, n, dilation, h, e).transpose(0, 2, 1, 3, 4)
        return t.reshape(b, dilation, nb, BLK, h, e)

    def with_prev(t):
        prev = jnp.pad(t, ((0, 0), (0, 0), (1, 0), (0, 0), (0, 0), (0, 0)))[:, :, :-1]
        return jnp.concatenate([prev, t], axis=3)

    def from_sub(t):
        rest = t.shape[4:]
        t = jnp.swapaxes(t.reshape((b, dilation, n) + rest), 1, 2)
        return t.reshape((b, L) + rest)[:, :s]

    qs = to_sub(q)
    kw, vw = with_prev(to_sub(k)), with_prev(to_sub(v))
    scores = jnp.einsum('bdnqhe,bdnkhe->bdnhqk', qs, kw,
                        preferred_element_type=jnp.float32) * (e ** -0.5)
    qi = jnp.arange(BLK)[:, None] + BLK
    ki = jnp.arange(2 * BLK)[None, :]
    dist = qi - ki
    band = (dist >= 0) & (dist <= span)
    has_key = (jnp.arange(nb) > 0)[:, None, None] | (ki >= BLK)[None]
    mask = band[None] & has_key
    scores = jnp.where(mask[None, None, :, None], scores, NEG_INF)
    m = jnp.max(scores, axis=-1, keepdims=True)
    p = jnp.exp(scores - m)
    den = jnp.sum(p, axis=-1, keepdims=True)
    o = jnp.einsum('bdnhqk,bdnkhe->bdnqhe', p, vw.astype(jnp.float32))
    o = o / jnp.moveaxis(den[..., 0], 3, 4)[..., None]
    lse = jnp.moveaxis((m + jnp.log(den))[..., 0], 3, 4)
    return from_sub(o), from_sub(lse)


def hybrid_mixer(x, w_in, shift_mix, w0, w_up, a0, a_up, g_up, k_k, k_a, r_k, lnx_g, lnx_b,
                 proj_a, proj_b, w_out):
    bsz, s, _ = x.shape
    proj = x @ w_in
    y_a = rwkv7_branch(token_shift(proj[..., :RWKV_COLS], shift_mix),
                       w0, w_up, a0, a_up, g_up, k_k, k_a, r_k, lnx_g, lnx_b)
    qkv = proj[..., RWKV_COLS:RWKV_COLS + ATTN_COLS].reshape(
        bsz, s, 3, N_GROUPS, DIL_HEADS, DIL_HEAD_DIM)
    q, k, v = rotary(qkv[:, :, 0]), rotary(qkv[:, :, 1]), qkv[:, :, 2]
    outs, lses = [], []
    for gi, (window, dilation) in enumerate(DIL_GROUPS):
        o, lse = dilated_causal_attention(q[:, :, gi], k[:, :, gi], v[:, :, gi], window, dilation)
        outs.append(o)
        lses.append(lse)
    mix = jax.nn.softmax(jnp.stack(lses), axis=0)[..., None]
    y_b = jnp.sum(mix * jnp.stack(outs), axis=0).reshape(bsz, s, DIL_GROUP_W).astype(x.dtype)
    gates = jax.nn.sigmoid(proj[..., RWKV_COLS + ATTN_COLS:])
    merged = gates[..., :D_MODEL] * (y_a @ proj_a) + gates[..., D_MODEL:] * (y_b @ proj_b)
    return merged @ w_out


def memory_cross_attention(x, mem, wq, wkv, wo):
    bsz, s, d = x.shape
    q = (x @ wq).reshape(bsz, s, CA_HEADS, CA_HEAD_DIM)
    kv = (mem @ wkv).reshape(bsz, mem.shape[1], 2, CA_HEADS, CA_HEAD_DIM)
    k, v = kv[:, :, 0], kv[:, :, 1]
    scores = jnp.einsum('bshe,bmhe->bhsm', q, k,
                        preferred_element_type=jnp.float32) * (CA_HEAD_DIM ** -0.5)
    p = jax.nn.softmax(scores, axis=-1).astype(v.dtype)
    o = jnp.einsum('bhsm,bmhe->bshe', p, v).reshape(bsz, s, d)
    return o @ wo


def clamped_swiglu(h):
    glu, lin = h[..., ::2], h[..., 1::2]
    glu = jnp.minimum(glu, SWIGLU_LIMIT)
    lin = jnp.clip(lin, -SWIGLU_LIMIT, SWIGLU_LIMIT)
    return glu * jax.nn.sigmoid(SWIGLU_ALPHA * glu) * (lin + 1.0)


def moe_ffn(x, router_w, router_b, w1, b1, w2, b2):
    bsz, s, d = x.shape
    n = bsz * s
    xf = x.reshape(n, d)
    logits = jnp.dot(xf, router_w, preferred_element_type=jnp.float32) + router_b.astype(jnp.float32)
    top_val, top_idx = lax.top_k(logits, TOP_K)
    top_w = jax.nn.softmax(top_val, axis=-1)
    flat_e = top_idx.reshape(-1)
    order = jnp.argsort(flat_e)
    e_sorted = flat_e[order]
    tok_sorted = (order // TOP_K).astype(jnp.int32)
    gate_sorted = top_w.reshape(-1)[order]
    counts = jnp.bincount(flat_e, length=N_EXPERTS)
    starts = jnp.cumsum(counts) - counts
    padded = (counts + MOE_BLK - 1) // MOE_BLK * MOE_BLK
    pends = jnp.cumsum(padded)
    pstarts = pends - padded
    dest = pstarts[e_sorted] + jnp.arange(n * TOP_K) - starts[e_sorted]
    rows = n * TOP_K + N_EXPERTS * MOE_BLK
    nblk = rows // MOE_BLK
    row_tok = jnp.full((rows,), n, jnp.int32).at[dest].set(tok_sorted)
    row_gate = jnp.zeros((rows,), jnp.float32).at[dest].set(gate_sorted)
    blk_exp = jnp.minimum(jnp.searchsorted(pends, jnp.arange(nblk) * MOE_BLK, side='right'),
                          N_EXPERTS - 1)
    xpad = jnp.concatenate([xf, jnp.zeros((1, d), xf.dtype)], axis=0)

    def expert_block(args):
        tok, e = args
        h = xpad[tok] @ w1[e] + b1[e]
        return clamped_swiglu(h) @ w2[e] + b2[e]

    ys = lax.map(expert_block, (row_tok.reshape(nblk, MOE_BLK), blk_exp))
    ys = ys.reshape(rows, d) * row_gate[:, None].astype(ys.dtype)
    out = jnp.zeros((n + 1, d), ys.dtype).at[row_tok].add(ys)[:n]
    return out.reshape(bsz, s, d)


def setup_inputs(seed: int = 0) -> dict:
    key = jax.random.key(seed)
    ks = jax.random.split(key, 40)
    f32 = jnp.float32
    L = DEPTH

    def nrm(k, shape, scale):
        return jax.random.normal(k, shape, f32) * scale

    beta = DEEPNORM_BETA
    in_col_scale = jnp.concatenate([
        jnp.ones((2 * RWKV_DIM,), f32), jnp.full((RWKV_DIM,), beta, f32),
        jnp.ones((DECAY_LORA + AAA_LORA + GATE_LORA,), f32),
        jnp.ones((2 * DIL_DIM,), f32), jnp.full((DIL_DIM,), beta, f32),
        jnp.ones((GATE_COLS,), f32)])
    kv_col_scale = jnp.concatenate([jnp.ones((D_MODEL,), f32), jnp.full((D_MODEL,), beta, f32)])
    return {
        'x': nrm(ks[0], (BATCH, SEQ, D_MODEL), 1.0),
        'mem': nrm(ks[1], (BATCH, MEM_LEN, D_MODEL), 1.0),
        'w_in': nrm(ks[2], (L, D_MODEL, IN_COLS), D_MODEL ** -0.5) * in_col_scale,
        'shift_mix': jax.random.uniform(ks[3], (L, RWKV_COLS), f32, 0.2, 0.8),
        'w0': jax.random.uniform(ks[4], (L, RWKV_DIM), f32, -6.0, -1.0),
        'w_up': nrm(ks[5], (L, DECAY_LORA, RWKV_DIM), 0.1 * DECAY_LORA ** -0.5),
        'a0': nrm(ks[6], (L, RWKV_DIM), 0.1),
        'a_up': nrm(ks[7], (L, AAA_LORA, RWKV_DIM), 0.5 * AAA_LORA ** -0.5),
        'g_up': nrm(ks[8], (L, GATE_LORA, RWKV_DIM), GATE_LORA ** -0.5),
        'k_k': 0.85 + nrm(ks[9], (L, RWKV_DIM), 0.02),
        'k_a': 1.0 + nrm(ks[10], (L, RWKV_DIM), 0.02),
        'r_k': nrm(ks[11], (L, RWKV_HEADS, RWKV_HEAD), 0.1),
        'lnx_g': 1.0 + nrm(ks[12], (L, RWKV_DIM), 0.02),
        'lnx_b': nrm(ks[13], (L, RWKV_DIM), 0.02),
        'proj_a': nrm(ks[14], (L, RWKV_DIM, D_MODEL), RWKV_DIM ** -0.5),
        'proj_b': nrm(ks[15], (L, DIL_GROUP_W, D_MODEL), DIL_GROUP_W ** -0.5),
        'w_out': nrm(ks[16], (L, D_MODEL, D_MODEL), beta * D_MODEL ** -0.5),
        'ln1_g': 1.0 + nrm(ks[17], (L, D_MODEL), 0.02),
        'ln1_b': nrm(ks[18], (L, D_MODEL), 0.02),
        'ca_wq': nrm(ks[19], (L, D_MODEL, D_MODEL), D_MODEL ** -0.5),
        'ca_wkv': nrm(ks[20], (L, D_MODEL, 2 * D_MODEL), D_MODEL ** -0.5) * kv_col_scale,
        'ca_wo': nrm(ks[21], (L, D_MODEL, D_MODEL), beta * D_MODEL ** -0.5),
        'ln2_g': 1.0 + nrm(ks[22], (L, D_MODEL), 0.02),
        'ln2_b': nrm(ks[23], (L, D_MODEL), 0.02),
        'router_w': nrm(ks[24], (L, D_MODEL, N_EXPERTS), D_MODEL ** -0.5),
        'router_b': nrm(ks[25], (L, N_EXPERTS), 0.01),
        'moe_w1': nrm(ks[26], (L, N_EXPERTS, D_MODEL, 2 * D_FF), beta * D_MODEL ** -0.5),
        'moe_b1': nrm(ks[27], (L, N_EXPERTS, 2 * D_FF), 0.02),
        'moe_w2': nrm(ks[28], (L, N_EXPERTS, D_FF, D_MODEL), beta * D_FF ** -0.5),
        'moe_b2': nrm(ks[29], (L, N_EXPERTS, D_MODEL), 0.02),
        'ln3_g': 1.0 + nrm(ks[30], (L, D_MODEL), 0.02),
        'ln3_b': nrm(ks[31], (L, D_MODEL), 0.02),
    }


def reference(x, mem, w_in, shift_mix, w0, w_up, a0, a_up, g_up, k_k, k_a, r_k, lnx_g, lnx_b,
              proj_a, proj_b, w_out, ln1_g, ln1_b, ca_wq, ca_wkv, ca_wo, ln2_g, ln2_b,
              router_w, router_b, moe_w1, moe_b1, moe_w2, moe_b2, ln3_g, ln3_b):
    for l in range(DEPTH):
        h = hybrid_mixer(x, w_in[l], shift_mix[l], w0[l], w_up[l], a0[l], a_up[l], g_up[l],
                         k_k[l], k_a[l], r_k[l], lnx_g[l], lnx_b[l], proj_a[l], proj_b[l], w_out[l])
        x = layer_norm(DEEPNORM_ALPHA * x + h, ln1_g[l], ln1_b[l])
        h = memory_cross_attention(x, mem, ca_wq[l], ca_wkv[l], ca_wo[l])
        x = layer_norm(DEEPNORM_ALPHA * x + h, ln2_g[l], ln2_b[l])
        h = moe_ffn(x, router_w[l], router_b[l], moe_w1[l], moe_b1[l], moe_w2[l], moe_b2[l])
        x = layer_norm(DEEPNORM_ALPHA * x + h, ln3_g[l], ln3_b[l])
    return x
```

```python
import functools

import jax
import jax.numpy as jnp
import numpy as np
from jax import lax
from jax.experimental import pallas as pl
from jax.experimental.pallas import tpu as pltpu

F32 = jnp.float32
BF16 = jnp.bfloat16

D_MODEL = 2048
RWKV_HEAD = 64
RWKV_HEADS = D_MODEL // RWKV_HEAD
DECAY_LORA = 96
AAA_LORA = 96
GATE_LORA = 256
GN_EPS = 64e-5
DIL_GROUPS = ((128, 1), (512, 4), (2048, 16))
N_GROUPS = len(DIL_GROUPS)
DIL_HEADS = 8
DIL_HEAD_DIM = 64
DIL_GROUP_W = DIL_HEADS * DIL_HEAD_DIM
DIL_DIM = N_GROUPS * DIL_GROUP_W
BLK = 128
ROPE_THETA = 10000.0
NEG_INF = -1e30
RWKV_COLS = 3 * D_MODEL + DECAY_LORA + AAA_LORA + GATE_LORA
ATTN_COLS = 3 * DIL_DIM
CA_HEADS = 4
CA_HEAD_DIM = D_MODEL // CA_HEADS
N_EXPERTS = 32
TOP_K = 4
D_FF = D_MODEL
SWIGLU_LIMIT = 7.0
SWIGLU_ALPHA = 1.702
MOE_BLK = 128
LN_EPS = 1e-5
DEPTH = 1
DEEPNORM_ALPHA = (2 * DEPTH) ** 0.25

V7X_VMEM_LIMIT_BYTES = 56 * 1024 * 1024


def _cparams(*sem):
    return pltpu.CompilerParams(dimension_semantics=sem, vmem_limit_bytes=V7X_VMEM_LIMIT_BYTES)


def _mm_kernel(a_ref, b_ref, o_ref):
    o_ref[...] = jnp.dot(a_ref[...].astype(BF16), b_ref[...].astype(BF16),
                         preferred_element_type=F32).astype(o_ref.dtype)


def _matmul(a, b, *, tm, tn, name, out_dtype=F32):
    m, k = a.shape
    _, n = b.shape
    tm, tn = min(tm, m), min(tn, n)
    assert m % tm == 0 and n % tn == 0, (m, n, tm, tn)
    return pl.pallas_call(
        _mm_kernel,
        grid=(n // tn, m // tm),
        in_specs=[pl.BlockSpec((tm, k), lambda j, i: (i, 0)),
                  pl.BlockSpec((k, tn), lambda j, i: (0, j))],
        out_specs=pl.BlockSpec((tm, tn), lambda j, i: (i, j)),
        out_shape=jax.ShapeDtypeStruct((m, n), out_dtype),
        compiler_params=_cparams("parallel", "parallel"),
        name=name,
    )(a, b)


def _ln_kernel(x_ref, h_ref, g_ref, b_ref, o_ref):
    t = DEEPNORM_ALPHA * x_ref[...] + h_ref[...]
    mu = jnp.mean(t, axis=-1, keepdims=True)
    c = t - mu
    var = jnp.mean(c * c, axis=-1, keepdims=True)
    o_ref[...] = c * lax.rsqrt(var + LN_EPS) * g_ref[...] + b_ref[...]


def _res_layer_norm(x, h, g, b, *, name, tm=256):
    m, d = x.shape
    assert m % tm == 0
    row = pl.BlockSpec((tm, d), lambda i: (i, 0))
    vec = pl.BlockSpec((1, d), lambda i: (0, 0))
    return pl.pallas_call(
        _ln_kernel, grid=(m // tm,), in_specs=[row, row, vec, vec], out_specs=row,
        out_shape=jax.ShapeDtypeStruct((m, d), F32),
        compiler_params=_cparams("parallel"), name=name,
    )(x, h, g.reshape(1, d), b.reshape(1, d))


def _rotary_kernel(x_ref, cos_ref, sin_ref, o_ref):
    cos = cos_ref[...]
    sin = sin_ref[...]
    lane = lax.broadcasted_iota(jnp.int32, cos.shape, 1)
    first_half = (lane % DIL_HEAD_DIM) < (DIL_HEAD_DIM // 2)
    for c in range(x_ref.shape[-1] // 128):
        x = x_ref[:, c * 128:(c + 1) * 128]
        partner = jnp.where(first_half, pltpu.roll(x, 128 - 32, 1), pltpu.roll(x, 32, 1))
        o_ref[:, c * 128:(c + 1) * 128] = x * cos + partner * sin


def _rotary(x, cos_t, sin_t, *, ts=512):
    b, s, w = x.shape
    ts = min(ts, s)
    assert s % ts == 0 and w % 128 == 0
    return pl.pallas_call(
        _rotary_kernel, grid=(b, s // ts),
        in_specs=[pl.BlockSpec((None, ts, w), lambda i, j: (i, j, 0)),
                  pl.BlockSpec((ts, 128), lambda i, j: (j, 0)),
                  pl.BlockSpec((ts, 128), lambda i, j: (j, 0))],
        out_specs=pl.BlockSpec((None, ts, w), lambda i, j: (i, j, 0)),
        out_shape=jax.ShapeDtypeStruct(x.shape, F32),
        compiler_params=_cparams("parallel", "parallel"), name="rotary",
    )(x, cos_t, sin_t)


def _rotary_tables(s):
    half = DIL_HEAD_DIM // 2
    inv = ROPE_THETA ** (-jnp.arange(half, dtype=F32) * 2.0 / DIL_HEAD_DIM)
    ang = jnp.arange(s, dtype=F32)[:, None] * inv[None, :]
    cos, sin = jnp.cos(ang), jnp.sin(ang)
    cos_t = jnp.concatenate([cos, cos, cos, cos], axis=-1)
    sin_t = jnp.concatenate([-sin, sin, -sin, sin], axis=-1)
    return cos_t, sin_t


def _dil_attn_kernel(q_ref, kp_ref, kc_ref, vp_ref, vc_ref, o_ref, l_ref, *, span):
    nb = pl.program_id(2)
    qi = lax.broadcasted_iota(jnp.int32, (BLK, 2 * BLK), 0) + BLK
    ki = lax.broadcasted_iota(jnp.int32, (BLK, 2 * BLK), 1)
    dist = qi - ki
    mask = (dist >= 0) & (dist <= span) & ((nb > 0) | (ki >= BLK))
    scale = DIL_HEAD_DIM ** -0.5
    q = q_ref[...].astype(BF16)
    k = jnp.concatenate([kp_ref[...], kc_ref[...]], axis=0).astype(BF16)
    v = jnp.concatenate([vp_ref[...], vc_ref[...]], axis=0).astype(BF16)
    for h in range(DIL_HEADS):
        sl = slice(h * DIL_HEAD_DIM, (h + 1) * DIL_HEAD_DIM)
        s = lax.dot_general(q[:, sl], k[:, sl], (((1,), (1,)), ((), ())),
                            preferred_element_type=F32) * scale
        s = jnp.where(mask, s, NEG_INF)
        m = jnp.max(s, axis=-1, keepdims=True)
        p = jnp.exp(s - m)
        den = jnp.sum(p, axis=-1, keepdims=True)
        o = jnp.dot(p.astype(BF16), v[:, sl], preferred_element_type=F32)
        o_ref[:, sl] = o / den
        l_ref[:, sl] = jnp.broadcast_to(m + jnp.log(den), (BLK, DIL_HEAD_DIM))


def _dilated_attention(qkv, gi, window, dilation):
    b, s, w = qkv.shape
    d = dilation
    assert s % (d * BLK) == 0
    n = s // d
    nblk = n // BLK
    cpr = w // DIL_GROUP_W
    view = qkv.reshape(b, n, d * w)
    blk = (None, BLK, DIL_GROUP_W)

    def col(which):
        return lambda bi, r, nb: (bi, nb, r * cpr + which * N_GROUPS + gi)

    def col_prev(which):
        return lambda bi, r, nb: (bi, jnp.maximum(nb - 1, 0), r * cpr + which * N_GROUPS + gi)

    out_spec = pl.BlockSpec(blk, lambda bi, r, nb: (bi, nb, r))
    o, l = pl.pallas_call(
        functools.partial(_dil_attn_kernel, span=window // dilation),
        grid=(b, d, nblk),
        in_specs=[pl.BlockSpec(blk, col(0)),
                  pl.BlockSpec(blk, col_prev(1)), pl.BlockSpec(blk, col(1)),
                  pl.BlockSpec(blk, col_prev(2)), pl.BlockSpec(blk, col(2))],
        out_specs=[out_spec, out_spec],
        out_shape=[jax.ShapeDtypeStruct((b, n, d * DIL_GROUP_W), F32)] * 2,
        compiler_params=_cparams("parallel", "parallel", "arbitrary"),
        name=f"dil_attn_g{gi}",
    )(view, view, view, view, view)
    return o.reshape(b, s, DIL_GROUP_W), l.reshape(b, s, DIL_GROUP_W)


def _rwkv_scan_kernel(r_ref, w_ref, k_ref, v_ref, a_ref, b_ref, y_ref, s_ref, *, tc):
    n = RWKV_HEAD

    @pl.when(pl.program_id(0) == 0)
    def _():
        s_ref[...] = jnp.zeros_like(s_ref)

    def row(ref, t, j):
        return ref[t, pl.ds(j, 1), :]

    def step(t, carry):
        acc = [jnp.zeros(s_ref.shape[1:], F32), jnp.zeros(s_ref.shape[1:], F32)]
        for j in range(n):
            acc[j % 2] = acc[j % 2] + s_ref[j] * row(a_ref, t, j)
        sa = acc[0] + acc[1]
        vt = v_ref[t]
        yac = [jnp.zeros(s_ref.shape[1:], F32), jnp.zeros(s_ref.shape[1:], F32)]
        for j in range(n):
            sj = s_ref[j] * row(w_ref, t, j) + sa * row(b_ref, t, j) + vt * row(k_ref, t, j)
            s_ref[j] = sj
            yac[j % 2] = yac[j % 2] + sj * row(r_ref, t, j)
        y_ref[t] = yac[0] + yac[1]
        return carry

    lax.fori_loop(0, tc, step, 0)


def _rwkv_scan(r, w, k, v, a, b, *, tc=32):
    s, n, l = r.shape
    tc = min(tc, s)
    assert s % tc == 0
    blk = pl.BlockSpec((tc, n, l), lambda i: (i, 0, 0))
    return pl.pallas_call(
        functools.partial(_rwkv_scan_kernel, tc=tc),
        grid=(s // tc,), in_specs=[blk] * 6, out_specs=blk,
        out_shape=jax.ShapeDtypeStruct((s, n, l), F32),
        scratch_shapes=[pltpu.VMEM((n, n, l), F32)],
        compiler_params=_cparams("arbitrary"), name="rwkv_scan",
    )(r, w, k, v, a, b)


def _cross_attn_kernel(q_ref, k_ref, v_ref, o_ref):
    scale = CA_HEAD_DIM ** -0.5
    for h in range(CA_HEADS):
        sl = slice(h * CA_HEAD_DIM, (h + 1) * CA_HEAD_DIM)
        q = q_ref[:, sl].astype(BF16)
        k = k_ref[:, sl].astype(BF16)
        s = lax.dot_general(q, k, (((1,), (1,)), ((), ())), preferred_element_type=F32) * scale
        m = jnp.max(s, axis=-1, keepdims=True)
        p = jnp.exp(s - m)
        den = jnp.sum(p, axis=-1, keepdims=True)
        o = jnp.dot(p.astype(BF16), v_ref[:, sl].astype(BF16), preferred_element_type=F32)
        o_ref[:, sl] = o / den


def _cross_attention(q, kv, *, tq=512):
    b, s, d = q.shape
    mlen = kv.shape[1]
    tq = min(tq, s)
    return pl.pallas_call(
        _cross_attn_kernel, grid=(b, s // tq),
        in_specs=[pl.BlockSpec((None, tq, d), lambda i, j: (i, j, 0)),
                  pl.BlockSpec((None, mlen, d), lambda i, j: (i, 0, 0)),
                  pl.BlockSpec((None, mlen, d), lambda i, j: (i, 0, 1))],
        out_specs=pl.BlockSpec((None, tq, d), lambda i, j: (i, j, 0)),
        out_shape=jax.ShapeDtypeStruct((b, s, d), F32),
        compiler_params=_cparams("parallel", "parallel"), name="cross_attn",
    )(q, kv, kv)


def _router_kernel(x_ref, w_ref, b_ref, o_ref):
    o_ref[...] = jnp.dot(x_ref[...], w_ref[...], precision=lax.Precision.HIGHEST,
                         preferred_element_type=F32) + b_ref[...]


def _router_logits(x, w, b, *, tm=512):
    m, d = x.shape
    e = w.shape[1]
    return pl.pallas_call(
        _router_kernel, grid=(m // tm,),
        in_specs=[pl.BlockSpec((tm, d), lambda i: (i, 0)),
                  pl.BlockSpec((d, e), lambda i: (0, 0)),
                  pl.BlockSpec((1, e), lambda i: (0, 0))],
        out_specs=pl.BlockSpec((tm, e), lambda i: (i, 0)),
        out_shape=jax.ShapeDtypeStruct((m, e), F32),
        compiler_params=_cparams("parallel"), name="router",
    )(x, w, b.reshape(1, e))


def _moe_kernel(be_ref, nused_ref, x_ref, g_ref, w1g_ref, w1l_ref, b1g_ref, b1l_ref, w2_ref,
                b2_ref, o_ref):
    i = pl.program_id(0)

    @pl.when(i < nused_ref[0])
    def _():
        x = x_ref[...]
        glu = jnp.dot(x, w1g_ref[...], preferred_element_type=F32) + b1g_ref[...]
        lin = jnp.dot(x, w1l_ref[...], preferred_element_type=F32) + b1l_ref[...]
        glu = jnp.minimum(glu, SWIGLU_LIMIT)
        lin = jnp.clip(lin, -SWIGLU_LIMIT, SWIGLU_LIMIT)
        act = glu * jax.nn.sigmoid(SWIGLU_ALPHA * glu) * (lin + 1.0)
        y = jnp.dot(act.astype(BF16), w2_ref[...], preferred_element_type=F32) + b2_ref[...]
        o_ref[...] = y * g_ref[...]

    @pl.when(i >= nused_ref[0])
    def _():
        o_ref[...] = jnp.zeros_like(o_ref)


def _moe_experts(blk_exp, n_used, xs, row_gate, w1g, w1l, b1g, b1l, w2, b2):
    rows, d = xs.shape
    nblk = rows // MOE_BLK
    f = w1g.shape[-1]
    wspec = lambda shape: pl.BlockSpec(shape, lambda i, be, nu: (be[i], 0, 0),
                                       pipeline_mode=pl.Buffered(1))
    grid_spec = pltpu.PrefetchScalarGridSpec(
        num_scalar_prefetch=2, grid=(nblk,),
        in_specs=[pl.BlockSpec((MOE_BLK, d), lambda i, be, nu: (i, 0)),
                  pl.BlockSpec((MOE_BLK, 1), lambda i, be, nu: (i, 0)),
                  wspec((None, d, f)), wspec((None, d, f)),
                  wspec((None, 1, f)), wspec((None, 1, f)),
                  wspec((None, f, d)), wspec((None, 1, d))],
        out_specs=pl.BlockSpec((MOE_BLK, d), lambda i, be, nu: (i, 0)),
    )
    return pl.pallas_call(
        _moe_kernel, grid_spec=grid_spec,
        out_shape=jax.ShapeDtypeStruct((rows, d), F32),
        compiler_params=_cparams("arbitrary"), name="moe_experts",
    )(blk_exp, n_used, xs, row_gate.reshape(rows, 1), w1g, w1l, b1g, b1l, w2, b2)


def _moe_ffn(x2d, router_w, router_b, w1, b1, w2, b2):
    n, d = x2d.shape
    logits = _router_logits(x2d, router_w, router_b)
    top_val, top_idx = lax.top_k(logits, TOP_K)
    top_w = jax.nn.softmax(top_val, axis=-1)
    flat_e = top_idx.reshape(-1)
    order = jnp.argsort(flat_e)
    e_sorted = flat_e[order]
    tok_sorted = (order // TOP_K).astype(jnp.int32)
    gate_sorted = top_w.reshape(-1)[order]
    counts = jnp.bincount(flat_e, length=N_EXPERTS)
    starts = jnp.cumsum(counts) - counts
    padded = (counts + MOE_BLK - 1) // MOE_BLK * MOE_BLK
    pends = jnp.cumsum(padded)
    pstarts = pends - padded
    dest = (pstarts[e_sorted] + jnp.arange(n * TOP_K) - starts[e_sorted]).astype(jnp.int32)
    rows = n * TOP_K + N_EXPERTS * MOE_BLK
    nblk = rows // MOE_BLK
    row_tok = jnp.full((rows,), n, jnp.int32).at[dest].set(tok_sorted)
    row_gate = jnp.zeros((rows,), F32).at[dest].set(gate_sorted)
    blk_exp = jnp.minimum(jnp.searchsorted(pends, jnp.arange(nblk) * MOE_BLK, side='right'),
                          N_EXPERTS - 1).astype(jnp.int32)
    n_used = (pends[-1] // MOE_BLK).astype(jnp.int32).reshape(1)
    xpad = jnp.concatenate([x2d.astype(BF16), jnp.zeros((1, d), BF16)], axis=0)
    xs = xpad[row_tok]
    w1g = w1[:, :, 0::2].astype(BF16)
    w1l = w1[:, :, 1::2].astype(BF16)
    b1g = b1[:, None, 0::2]
    b1l = b1[:, None, 1::2]
    ys = _moe_experts(blk_exp, n_used, xs, row_gate, w1g, w1l, b1g, b1l, w2.astype(BF16),
                      b2[:, None, :])
    pos = jnp.zeros((n * TOP_K,), jnp.int32).at[order].set(dest)
    return ys[pos].reshape(n, TOP_K, d).sum(axis=1)


def _token_shift(z, mix):
    prev = jnp.pad(z, ((0, 0), (1, 0), (0, 0)))[:, :-1]
    return z + (prev - z) * mix


def _to_scan_layout(t, bsz, s):
    return t.reshape(bsz, s, RWKV_HEADS, RWKV_HEAD).transpose(1, 3, 0, 2).reshape(
        s, RWKV_HEAD, bsz * RWKV_HEADS)


def _from_scan_layout(t, bsz, s):
    return t.reshape(s, RWKV_HEAD, bsz, RWKV_HEADS).transpose(2, 0, 3, 1).reshape(bsz, s, D_MODEL)


def _rwkv7_branch(p_rkv, p_lora, shift_mix, w0, w_up, a0, a_up, g_up, k_k, k_a, r_k, lnx_g, lnx_b):
    bsz, s, _ = p_rkv.shape
    n_tok = bsz * s
    z = _token_shift(p_rkv, shift_mix[:3 * D_MODEL])
    zl = _token_shift(p_lora, shift_mix[3 * D_MODEL:])
    r, k, v = jnp.split(z, 3, axis=-1)
    wd = zl[..., :DECAY_LORA]
    ad = zl[..., DECAY_LORA:DECAY_LORA + AAA_LORA]
    gd = zl[..., DECAY_LORA + AAA_LORA:]
    mm = lambda t, wgt, name: _matmul(t.reshape(n_tok, -1), wgt, tm=1024, tn=1024,
                                      name=name).reshape(bsz, s, -1)
    w = -jax.nn.softplus(-(w0 + mm(jnp.tanh(wd), w_up, "rwkv_w_up"))) - 0.5
    decay = jnp.exp(-jnp.exp(w))
    a = jax.nn.sigmoid(a0 + mm(ad, a_up, "rwkv_a_up"))
    g = mm(jax.nn.sigmoid(gd), g_up, "rwkv_g_up")
    heads = lambda t: t.reshape(bsz, s, RWKV_HEADS, RWKV_HEAD)
    kk = heads(k * k_k)
    kk = kk / jnp.maximum(jnp.sqrt(jnp.sum(kk * kk, axis=-1, keepdims=True)), 1e-12)
    kk = kk.reshape(bsz, s, D_MODEL)
    k = k * (1.0 + (a - 1.0) * k_a)
    lay = lambda t: _to_scan_layout(t, bsz, s)
    y = _rwkv_scan(lay(r), lay(decay), lay(k), lay(v), lay(-kk), lay(kk * a))
    y = heads(_from_scan_layout(y, bsz, s))
    mu = jnp.mean(y, axis=-1, keepdims=True)
    var = jnp.mean(jnp.square(y - mu), axis=-1, keepdims=True)
    y = ((y - mu) * lax.rsqrt(var + GN_EPS)).reshape(bsz, s, D_MODEL) * lnx_g + lnx_b
    bonus = jnp.sum(heads(r) * heads(k) * r_k, axis=-1, keepdims=True) * heads(v)
    y = y + bonus.reshape(bsz, s, D_MODEL)
    return y * g


def _hybrid_mixer(x, w_in, shift_mix, w0, w_up, a0, a_up, g_up, k_k, k_a, r_k, lnx_g, lnx_b,
                  proj_a, proj_b, w_out):
    bsz, s, d = x.shape
    n_tok = bsz * s
    x2d = x.reshape(n_tok, d)
    c0, c1, c2 = 3 * D_MODEL, RWKV_COLS, RWKV_COLS + ATTN_COLS
    p_rkv = _matmul(x2d, w_in[:, :c0], tm=512, tn=1024, name="in_rkv").reshape(bsz, s, -1)
    p_lora = _matmul(x2d, w_in[:, c0:c1], tm=1024, tn=512, name="in_lora").reshape(bsz, s, -1)
    p_attn = _matmul(x2d, w_in[:, c1:c2], tm=512, tn=1536, name="in_attn").reshape(bsz, s, -1)
    p_gate = _matmul(x2d, w_in[:, c2:], tm=512, tn=1024, name="in_gate").reshape(bsz, s, -1)

    y_a = _rwkv7_branch(p_rkv, p_lora, shift_mix, w0, w_up, a0, a_up, g_up, k_k, k_a, r_k,
                        lnx_g, lnx_b)

    cos_t, sin_t = _rotary_tables(s)
    qk = _rotary(p_attn[..., :2 * DIL_DIM], cos_t, sin_t)
    qkv = jnp.concatenate([qk, p_attn[..., 2 * DIL_DIM:]], axis=-1)
    outs, lses = [], []
    for gi, (window, dilation) in enumerate(DIL_GROUPS):
        o, l = _dilated_attention(qkv, gi, window, dilation)
        outs.append(o)
        lses.append(l)
    mix = jax.nn.softmax(jnp.stack(lses), axis=0)
    y_b = jnp.sum(mix * jnp.stack(outs), axis=0)

    gates = jax.nn.sigmoid(p_gate)
    pa = _matmul(y_a.reshape(n_tok, d), proj_a, tm=512, tn=1024, name="proj_a")
    pb = _matmul(y_b.reshape(n_tok, DIL_GROUP_W), proj_b, tm=512, tn=1024, name="proj_b")
    merged = gates.reshape(n_tok, -1)[:, :d] * pa + gates.reshape(n_tok, -1)[:, d:] * pb
    return _matmul(merged, w_out, tm=512, tn=1024, name="w_out")


def _memory_cross_attention(x2d, mem, wq, wkv, wo, bsz, s):
    d = x2d.shape[-1]
    q = _matmul(x2d, wq, tm=512, tn=1024, name="ca_q").reshape(bsz, s, d)
    kv = _matmul(mem.reshape(-1, d), wkv, tm=512, tn=1024, name="ca_kv").reshape(bsz, -1, 2 * d)
    o = _cross_attention(q, kv)
    return _matmul(o.reshape(-1, d), wo, tm=512, tn=1024, name="ca_o")


def kernel(x, mem, w_in, shift_mix, w0, w_up, a0, a_up, g_up, k_k, k_a, r_k, lnx_g, lnx_b, proj_a, proj_b, w_out, ln1_g, ln1_b, ca_wq, ca_wkv, ca_wo, ln2_g, ln2_b, router_w, router_b, moe_w1, moe_b1, moe_w2, moe_b2, ln3_g, ln3_b):
    bsz, s, d = x.shape
    n_tok = bsz * s
    for l in range(DEPTH):
        h = _hybrid_mixer(x, w_in[l], shift_mix[l], w0[l], w_up[l], a0[l], a_up[l], g_up[l],
                          k_k[l], k_a[l], r_k[l], lnx_g[l], lnx_b[l], proj_a[l], proj_b[l],
                          w_out[l])
        x1 = _res_layer_norm(x.reshape(n_tok, d), h, ln1_g[l], ln1_b[l], name="ln1")
        h = _memory_cross_attention(x1, mem, ca_wq[l], ca_wkv[l], ca_wo[l], bsz, s)
        x2 = _res_layer_norm(x1, h, ln2_g[l], ln2_b[l], name="ln2")
        h = _moe_ffn(x2, router_w[l], router_b[l], moe_w1[l], moe_b1[l], moe_w2[l], moe_b2[l])
        x = _res_layer_norm(x2, h, ln3_g[l], ln3_b[l], name="ln3").reshape(bsz, s, d)
    return x
```

```python
import functools

import jax
import jax.numpy as jnp
import numpy as np
from jax import lax
from jax.experimental import pallas as pl
from jax.experimental.pallas import tpu as pltpu

F32 = jnp.float32
BF16 = jnp.bfloat16

D_MODEL = 2048
RWKV_HEAD = 64
RWKV_HEADS = D_MODEL // RWKV_HEAD
DECAY_LORA = 96
AAA_LORA = 96
GATE_LORA = 256
GN_EPS = 64e-5
DIL_GROUPS = ((128, 1), (512, 4), (2048, 16))
N_GROUPS = len(DIL_GROUPS)
DIL_HEADS = 8
DIL_HEAD_DIM = 64
DIL_GROUP_W = DIL_HEADS * DIL_HEAD_DIM
DIL_DIM = N_GROUPS * DIL_GROUP_W
BLK = 128
ROPE_THETA = 10000.0
NEG_INF = -1e30
RWKV_COLS = 3 * D_MODEL + DECAY_LORA + AAA_LORA + GATE_LORA
ATTN_COLS = 3 * DIL_DIM
CA_HEADS = 4
CA_HEAD_DIM = D_MODEL // CA_HEADS
N_EXPERTS = 32
TOP_K = 4
D_FF = D_MODEL
SWIGLU_LIMIT = 7.0
SWIGLU_ALPHA = 1.702
MOE_BLK = 128
LN_EPS = 1e-5
DEPTH = 1
DEEPNORM_ALPHA = (2 * DEPTH) ** 0.25

V7X_VMEM_LIMIT_BYTES = 56 * 1024 * 1024


def _cparams(*sem):
    return pltpu.CompilerParams(dimension_semantics=sem, vmem_limit_bytes=V7X_VMEM_LIMIT_BYTES)


def _mm_kernel(a_ref, b_ref, o_ref):
    o_ref[...] = jnp.dot(a_ref[...].astype(BF16), b_ref[...].astype(BF16),
                         preferred_element_type=F32).astype(o_ref.dtype)


def _matmul(a, b, *, tm, tn, name, out_dtype=F32):
    m, k = a.shape
    _, n = b.shape
    tm, tn = min(tm, m), min(tn, n)
    assert m % tm == 0 and n % tn == 0, (m, n, tm, tn)
    return pl.pallas_call(
        _mm_kernel,
        grid=(n // tn, m // tm),
        in_specs=[pl.BlockSpec((tm, k), lambda j, i: (i, 0)),
                  pl.BlockSpec((k, tn), lambda j, i: (0, j))],
        out_specs=pl.BlockSpec((tm, tn), lambda j, i: (i, j)),
        out_shape=jax.ShapeDtypeStruct((m, n), out_dtype),
        compiler_params=_cparams("parallel", "parallel"),
        name=name,
    )(a, b)


def _ln_kernel(x_ref, h_ref, g_ref, b_ref, o_ref):
    t = DEEPNORM_ALPHA * x_ref[...] + h_ref[...]
    mu = jnp.mean(t, axis=-1, keepdims=True)
    c = t - mu
    var = jnp.mean(c * c, axis=-1, keepdims=True)
    o_ref[...] = c * lax.rsqrt(var + LN_EPS) * g_ref[...] + b_ref[...]


def _res_layer_norm(x, h, g, b, *, name, tm=256):
    m, d = x.shape
    assert m % tm == 0
    row = pl.BlockSpec((tm, d), lambda i: (i, 0))
    vec = pl.BlockSpec((1, d), lambda i: (0, 0))
    return pl.pallas_call(
        _ln_kernel, grid=(m // tm,), in_specs=[row, row, vec, vec], out_specs=row,
        out_shape=jax.ShapeDtypeStruct((m, d), F32),
        compiler_params=_cparams("parallel"), name=name,
    )(x, h, g.reshape(1, d), b.reshape(1, d))


def _rotary_kernel(x_ref, cos_ref, sin_ref, o_ref):
    cos = cos_ref[...]
    sin = sin_ref[...]
    lane = lax.broadcasted_iota(jnp.int32, cos.shape, 1)
    first_half = (lane % DIL_HEAD_DIM) < (DIL_HEAD_DIM // 2)
    for c in range(x_ref.shape[-1] // 128):
        x = x_ref[:, c * 128:(c + 1) * 128]
        partner = jnp.where(first_half, pltpu.roll(x, 128 - 32, 1), pltpu.roll(x, 32, 1))
        o_ref[:, c * 128:(c + 1) * 128] = x * cos + partner * sin


def _rotary(x, cos_t, sin_t, *, ts=512):
    b, s, w = x.shape
    ts = min(ts, s)
    assert s % ts == 0 and w % 128 == 0
    return pl.pallas_call(
        _rotary_kernel, grid=(b, s // ts),
        in_specs=[pl.BlockSpec((None, ts, w), lambda i, j: (i, j, 0)),
                  pl.BlockSpec((ts, 128), lambda i, j: (j, 0)),
                  pl.BlockSpec((ts, 128), lambda i, j: (j, 0))],
        out_specs=pl.BlockSpec((None, ts, w), lambda i, j: (i, j, 0)),
        out_shape=jax.ShapeDtypeStruct(x.shape, F32),
        compiler_params=_cparams("parallel", "parallel"), name="rotary",
    )(x, cos_t, sin_t)


def _rotary_tables(s):
    half = DIL_HEAD_DIM // 2
    inv = ROPE_THETA ** (-jnp.arange(half, dtype=F32) * 2.0 / DIL_HEAD_DIM)
    ang = jnp.arange(s, dtype=F32)[:, None] * inv[None, :]
    cos, sin = jnp.cos(ang), jnp.sin(ang)
    cos_t = jnp.concatenate([cos, cos, cos, cos], axis=-1)
    sin_t = jnp.concatenate([-sin, sin, -sin, sin], axis=-1)
    return cos_t, sin_t


def _dil_attn_kernel(q_ref, kp_ref, kc_ref, vp_ref, vc_ref, o_ref, l_ref, *, span):
    nb = pl.program_id(2)
    qi = lax.broadcasted_iota(jnp.int32, (BLK, 2 * BLK), 0) + BLK
    ki = lax.broadcasted_iota(jnp.int32, (BLK, 2 * BLK), 1)
    dist = qi - ki
    mask = (dist >= 0) & (dist <= span) & ((nb > 0) | (ki >= BLK))
    scale = DIL_HEAD_DIM ** -0.5
    q = q_ref[...].astype(BF16)
    k = jnp.concatenate([kp_ref[...], kc_ref[...]], axis=0).astype(BF16)
    v = jnp.concatenate([vp_ref[...], vc_ref[...]], axis=0).astype(BF16)
    for h in range(DIL_HEADS):
        sl = slice(h * DIL_HEAD_DIM, (h + 1) * DIL_HEAD_DIM)
        s = lax.dot_general(q[:, sl], k[:, sl], (((1,), (1,)), ((), ())),
                            preferred_element_type=F32) * scale
        s = jnp.where(mask, s, NEG_INF)
        m = jnp.max(s, axis=-1, keepdims=True)
        p = jnp.exp(s - m)
        den = jnp.sum(p, axis=-1, keepdims=True)
        o = jnp.dot(p.astype(BF16), v[:, sl], preferred_element_type=F32)
        o_ref[:, sl] = o / den
        l_ref[:, sl] = jnp.broadcast_to(m + jnp.log(den), (BLK, DIL_HEAD_DIM))


def _dilated_attention(qkv, gi, window, dilation):
    b, s, w = qkv.shape
    d = dilation
    assert s % (d * BLK) == 0
    n = s // d
    nblk = n // BLK
    cpr = w // DIL_GROUP_W
    view = qkv.reshape(b, n, d * w)
    blk = (None, BLK, DIL_GROUP_W)

    def col(which):
        return lambda bi, r, nb: (bi, nb, r * cpr + which * N_GROUPS + gi)

    def col_prev(which):
        return lambda bi, r, nb: (bi, jnp.maximum(nb - 1, 0), r * cpr + which * N_GROUPS + gi)

    out_spec = pl.BlockSpec(blk, lambda bi, r, nb: (bi, nb, r))
    o, l = pl.pallas_call(
        functools.partial(_dil_attn_kernel, span=window // dilation),
        grid=(b, d, nblk),
        in_specs=[pl.BlockSpec(blk, col(0)),
                  pl.BlockSpec(blk, col_prev(1)), pl.BlockSpec(blk, col(1)),
                  pl.BlockSpec(blk, col_prev(2)), pl.BlockSpec(blk, col(2))],
        out_specs=[out_spec, out_spec],
        out_shape=[jax.ShapeDtypeStruct((b, n, d * DIL_GROUP_W), F32)] * 2,
        compiler_params=_cparams("parallel", "parallel", "arbitrary"),
        name=f"dil_attn_g{gi}",
    )(view, view, view, view, view)
    return o.reshape(b, s, DIL_GROUP_W), l.reshape(b, s, DIL_GROUP_W)


def _rwkv_scan_kernel(r_ref, w_ref, k_ref, v_ref, a_ref, b_ref, y_ref, s_ref, *, tc):
    n = RWKV_HEAD

    @pl.when(pl.program_id(0) == 0)
    def _():
        s_ref[...] = jnp.zeros_like(s_ref)

    def row(ref, t, j):
        return ref[t, pl.ds(j, 1), :]

    def step(t, carry):
        acc = [jnp.zeros(s_ref.shape[1:], F32), jnp.zeros(s_ref.shape[1:], F32)]
        for j in range(n):
            acc[j % 2] = acc[j % 2] + s_ref[j] * row(a_ref, t, j)
        sa = acc[0] + acc[1]
        vt = v_ref[t]
        yac = [jnp.zeros(s_ref.shape[1:], F32), jnp.zeros(s_ref.shape[1:], F32)]
        for j in range(n):
            sj = s_ref[j] * row(w_ref, t, j) + sa * row(b_ref, t, j) + vt * row(k_ref, t, j)
            s_ref[j] = sj
            yac[j % 2] = yac[j % 2] + sj * row(r_ref, t, j)
        y_ref[t] = yac[0] + yac[1]
        return carry

    lax.fori_loop(0, tc, step, 0)


def _rwkv_scan(r, w, k, v, a, b, *, tc=32):
    s, n, l = r.shape
    tc = min(tc, s)
    assert s % tc == 0
    blk = pl.BlockSpec((tc, n, l), lambda i: (i, 0, 0))
    return pl.pallas_call(
        functools.partial(_rwkv_scan_kernel, tc=tc),
        grid=(s // tc,), in_specs=[blk] * 6, out_specs=blk,
        out_shape=jax.ShapeDtypeStruct((s, n, l), F32),
        scratch_shapes=[pltpu.VMEM((n, n, l), F32)],
        compiler_params=_cparams("arbitrary"), name="rwkv_scan",
    )(r, w, k, v, a, b)


def _cross_attn_kernel(q_ref, k_ref, v_ref, o_ref):
    scale = CA_HEAD_DIM ** -0.5
    for h in range(CA_HEADS):
        sl = slice(h * CA_HEAD_DIM, (h + 1) * CA_HEAD_DIM)
        q = q_ref[:, sl].astype(BF16)
        k = k_ref[:, sl].astype(BF16)
        s = lax.dot_general(q, k, (((1,), (1,)), ((), ())), preferred_element_type=F32) * scale
        m = jnp.max(s, axis=-1, keepdims=True)
        p = jnp.exp(s - m)
        den = jnp.sum(p, axis=-1, keepdims=True)
        o = jnp.dot(p.astype(BF16), v_ref[:, sl].astype(BF16), preferred_element_type=F32)
        o_ref[:, sl] = o / den


def _cross_attention(q, kv, *, tq=512):
    b, s, d = q.shape
    mlen = kv.shape[1]
    tq = min(tq, s)
    return pl.pallas_call(
        _cross_attn_kernel, grid=(b, s // tq),
        in_specs=[pl.BlockSpec((None, tq, d), lambda i, j: (i, j, 0)),
                  pl.BlockSpec((None, mlen, d), lambda i, j: (i, 0, 0)),
                  pl.BlockSpec((None, mlen, d), lambda i, j: (i, 0, 1))],
        out_specs=pl.BlockSpec((None, tq, d), lambda i, j: (i, j, 0)),
        out_shape=jax.ShapeDtypeStruct((b, s, d), F32),
        compiler_params=_cparams("parallel", "parallel"), name="cross_attn",
    )(q, kv, kv)


def _router_kernel(x_ref, w_ref, b_ref, o_ref):
    o_ref[...] = jnp.dot(x_ref[...], w_ref[...], precision=lax.Precision.HIGHEST,
                         preferred_element_type=F32) + b_ref[...]


def _router_logits(x, w, b, *, tm=512):
    m, d = x.shape
    e = w.shape[1]
    return pl.pallas_call(
        _router_kernel, grid=(m // tm,),
        in_specs=[pl.BlockSpec((tm, d), lambda i: (i, 0)),
                  pl.BlockSpec((d, e), lambda i: (0, 0)),
                  pl.BlockSpec((1, e), lambda i: (0, 0))],
        out_specs=pl.BlockSpec((tm, e), lambda i: (i, 0)),
        out_shape=jax.ShapeDtypeStruct((m, e), F32),
        compiler_params=_cparams("parallel"), name="router",
    )(x, w, b.reshape(1, e))


W1_PREP_COLS = 512


def _w1_split_kernel(w_ref, g_ref, l_ref):
    w = w_ref[...].astype(BF16)
    cin, cout = w.shape[1], w.shape[1] // 2
    src = lax.broadcasted_iota(jnp.int32, (cin, cout), 0)
    dst = lax.broadcasted_iota(jnp.int32, (cin, cout), 1)
    sel_g = (src == 2 * dst).astype(BF16)
    sel_l = (src == 2 * dst + 1).astype(BF16)
    g_ref[...] = jnp.dot(w, sel_g, preferred_element_type=F32).astype(BF16)
    l_ref[...] = jnp.dot(w, sel_l, preferred_element_type=F32).astype(BF16)


def _split_w1(w1):
    e, d, f2 = w1.shape
    cin = W1_PREP_COLS
    out = jax.ShapeDtypeStruct((e, d, f2 // 2), BF16)
    ospec = pl.BlockSpec((None, d, cin // 2), lambda i, j: (i, 0, j))
    return pl.pallas_call(
        _w1_split_kernel, grid=(e, f2 // cin),
        in_specs=[pl.BlockSpec((None, d, cin), lambda i, j: (i, 0, j))],
        out_specs=[ospec, ospec], out_shape=[out, out],
        compiler_params=_cparams("parallel", "parallel"), name="moe_w1_split",
    )(w1)


def _moe_kernel(be_ref, nused_ref, x_ref, g_ref, w1g_ref, w1l_ref, b1g_ref, b1l_ref, w2_ref,
                b2_ref, o_ref, w2b_ref):
    i = pl.program_id(0)

    @pl.when((i == 0) | (be_ref[i] != be_ref[jnp.maximum(i - 1, 0)]))
    def _():
        w2b_ref[...] = w2_ref[...].astype(BF16)

    @pl.when(i < nused_ref[0])
    def _():
        x = x_ref[...]
        glu = jnp.dot(x, w1g_ref[...], preferred_element_type=F32) + b1g_ref[...]
        lin = jnp.dot(x, w1l_ref[...], preferred_element_type=F32) + b1l_ref[...]
        glu = jnp.minimum(glu, SWIGLU_LIMIT)
        lin = jnp.clip(lin, -SWIGLU_LIMIT, SWIGLU_LIMIT)
        act = glu * jax.nn.sigmoid(SWIGLU_ALPHA * glu) * (lin + 1.0)
        y = jnp.dot(act.astype(BF16), w2b_ref[...], preferred_element_type=F32) + b2_ref[...]
        o_ref[...] = y * g_ref[...]

    @pl.when(i >= nused_ref[0])
    def _():
        o_ref[...] = jnp.zeros_like(o_ref)


def _moe_experts(blk_exp, n_used, xs, row_gate, w1g, w1l, b1g, b1l, w2, b2):
    rows, d = xs.shape
    nblk = rows // MOE_BLK
    f = w1g.shape[-1]
    wspec = lambda shape: pl.BlockSpec(shape, lambda i, be, nu: (be[i], 0, 0),
                                       pipeline_mode=pl.Buffered(1))
    grid_spec = pltpu.PrefetchScalarGridSpec(
        num_scalar_prefetch=2, grid=(nblk,),
        in_specs=[pl.BlockSpec((MOE_BLK, d), lambda i, be, nu: (i, 0)),
                  pl.BlockSpec((MOE_BLK, 1), lambda i, be, nu: (i, 0)),
                  wspec((None, d, f)), wspec((None, d, f)),
                  wspec((None, 1, f)), wspec((None, 1, f)),
                  wspec((None, f, d)), wspec((None, 1, d))],
        out_specs=pl.BlockSpec((MOE_BLK, d), lambda i, be, nu: (i, 0)),
        scratch_shapes=[pltpu.VMEM((f, d), BF16)],
    )
    return pl.pallas_call(
        _moe_kernel, grid_spec=grid_spec,
        out_shape=jax.ShapeDtypeStruct((rows, d), F32),
        compiler_params=_cparams("arbitrary"), name="moe_experts",
    )(blk_exp, n_used, xs, row_gate.reshape(rows, 1), w1g, w1l, b1g, b1l, w2, b2)


def _moe_ffn(x2d, router_w, router_b, w1, b1, w2, b2):
    n, d = x2d.shape
    logits = _router_logits(x2d, router_w, router_b)
    top_val, top_idx = lax.top_k(logits, TOP_K)
    top_w = jax.nn.softmax(top_val, axis=-1)
    flat_e = top_idx.reshape(-1)
    order = jnp.argsort(flat_e)
    e_sorted = flat_e[order]
    tok_sorted = (order // TOP_K).astype(jnp.int32)
    gate_sorted = top_w.reshape(-1)[order]
    counts = jnp.bincount(flat_e, length=N_EXPERTS)
    starts = jnp.cumsum(counts) - counts
    padded = (counts + MOE_BLK - 1) // MOE_BLK * MOE_BLK
    pends = jnp.cumsum(padded)
    pstarts = pends - padded
    dest = (pstarts[e_sorted] + jnp.arange(n * TOP_K) - starts[e_sorted]).astype(jnp.int32)
    rows = n * TOP_K + N_EXPERTS * MOE_BLK
    nblk = rows // MOE_BLK
    row_tok = jnp.full((rows,), n, jnp.int32).at[dest].set(tok_sorted)
    row_gate = jnp.zeros((rows,), F32).at[dest].set(gate_sorted)
    blk_exp = jnp.minimum(jnp.searchsorted(pends, jnp.arange(nblk) * MOE_BLK, side='right'),
                          N_EXPERTS - 1).astype(jnp.int32)
    n_used = (pends[-1] // MOE_BLK).astype(jnp.int32).reshape(1)
    xpad = jnp.concatenate([x2d.astype(BF16), jnp.zeros((1, d), BF16)], axis=0)
    xs = xpad[row_tok]
    w1g, w1l = _split_w1(w1)
    b1g = b1[:, None, 0::2]
    b1l = b1[:, None, 1::2]
    ys = _moe_experts(blk_exp, n_used, xs, row_gate, w1g, w1l, b1g, b1l, w2,
                      b2[:, None, :])
    pos = jnp.zeros((n * TOP_K,), jnp.int32).at[order].set(dest)
    return ys[pos].reshape(n, TOP_K, d).sum(axis=1)


def _token_shift(z, mix):
    prev = jnp.pad(z, ((0, 0), (1, 0), (0, 0)))[:, :-1]
    return z + (prev - z) * mix


def _to_scan_layout(t, bsz, s):
    return t.reshape(bsz, s, RWKV_HEADS, RWKV_HEAD).transpose(1, 3, 0, 2).reshape(
        s, RWKV_HEAD, bsz * RWKV_HEADS)


def _from_scan_layout(t, bsz, s):
    return t.reshape(s, RWKV_HEAD, bsz, RWKV_HEADS).transpose(2, 0, 3, 1).reshape(bsz, s, D_MODEL)


def _rwkv7_branch(p_rkv, p_lora, shift_mix, w0, w_up, a0, a_up, g_up, k_k, k_a, r_k, lnx_g, lnx_b):
    bsz, s, _ = p_rkv.shape
    n_tok = bsz * s
    z = _token_shift(p_rkv, shift_mix[:3 * D_MODEL])
    zl = _token_shift(p_lora, shift_mix[3 * D_MODEL:])
    r, k, v = jnp.split(z, 3, axis=-1)
    wd = zl[..., :DECAY_LORA]
    ad = zl[..., DECAY_LORA:DECAY_LORA + AAA_LORA]
    gd = zl[..., DECAY_LORA + AAA_LORA:]
    mm = lambda t, wgt, name: _matmul(t.reshape(n_tok, -1), wgt, tm=1024, tn=1024,
                                      name=name).reshape(bsz, s, -1)
    w = -jax.nn.softplus(-(w0 + mm(jnp.tanh(wd), w_up, "rwkv_w_up"))) - 0.5
    decay = jnp.exp(-jnp.exp(w))
    a = jax.nn.sigmoid(a0 + mm(ad, a_up, "rwkv_a_up"))
    g = mm(jax.nn.sigmoid(gd), g_up, "rwkv_g_up")
    heads = lambda t: t.reshape(bsz, s, RWKV_HEADS, RWKV_HEAD)
    kk = heads(k * k_k)
    kk = kk / jnp.maximum(jnp.sqrt(jnp.sum(kk * kk, axis=-1, keepdims=True)), 1e-12)
    kk = kk.reshape(bsz, s, D_MODEL)
    k = k * (1.0 + (a - 1.0) * k_a)
    lay = lambda t: _to_scan_layout(t, bsz, s)
    y = _rwkv_scan(lay(r), lay(decay), lay(k), lay(v), lay(-kk), lay(kk * a))
    y = heads(_from_scan_layout(y, bsz, s))
    mu = jnp.mean(y, axis=-1, keepdims=True)
    var = jnp.mean(jnp.square(y - mu), axis=-1, keepdims=True)
    y = ((y - mu) * lax.rsqrt(var + GN_EPS)).reshape(bsz, s, D_MODEL) * lnx_g + lnx_b
    bonus = jnp.sum(heads(r) * heads(k) * r_k, axis=-1, keepdims=True) * heads(v)
    y = y + bonus.reshape(bsz, s, D_MODEL)
    return y * g


def _hybrid_mixer(x, w_in, shift_mix, w0, w_up, a0, a_up, g_up, k_k, k_a, r_k, lnx_g, lnx_b,
                  proj_a, proj_b, w_out):
    bsz, s, d = x.shape
    n_tok = bsz * s
    x2d = x.reshape(n_tok, d)
    c0, c1, c2 = 3 * D_MODEL, RWKV_COLS, RWKV_COLS + ATTN_COLS
    p_rkv = _matmul(x2d, w_in[:, :c0], tm=512, tn=1024, name="in_rkv").reshape(bsz, s, -1)
    p_lora = _matmul(x2d, w_in[:, c0:c1], tm=1024, tn=512, name="in_lora").reshape(bsz, s, -1)
    p_attn = _matmul(x2d, w_in[:, c1:c2], tm=512, tn=1536, name="in_attn").reshape(bsz, s, -1)
    p_gate = _matmul(x2d, w_in[:, c2:], tm=512, tn=1024, name="in_gate").reshape(bsz, s, -1)

    y_a = _rwkv7_branch(p_rkv, p_lora, shift_mix, w0, w_up, a0, a_up, g_up, k_k, k_a, r_k,
                        lnx_g, lnx_b)

    cos_t, sin_t = _rotary_tables(s)
    qk = _rotary(p_attn[..., :2 * DIL_DIM], cos_t, sin_t)
    qkv = jnp.concatenate([qk, p_attn[..., 2 * DIL_DIM:]], axis=-1)
    outs, lses = [], []
    for gi, (window, dilation) in enumerate(DIL_GROUPS):
        o, l = _dilated_attention(qkv, gi, window, dilation)
        outs.append(o)
        lses.append(l)
    mix = jax.nn.softmax(jnp.stack(lses), axis=0)
    y_b = jnp.sum(mix * jnp.stack(outs), axis=0)

    gates = jax.nn.sigmoid(p_gate)
    pa = _matmul(y_a.reshape(n_tok, d), proj_a, tm=512, tn=1024, name="proj_a")
    pb = _matmul(y_b.reshape(n_tok, DIL_GROUP_W), proj_b, tm=512, tn=1024, name="proj_b")
    merged = gates.reshape(n_tok, -1)[:, :d] * pa + gates.reshape(n_tok, -1)[:, d:] * pb
    return _matmul(merged, w_out, tm=512, tn=1024, name="w_out")


def _memory_cross_attention(x2d, mem, wq, wkv, wo, bsz, s):
    d = x2d.shape[-1]
    q = _matmul(x2d, wq, tm=512, tn=1024, name="ca_q").reshape(bsz, s, d)
    kv = _matmul(mem.reshape(-1, d), wkv, tm=512, tn=1024, name="ca_kv").reshape(bsz, -1, 2 * d)
    o = _cross_attention(q, kv)
    return _matmul(o.reshape(-1, d), wo, tm=512, tn=1024, name="ca_o")


def kernel(x, mem, w_in, shift_mix, w0, w_up, a0, a_up, g_up, k_k, k_a, r_k, lnx_g, lnx_b, proj_a, proj_b, w_out, ln1_g, ln1_b, ca_wq, ca_wkv, ca_wo, ln2_g, ln2_b, router_w, router_b, moe_w1, moe_b1, moe_w2, moe_b2, ln3_g, ln3_b):
    bsz, s, d = x.shape
    n_tok = bsz * s
    for l in range(DEPTH):
        h = _hybrid_mixer(x, w_in[l], shift_mix[l], w0[l], w_up[l], a0[l], a_up[l], g_up[l],
                          k_k[l], k_a[l], r_k[l], lnx_g[l], lnx_b[l], proj_a[l], proj_b[l],
                          w_out[l])
        x1 = _res_layer_norm(x.reshape(n_tok, d), h, ln1_g[l], ln1_b[l], name="ln1")
        h = _memory_cross_attention(x1, mem, ca_wq[l], ca_wkv[l], ca_wo[l], bsz, s)
        x2 = _res_layer_norm(x1, h, ln2_g[l], ln2_b[l], name="ln2")
        h = _moe_ffn(x2, router_w[l], router_b[l], moe_w1[l], moe_b1[l], moe_w2[l], moe_b2[l])
        x = _res_layer_norm(x2, h, ln3_g[l], ln3_b[l], name="ln3").reshape(bsz, s, d)
    return x
```

```python
import functools

import jax
import jax.numpy as jnp
import numpy as np
from jax import lax
from jax.experimental import pallas as pl
from jax.experimental.pallas import tpu as pltpu

F32 = jnp.float32
BF16 = jnp.bfloat16

D_MODEL = 2048
RWKV_HEAD = 64
RWKV_HEADS = D_MODEL // RWKV_HEAD
DECAY_LORA = 96
AAA_LORA = 96
GATE_LORA = 256
GN_EPS = 64e-5
DIL_GROUPS = ((128, 1), (512, 4), (2048, 16))
N_GROUPS = len(DIL_GROUPS)
DIL_HEADS = 8
DIL_HEAD_DIM = 64
DIL_GROUP_W = DIL_HEADS * DIL_HEAD_DIM
DIL_DIM = N_GROUPS * DIL_GROUP_W
BLK = 128
ROPE_THETA = 10000.0
NEG_INF = -1e30
RWKV_COLS = 3 * D_MODEL + DECAY_LORA + AAA_LORA + GATE_LORA
ATTN_COLS = 3 * DIL_DIM
CA_HEADS = 4
CA_HEAD_DIM = D_MODEL // CA_HEADS
N_EXPERTS = 32
TOP_K = 4
D_FF = D_MODEL
SWIGLU_LIMIT = 7.0
SWIGLU_ALPHA = 1.702
MOE_BLK = 128
LN_EPS = 1e-5
DEPTH = 1
DEEPNORM_ALPHA = (2 * DEPTH) ** 0.25

V7X_VMEM_LIMIT_BYTES = 56 * 1024 * 1024


def _cparams(*sem):
    return pltpu.CompilerParams(dimension_semantics=sem, vmem_limit_bytes=V7X_VMEM_LIMIT_BYTES)


def _mm_kernel(a_ref, b_ref, o_ref):
    o_ref[...] = jnp.dot(a_ref[...].astype(BF16), b_ref[...].astype(BF16),
                         preferred_element_type=F32).astype(o_ref.dtype)


def _matmul(a, b, *, tm, tn, name, out_dtype=F32):
    m, k = a.shape
    _, n = b.shape
    tm, tn = min(tm, m), min(tn, n)
    assert m % tm == 0 and n % tn == 0, (m, n, tm, tn)
    return pl.pallas_call(
        _mm_kernel,
        grid=(n // tn, m // tm),
        in_specs=[pl.BlockSpec((tm, k), lambda j, i: (i, 0)),
                  pl.BlockSpec((k, tn), lambda j, i: (0, j))],
        out_specs=pl.BlockSpec((tm, tn), lambda j, i: (i, j)),
        out_shape=jax.ShapeDtypeStruct((m, n), out_dtype),
        compiler_params=_cparams("parallel", "parallel"),
        name=name,
    )(a, b)


def _ln_kernel(x_ref, h_ref, g_ref, b_ref, o_ref):
    t = DEEPNORM_ALPHA * x_ref[...] + h_ref[...]
    mu = jnp.mean(t, axis=-1, keepdims=True)
    c = t - mu
    var = jnp.mean(c * c, axis=-1, keepdims=True)
    o_ref[...] = c * lax.rsqrt(var + LN_EPS) * g_ref[...] + b_ref[...]


def _res_layer_norm(x, h, g, b, *, name, tm=256):
    m, d = x.shape
    assert m % tm == 0
    row = pl.BlockSpec((tm, d), lambda i: (i, 0))
    vec = pl.BlockSpec((1, d), lambda i: (0, 0))
    return pl.pallas_call(
        _ln_kernel, grid=(m // tm,), in_specs=[row, row, vec, vec], out_specs=row,
        out_shape=jax.ShapeDtypeStruct((m, d), F32),
        compiler_params=_cparams("parallel"), name=name,
    )(x, h, g.reshape(1, d), b.reshape(1, d))


def _rotary_kernel(x_ref, cos_ref, sin_ref, o_ref):
    cos = cos_ref[...]
    sin = sin_ref[...]
    lane = lax.broadcasted_iota(jnp.int32, cos.shape, 1)
    first_half = (lane % DIL_HEAD_DIM) < (DIL_HEAD_DIM // 2)
    for c in range(x_ref.shape[-1] // 128):
        x = x_ref[:, c * 128:(c + 1) * 128]
        partner = jnp.where(first_half, pltpu.roll(x, 128 - 32, 1), pltpu.roll(x, 32, 1))
        o_ref[:, c * 128:(c + 1) * 128] = x * cos + partner * sin


def _rotary(x, cos_t, sin_t, *, ts=512):
    b, s, w = x.shape
    ts = min(ts, s)
    assert s % ts == 0 and w % 128 == 0
    return pl.pallas_call(
        _rotary_kernel, grid=(b, s // ts),
        in_specs=[pl.BlockSpec((None, ts, w), lambda i, j: (i, j, 0)),
                  pl.BlockSpec((ts, 128), lambda i, j: (j, 0)),
                  pl.BlockSpec((ts, 128), lambda i, j: (j, 0))],
        out_specs=pl.BlockSpec((None, ts, w), lambda i, j: (i, j, 0)),
        out_shape=jax.ShapeDtypeStruct(x.shape, F32),
        compiler_params=_cparams("parallel", "parallel"), name="rotary",
    )(x, cos_t, sin_t)


def _rotary_tables(s):
    half = DIL_HEAD_DIM // 2
    inv = ROPE_THETA ** (-jnp.arange(half, dtype=F32) * 2.0 / DIL_HEAD_DIM)
    ang = jnp.arange(s, dtype=F32)[:, None] * inv[None, :]
    cos, sin = jnp.cos(ang), jnp.sin(ang)
    cos_t = jnp.concatenate([cos, cos, cos, cos], axis=-1)
    sin_t = jnp.concatenate([-sin, sin, -sin, sin], axis=-1)
    return cos_t, sin_t


def _dil_attn_kernel(q_ref, kp_ref, kc_ref, vp_ref, vc_ref, o_ref, l_ref, *, span):
    nb = pl.program_id(2)
    qi = lax.broadcasted_iota(jnp.int32, (BLK, 2 * BLK), 0) + BLK
    ki = lax.broadcasted_iota(jnp.int32, (BLK, 2 * BLK), 1)
    dist = qi - ki
    mask = (dist >= 0) & (dist <= span) & ((nb > 0) | (ki >= BLK))
    scale = DIL_HEAD_DIM ** -0.5
    q = q_ref[...].astype(BF16)
    k = jnp.concatenate([kp_ref[...], kc_ref[...]], axis=0).astype(BF16)
    v = jnp.concatenate([vp_ref[...], vc_ref[...]], axis=0).astype(BF16)
    for h in range(DIL_HEADS):
        sl = slice(h * DIL_HEAD_DIM, (h + 1) * DIL_HEAD_DIM)
        s = lax.dot_general(q[:, sl], k[:, sl], (((1,), (1,)), ((), ())),
                            preferred_element_type=F32) * scale
        s = jnp.where(mask, s, NEG_INF)
        m = jnp.max(s, axis=-1, keepdims=True)
        p = jnp.exp(s - m)
        den = jnp.sum(p, axis=-1, keepdims=True)
        o = jnp.dot(p.astype(BF16), v[:, sl], preferred_element_type=F32)
        o_ref[:, sl] = o / den
        l_ref[:, sl] = jnp.broadcast_to(m + jnp.log(den), (BLK, DIL_HEAD_DIM))


def _dilated_attention(qkv, gi, window, dilation):
    b, s, w = qkv.shape
    d = dilation
    assert s % (d * BLK) == 0
    n = s // d
    nblk = n // BLK
    cpr = w // DIL_GROUP_W
    view = qkv.reshape(b, n, d * w)
    blk = (None, BLK, DIL_GROUP_W)

    def col(which):
        return lambda bi, r, nb: (bi, nb, r * cpr + which * N_GROUPS + gi)

    def col_prev(which):
        return lambda bi, r, nb: (bi, jnp.maximum(nb - 1, 0), r * cpr + which * N_GROUPS + gi)

    out_spec = pl.BlockSpec(blk, lambda bi, r, nb: (bi, nb, r))
    o, l = pl.pallas_call(
        functools.partial(_dil_attn_kernel, span=window // dilation),
        grid=(b, d, nblk),
        in_specs=[pl.BlockSpec(blk, col(0)),
                  pl.BlockSpec(blk, col_prev(1)), pl.BlockSpec(blk, col(1)),
                  pl.BlockSpec(blk, col_prev(2)), pl.BlockSpec(blk, col(2))],
        out_specs=[out_spec, out_spec],
        out_shape=[jax.ShapeDtypeStruct((b, n, d * DIL_GROUP_W), F32)] * 2,
        compiler_params=_cparams("parallel", "parallel", "arbitrary"),
        name=f"dil_attn_g{gi}",
    )(view, view, view, view, view)
    return o.reshape(b, s, DIL_GROUP_W), l.reshape(b, s, DIL_GROUP_W)


def _rwkv_scan_kernel(r_ref, w_ref, k_ref, v_ref, a_ref, b_ref, y_ref, s_ref, *, tc):
    n = RWKV_HEAD

    @pl.when(pl.program_id(0) == 0)
    def _():
        s_ref[...] = jnp.zeros_like(s_ref)

    def row(ref, t, j):
        return ref[t, pl.ds(j, 1), :]

    def step(t, carry):
        acc = [jnp.zeros(s_ref.shape[1:], F32), jnp.zeros(s_ref.shape[1:], F32)]
        for j in range(n):
            acc[j % 2] = acc[j % 2] + s_ref[j] * row(a_ref, t, j)
        sa = acc[0] + acc[1]
        vt = v_ref[t]
        yac = [jnp.zeros(s_ref.shape[1:], F32), jnp.zeros(s_ref.shape[1:], F32)]
        for j in range(n):
            sj = s_ref[j] * row(w_ref, t, j) + sa * row(b_ref, t, j) + vt * row(k_ref, t, j)
            s_ref[j] = sj
            yac[j % 2] = yac[j % 2] + sj * row(r_ref, t, j)
        y_ref[t] = yac[0] + yac[1]
        return carry

    lax.fori_loop(0, tc, step, 0)


def _rwkv_scan(r, w, k, v, a, b, *, tc=32):
    s, n, l = r.shape
    tc = min(tc, s)
    assert s % tc == 0
    blk = pl.BlockSpec((tc, n, l), lambda i: (i, 0, 0))
    return pl.pallas_call(
        functools.partial(_rwkv_scan_kernel, tc=tc),
        grid=(s // tc,), in_specs=[blk] * 6, out_specs=blk,
        out_shape=jax.ShapeDtypeStruct((s, n, l), F32),
        scratch_shapes=[pltpu.VMEM((n, n, l), F32)],
        compiler_params=_cparams("arbitrary"), name="rwkv_scan",
    )(r, w, k, v, a, b)


RWKV_SEG = 256
RWKV_PREP_ROWS = 128
SUBLANES = 8


def _head_sum(x):
    seg = RWKV_SEG
    src = lax.broadcasted_iota(jnp.int32, (seg, seg), 0) // RWKV_HEAD
    dst = lax.broadcasted_iota(jnp.int32, (seg, seg), 1) // RWKV_HEAD
    ones = (src == dst).astype(BF16)
    hi = x.astype(BF16)
    lo = (x - hi.astype(F32)).astype(BF16)
    parts = []
    for c in range(x.shape[-1] // seg):
        sl = slice(c * seg, (c + 1) * seg)
        parts.append(jnp.dot(hi[:, sl], ones, preferred_element_type=F32)
                     + jnp.dot(lo[:, sl], ones, preferred_element_type=F32))
    return jnp.concatenate(parts, axis=-1)


def _token_shift_rows(p, last_prev_row, mix, first):
    rolled = pltpu.roll(p, 1, 0)
    row0 = jnp.where(first, jnp.zeros_like(last_prev_row), last_prev_row)
    t = lax.broadcasted_iota(jnp.int32, p.shape, 0)
    prev = jnp.where(t == 0, row0, rolled)
    return p + (prev - p) * mix


def _rwkv_prep_kernel(rkv_ref, rkvp_ref, lo_ref, lop_ref, mixr_ref, mixl_ref, w0_ref, a0_ref,
                      kk_ref, ka_ref, rk_ref, wup_ref, aup_ref, gup_ref,
                      r_ref, w_ref, k_ref, v_ref, al_ref, be_ref, g_ref, bo_ref):
    first = pl.program_id(1) == 0
    d = D_MODEL

    def seg(c):
        sl = slice(c * d, (c + 1) * d)
        return _token_shift_rows(rkv_ref[:, sl], rkvp_ref[SUBLANES - 1:SUBLANES, sl],
                                 mixr_ref[:, sl], first)

    r, k, v = seg(0), seg(1), seg(2)
    zl = _token_shift_rows(lo_ref[...], lop_ref[SUBLANES - 1:SUBLANES, :], mixl_ref[...], first)
    wd = zl[:, :DECAY_LORA]
    ad = zl[:, DECAY_LORA:DECAY_LORA + AAA_LORA]
    gd = zl[:, DECAY_LORA + AAA_LORA:]
    mm = lambda a, w_ref_: jnp.dot(a.astype(BF16), w_ref_[...].astype(BF16),
                                   preferred_element_type=F32)
    z = -(w0_ref[...] + mm(jnp.tanh(wd), wup_ref))
    softplus = jnp.maximum(z, 0.0) + jnp.log1p(jnp.exp(-jnp.abs(z)))
    w = -softplus - 0.5
    a = jax.nn.sigmoid(a0_ref[...] + mm(ad, aup_ref))
    kk = k * kk_ref[...]
    kk = kk / jnp.maximum(jnp.sqrt(_head_sum(kk * kk)), 1e-12)
    k2 = k * (1.0 + (a - 1.0) * ka_ref[...])
    r_ref[...] = r
    w_ref[...] = jnp.exp(-jnp.exp(w))
    k_ref[...] = k2
    v_ref[...] = v
    al_ref[...] = -kk
    be_ref[...] = kk * a
    g_ref[...] = mm(jax.nn.sigmoid(gd), gup_ref)
    bo_ref[...] = _head_sum(r * k2 * rk_ref[...]) * v


def _rwkv_prep(p_rkv, p_lora, shift_mix, w0, w_up, a0, a_up, g_up, k_k, k_a, r_k):
    b, s, c_rkv = p_rkv.shape
    d, ts, c_lo = D_MODEL, min(RWKV_PREP_ROWS, s), p_lora.shape[-1]
    assert s % ts == 0 and ts % SUBLANES == 0
    cur = lambda c: pl.BlockSpec((None, ts, c), lambda i, j: (i, j, 0))
    prev = lambda c: pl.BlockSpec(
        (None, SUBLANES, c), lambda i, j: (i, jnp.maximum(j * (ts // SUBLANES) - 1, 0), 0))
    vec = lambda c: pl.BlockSpec((1, c), lambda i, j: (0, 0))
    full = lambda a: pl.BlockSpec(a.shape, lambda i, j: (0, 0))
    out = jax.ShapeDtypeStruct((b, s, d), F32)
    row = lambda t: t.reshape(1, -1)
    return pl.pallas_call(
        _rwkv_prep_kernel, grid=(b, s // ts),
        in_specs=[cur(c_rkv), prev(c_rkv), cur(c_lo), prev(c_lo), vec(c_rkv), vec(c_lo),
                  vec(d), vec(d), vec(d), vec(d), vec(d), full(w_up), full(a_up), full(g_up)],
        out_specs=[cur(d)] * 8, out_shape=[out] * 8,
        compiler_params=_cparams("parallel", "parallel"), name="rwkv_prep",
    )(p_rkv, p_rkv, p_lora, p_lora, row(shift_mix[:c_rkv]), row(shift_mix[c_rkv:]), row(w0),
      row(a0), row(k_k), row(k_a), row(r_k), w_up, a_up, g_up)


def _proj_a_kernel(y_ref, bo_ref, g_ref, lg_ref, lb_ref, w_ref, o_ref):
    y = y_ref[...]
    inv_n = 1.0 / RWKV_HEAD
    c = y - _head_sum(y) * inv_n
    var = _head_sum(c * c) * inv_n
    ya = (c * lax.rsqrt(var + GN_EPS) * lg_ref[...] + lb_ref[...] + bo_ref[...]) * g_ref[...]
    o_ref[...] = jnp.dot(ya.astype(BF16), w_ref[...], preferred_element_type=F32)


def _rwkv_post_proj(y, bonus, gate, lnx_g, lnx_b, proj_a, *, tm=256):
    m, d = y.shape
    assert m % tm == 0
    rowb = pl.BlockSpec((tm, d), lambda i: (i, 0))
    vec = pl.BlockSpec((1, d), lambda i: (0, 0))
    return pl.pallas_call(
        _proj_a_kernel, grid=(m // tm,),
        in_specs=[rowb, rowb, rowb, vec, vec, pl.BlockSpec((d, d), lambda i: (0, 0))],
        out_specs=rowb, out_shape=jax.ShapeDtypeStruct((m, d), F32),
        compiler_params=_cparams("parallel"), name="rwkv_post_proj_a",
    )(y, bonus, gate, lnx_g.reshape(1, d), lnx_b.reshape(1, d), proj_a.astype(BF16))


def _cross_attn_kernel(q_ref, k_ref, v_ref, o_ref):
    scale = CA_HEAD_DIM ** -0.5
    for h in range(CA_HEADS):
        sl = slice(h * CA_HEAD_DIM, (h + 1) * CA_HEAD_DIM)
        q = q_ref[:, sl].astype(BF16)
        k = k_ref[:, sl].astype(BF16)
        s = lax.dot_general(q, k, (((1,), (1,)), ((), ())), preferred_element_type=F32) * scale
        m = jnp.max(s, axis=-1, keepdims=True)
        p = jnp.exp(s - m)
        den = jnp.sum(p, axis=-1, keepdims=True)
        o = jnp.dot(p.astype(BF16), v_ref[:, sl].astype(BF16), preferred_element_type=F32)
        o_ref[:, sl] = o / den


def _cross_attention(q, kv, *, tq=512):
    b, s, d = q.shape
    mlen = kv.shape[1]
    tq = min(tq, s)
    return pl.pallas_call(
        _cross_attn_kernel, grid=(b, s // tq),
        in_specs=[pl.BlockSpec((None, tq, d), lambda i, j: (i, j, 0)),
                  pl.BlockSpec((None, mlen, d), lambda i, j: (i, 0, 0)),
                  pl.BlockSpec((None, mlen, d), lambda i, j: (i, 0, 1))],
        out_specs=pl.BlockSpec((None, tq, d), lambda i, j: (i, j, 0)),
        out_shape=jax.ShapeDtypeStruct((b, s, d), F32),
        compiler_params=_cparams("parallel", "parallel"), name="cross_attn",
    )(q, kv, kv)


def _router_kernel(x_ref, w_ref, b_ref, o_ref):
    o_ref[...] = jnp.dot(x_ref[...], w_ref[...], precision=lax.Precision.HIGHEST,
                         preferred_element_type=F32) + b_ref[...]


def _router_logits(x, w, b, *, tm=512):
    m, d = x.shape
    e = w.shape[1]
    return pl.pallas_call(
        _router_kernel, grid=(m // tm,),
        in_specs=[pl.BlockSpec((tm, d), lambda i: (i, 0)),
                  pl.BlockSpec((d, e), lambda i: (0, 0)),
                  pl.BlockSpec((1, e), lambda i: (0, 0))],
        out_specs=pl.BlockSpec((tm, e), lambda i: (i, 0)),
        out_shape=jax.ShapeDtypeStruct((m, e), F32),
        compiler_params=_cparams("parallel"), name="router",
    )(x, w, b.reshape(1, e))


W1_PREP_COLS = 1024
W1_PREP_SUB = 256


def _w1_split_kernel(w_ref, g_ref, l_ref):
    sub, half = W1_PREP_SUB, W1_PREP_SUB // 2
    src = lax.broadcasted_iota(jnp.int32, (sub, sub), 0)
    dst = lax.broadcasted_iota(jnp.int32, (sub, sub), 1)
    want = jnp.where(dst < half, 2 * dst, 2 * (dst - half) + 1)
    sel = (src == want).astype(BF16)
    for q in range(w_ref.shape[1] // sub):
        w = w_ref[:, q * sub:(q + 1) * sub].astype(BF16)
        r = jnp.dot(w, sel, preferred_element_type=F32).astype(BF16)
        g_ref[:, q * half:(q + 1) * half] = r[:, :half]
        l_ref[:, q * half:(q + 1) * half] = r[:, half:]


def _split_w1(w1):
    e, d, f2 = w1.shape
    cin = W1_PREP_COLS
    out = jax.ShapeDtypeStruct((e, d, f2 // 2), BF16)
    ospec = pl.BlockSpec((None, d, cin // 2), lambda i, j: (i, 0, j))
    return pl.pallas_call(
        _w1_split_kernel, grid=(e, f2 // cin),
        in_specs=[pl.BlockSpec((None, d, cin), lambda i, j: (i, 0, j))],
        out_specs=[ospec, ospec], out_shape=[out, out],
        compiler_params=_cparams("parallel", "parallel"), name="moe_w1_split",
    )(w1)


def _moe_kernel(be_ref, nused_ref, x_ref, g_ref, w1g_ref, w1l_ref, b1g_ref, b1l_ref, w2_ref,
                b2_ref, o_ref, w2b_ref):
    i = pl.program_id(0)

    @pl.when((i == 0) | (be_ref[i] != be_ref[jnp.maximum(i - 1, 0)]))
    def _():
        w2b_ref[...] = w2_ref[...].astype(BF16)

    @pl.when(i < nused_ref[0])
    def _():
        x = x_ref[...]
        glu = jnp.dot(x, w1g_ref[...], preferred_element_type=F32) + b1g_ref[...]
        lin = jnp.dot(x, w1l_ref[...], preferred_element_type=F32) + b1l_ref[...]
        glu = jnp.minimum(glu, SWIGLU_LIMIT)
        lin = jnp.clip(lin, -SWIGLU_LIMIT, SWIGLU_LIMIT)
        act = glu * jax.nn.sigmoid(SWIGLU_ALPHA * glu) * (lin + 1.0)
        y = jnp.dot(act.astype(BF16), w2b_ref[...], preferred_element_type=F32) + b2_ref[...]
        o_ref[...] = y * g_ref[...]

    @pl.when(i >= nused_ref[0])
    def _():
        o_ref[...] = jnp.zeros_like(o_ref)


def _moe_experts(blk_exp, n_used, xs, row_gate, w1g, w1l, b1g, b1l, w2, b2):
    rows, d = xs.shape
    nblk = rows // MOE_BLK
    f = w1g.shape[-1]
    wspec = lambda shape: pl.BlockSpec(shape, lambda i, be, nu: (be[i], 0, 0),
                                       pipeline_mode=pl.Buffered(1))
    grid_spec = pltpu.PrefetchScalarGridSpec(
        num_scalar_prefetch=2, grid=(nblk,),
        in_specs=[pl.BlockSpec((MOE_BLK, d), lambda i, be, nu: (i, 0)),
                  pl.BlockSpec((MOE_BLK, 1), lambda i, be, nu: (i, 0)),
                  wspec((None, d, f)), wspec((None, d, f)),
                  wspec((None, 1, f)), wspec((None, 1, f)),
                  wspec((None, f, d)), wspec((None, 1, d))],
        out_specs=pl.BlockSpec((MOE_BLK, d), lambda i, be, nu: (i, 0)),
        scratch_shapes=[pltpu.VMEM((f, d), BF16)],
    )
    return pl.pallas_call(
        _moe_kernel, grid_spec=grid_spec,
        out_shape=jax.ShapeDtypeStruct((rows, d), F32),
        compiler_params=_cparams("arbitrary"), name="moe_experts",
    )(blk_exp, n_used, xs, row_gate.reshape(rows, 1), w1g, w1l, b1g, b1l, w2, b2)


def _moe_ffn(x2d, router_w, router_b, w1, b1, w2, b2):
    n, d = x2d.shape
    logits = _router_logits(x2d, router_w, router_b)
    top_val, top_idx = lax.top_k(logits, TOP_K)
    top_w = jax.nn.softmax(top_val, axis=-1)
    flat_e = top_idx.reshape(-1)
    order = jnp.argsort(flat_e)
    e_sorted = flat_e[order]
    tok_sorted = (order // TOP_K).astype(jnp.int32)
    gate_sorted = top_w.reshape(-1)[order]
    counts = jnp.bincount(flat_e, length=N_EXPERTS)
    starts = jnp.cumsum(counts) - counts
    padded = (counts + MOE_BLK - 1) // MOE_BLK * MOE_BLK
    pends = jnp.cumsum(padded)
    pstarts = pends - padded
    dest = (pstarts[e_sorted] + jnp.arange(n * TOP_K) - starts[e_sorted]).astype(jnp.int32)
    rows = n * TOP_K + N_EXPERTS * MOE_BLK
    nblk = rows // MOE_BLK
    row_tok = jnp.full((rows,), n, jnp.int32).at[dest].set(tok_sorted)
    row_gate = jnp.zeros((rows,), F32).at[dest].set(gate_sorted)
    blk_exp = jnp.minimum(jnp.searchsorted(pends, jnp.arange(nblk) * MOE_BLK, side='right'),
                          N_EXPERTS - 1).astype(jnp.int32)
    n_used = (pends[-1] // MOE_BLK).astype(jnp.int32).reshape(1)
    xpad = jnp.concatenate([x2d.astype(BF16), jnp.zeros((1, d), BF16)], axis=0)
    xs = xpad[row_tok]
    w1g, w1l = _split_w1(w1)
    b1g = b1[:, None, 0::2]
    b1l = b1[:, None, 1::2]
    ys = _moe_experts(blk_exp, n_used, xs, row_gate, w1g, w1l, b1g, b1l, w2,
                      b2[:, None, :])
    pos = jnp.zeros((n * TOP_K,), jnp.int32).at[order].set(dest)
    return ys[pos].reshape(n, TOP_K, d).sum(axis=1)


def _to_scan_layout(t, bsz, s):
    return t.reshape(bsz, s, RWKV_HEADS, RWKV_HEAD).transpose(1, 3, 0, 2).reshape(
        s, RWKV_HEAD, bsz * RWKV_HEADS)


def _from_scan_layout(t, bsz, s):
    return t.reshape(s, RWKV_HEAD, bsz, RWKV_HEADS).transpose(2, 0, 3, 1).reshape(bsz, s, D_MODEL)


def _rwkv7_branch_proj(p_rkv, p_lora, shift_mix, w0, w_up, a0, a_up, g_up, k_k, k_a, r_k, lnx_g,
                       lnx_b, proj_a):
    bsz, s, _ = p_rkv.shape
    r, decay, k, v, alpha, beta, gate, bonus = _rwkv_prep(
        p_rkv, p_lora, shift_mix, w0, w_up, a0, a_up, g_up, k_k, k_a, r_k)
    lay = lambda t: _to_scan_layout(t, bsz, s)
    y = _rwkv_scan(lay(r), lay(decay), lay(k), lay(v), lay(alpha), lay(beta))
    y = _from_scan_layout(y, bsz, s)
    flat = lambda t: t.reshape(bsz * s, D_MODEL)
    return _rwkv_post_proj(flat(y), flat(bonus), flat(gate), lnx_g, lnx_b, proj_a)


def _hybrid_mixer(x, w_in, shift_mix, w0, w_up, a0, a_up, g_up, k_k, k_a, r_k, lnx_g, lnx_b,
                  proj_a, proj_b, w_out):
    bsz, s, d = x.shape
    n_tok = bsz * s
    x2d = x.reshape(n_tok, d)
    c0, c1, c2 = 3 * D_MODEL, RWKV_COLS, RWKV_COLS + ATTN_COLS
    p_rkv = _matmul(x2d, w_in[:, :c0], tm=512, tn=1024, name="in_rkv").reshape(bsz, s, -1)
    p_lora = _matmul(x2d, w_in[:, c0:c1], tm=1024, tn=512, name="in_lora").reshape(bsz, s, -1)
    p_attn = _matmul(x2d, w_in[:, c1:c2], tm=512, tn=1536, name="in_attn").reshape(bsz, s, -1)
    p_gate = _matmul(x2d, w_in[:, c2:], tm=512, tn=1024, name="in_gate").reshape(bsz, s, -1)

    pa = _rwkv7_branch_proj(p_rkv, p_lora, shift_mix, w0, w_up, a0, a_up, g_up, k_k, k_a, r_k,
                            lnx_g, lnx_b, proj_a)

    cos_t, sin_t = _rotary_tables(s)
    qk = _rotary(p_attn[..., :2 * DIL_DIM], cos_t, sin_t)
    qkv = jnp.concatenate([qk, p_attn[..., 2 * DIL_DIM:]], axis=-1)
    outs, lses = [], []
    for gi, (window, dilation) in enumerate(DIL_GROUPS):
        o, l = _dilated_attention(qkv, gi, window, dilation)
        outs.append(o)
        lses.append(l)
    mix = jax.nn.softmax(jnp.stack(lses), axis=0)
    y_b = jnp.sum(mix * jnp.stack(outs), axis=0)

    gates = jax.nn.sigmoid(p_gate)
    pb = _matmul(y_b.reshape(n_tok, DIL_GROUP_W), proj_b, tm=512, tn=1024, name="proj_b")
    merged = gates.reshape(n_tok, -1)[:, :d] * pa + gates.reshape(n_tok, -1)[:, d:] * pb
    return _matmul(merged, w_out, tm=512, tn=1024, name="w_out")


def _memory_cross_attention(x2d, mem, wq, wkv, wo, bsz, s):
    d = x2d.shape[-1]
    q = _matmul(x2d, wq, tm=512, tn=1024, name="ca_q").reshape(bsz, s, d)
    kv = _matmul(mem.reshape(-1, d), wkv, tm=512, tn=1024, name="ca_kv").reshape(bsz, -1, 2 * d)
    o = _cross_attention(q, kv)
    return _matmul(o.reshape(-1, d), wo, tm=512, tn=1024, name="ca_o")


def kernel(x, mem, w_in, shift_mix, w0, w_up, a0, a_up, g_up, k_k, k_a, r_k, lnx_g, lnx_b, proj_a, proj_b, w_out, ln1_g, ln1_b, ca_wq, ca_wkv, ca_wo, ln2_g, ln2_b, router_w, router_b, moe_w1, moe_b1, moe_w2, moe_b2, ln3_g, ln3_b):
    bsz, s, d = x.shape
    n_tok = bsz * s
    for l in range(DEPTH):
        h = _hybrid_mixer(x, w_in[l], shift_mix[l], w0[l], w_up[l], a0[l], a_up[l], g_up[l],
                          k_k[l], k_a[l], r_k[l], lnx_g[l], lnx_b[l], proj_a[l], proj_b[l],
                          w_out[l])
        x1 = _res_layer_norm(x.reshape(n_tok, d), h, ln1_g[l], ln1_b[l], name="ln1")
        h = _memory_cross_attention(x1, mem, ca_wq[l], ca_wkv[l], ca_wo[l], bsz, s)
        x2 = _res_layer_norm(x1, h, ln2_g[l], ln2_b[l], name="ln2")
        h = _moe_ffn(x2, router_w[l], router_b[l], moe_w1[l], moe_b1[l], moe_w2[l], moe_b2[l])
        x = _res_layer_norm(x2, h, ln3_g[l], ln3_b[l], name="ln3").reshape(bsz, s, d)
    return x
```

```python
import functools

import jax
import jax.numpy as jnp
import numpy as np
from jax import lax
from jax.experimental import pallas as pl
from jax.experimental.pallas import tpu as pltpu

F32 = jnp.float32
BF16 = jnp.bfloat16

D_MODEL = 2048
RWKV_HEAD = 64
RWKV_HEADS = D_MODEL // RWKV_HEAD
DECAY_LORA = 96
AAA_LORA = 96
GATE_LORA = 256
GN_EPS = 64e-5
DIL_GROUPS = ((128, 1), (512, 4), (2048, 16))
N_GROUPS = len(DIL_GROUPS)
DIL_HEADS = 8
DIL_HEAD_DIM = 64
DIL_GROUP_W = DIL_HEADS * DIL_HEAD_DIM
DIL_DIM = N_GROUPS * DIL_GROUP_W
BLK = 128
ROPE_THETA = 10000.0
NEG_INF = -1e30
RWKV_COLS = 3 * D_MODEL + DECAY_LORA + AAA_LORA + GATE_LORA
ATTN_COLS = 3 * DIL_DIM
CA_HEADS = 4
CA_HEAD_DIM = D_MODEL // CA_HEADS
N_EXPERTS = 32
TOP_K = 4
D_FF = D_MODEL
SWIGLU_LIMIT = 7.0
SWIGLU_ALPHA = 1.702
MOE_BLK = 128
LN_EPS = 1e-5
DEPTH = 1
DEEPNORM_ALPHA = (2 * DEPTH) ** 0.25

V7X_VMEM_LIMIT_BYTES = 56 * 1024 * 1024


def _cparams(*sem):
    return pltpu.CompilerParams(dimension_semantics=sem, vmem_limit_bytes=V7X_VMEM_LIMIT_BYTES)


def _mm_kernel(a_ref, b_ref, o_ref):
    o_ref[...] = jnp.dot(a_ref[...].astype(BF16), b_ref[...].astype(BF16),
                         preferred_element_type=F32).astype(o_ref.dtype)


def _matmul(a, b, *, tm, tn, name, out_dtype=F32):
    m, k = a.shape
    _, n = b.shape
    tm, tn = min(tm, m), min(tn, n)
    assert m % tm == 0 and n % tn == 0, (m, n, tm, tn)
    return pl.pallas_call(
        _mm_kernel,
        grid=(n // tn, m // tm),
        in_specs=[pl.BlockSpec((tm, k), lambda j, i: (i, 0)),
                  pl.BlockSpec((k, tn), lambda j, i: (0, j))],
        out_specs=pl.BlockSpec((tm, tn), lambda j, i: (i, j)),
        out_shape=jax.ShapeDtypeStruct((m, n), out_dtype),
        compiler_params=_cparams("parallel", "parallel"),
        name=name,
    )(a, b)


def _ln_kernel(x_ref, h_ref, g_ref, b_ref, o_ref):
    t = DEEPNORM_ALPHA * x_ref[...] + h_ref[...]
    mu = jnp.mean(t, axis=-1, keepdims=True)
    c = t - mu
    var = jnp.mean(c * c, axis=-1, keepdims=True)
    o_ref[...] = c * lax.rsqrt(var + LN_EPS) * g_ref[...] + b_ref[...]


def _res_layer_norm(x, h, g, b, *, name, tm=256):
    m, d = x.shape
    assert m % tm == 0
    row = pl.BlockSpec((tm, d), lambda i: (i, 0))
    vec = pl.BlockSpec((1, d), lambda i: (0, 0))
    return pl.pallas_call(
        _ln_kernel, grid=(m // tm,), in_specs=[row, row, vec, vec], out_specs=row,
        out_shape=jax.ShapeDtypeStruct((m, d), F32),
        compiler_params=_cparams("parallel"), name=name,
    )(x, h, g.reshape(1, d), b.reshape(1, d))


LANES = 128
ATTN_ROWS = 256
GROUP_QKV_W = 3 * DIL_GROUP_W
LANE_CHUNKS = DIL_GROUP_W // LANES


def _rotary_tables(s):
    half = DIL_HEAD_DIM // 2
    inv = ROPE_THETA ** (-jnp.arange(half, dtype=F32) * 2.0 / DIL_HEAD_DIM)
    ang = jnp.arange(s, dtype=F32)[:, None] * inv[None, :]
    cos, sin = jnp.cos(ang), jnp.sin(ang)
    return (jnp.concatenate([cos, cos, cos, cos], axis=-1),
            jnp.concatenate([-sin, sin, -sin, sin], axis=-1))


def _attn_in_kernel(x_ref, w_ref, cos_ref, sin_ref, o0_ref, o1_ref, o2_ref, scr_ref):
    tm = x_ref.shape[0]
    acc = jnp.dot(x_ref[...].astype(BF16), w_ref[...], preferred_element_type=F32)
    cos, sin = cos_ref[...], sin_ref[...]
    lane = lax.broadcasted_iota(jnp.int32, (tm, LANES), 1)
    first_half = (lane % DIL_HEAD_DIM) < (DIL_HEAD_DIM // 2)
    outs = (o0_ref, o1_ref, o2_ref)
    slab = 0
    for part in range(3):
        for gi, (_, d) in enumerate(DIL_GROUPS):
            for c in range(LANE_CHUNKS):
                col = part * DIL_DIM + gi * DIL_GROUP_W + c * LANES
                x = acc[:, col:col + LANES]
                if part < 2:
                    partner = jnp.where(first_half, pltpu.roll(x, LANES - 32, 1),
                                        pltpu.roll(x, 32, 1))
                    x = x * cos + partner * sin
                dst = part * DIL_GROUP_W + c * LANES
                if d == 1:
                    outs[gi][:, dst:dst + LANES] = x
                else:
                    scr_ref[slab] = x
                    for r in range(d):
                        outs[gi][:, r * GROUP_QKV_W + dst:r * GROUP_QKV_W + dst + LANES] = (
                            scr_ref[slab, pl.ds(r, tm // d, stride=d), :])
                    slab += 1


def _attn_in_proj(x, w_attn, cos_t, sin_t):
    b, s, dm = x.shape
    tm = min(ATTN_ROWS, s)
    dmax = max(d for _, d in DIL_GROUPS)
    assert s % tm == 0 and tm % (dmax * 8) == 0
    n_strided = sum(3 * LANE_CHUNKS for _, d in DIL_GROUPS if d > 1)
    tab = pl.BlockSpec((tm, LANES), lambda i, j: (j, 0))
    ospec = lambda d: pl.BlockSpec((None, tm // d, d * GROUP_QKV_W), lambda i, j: (i, j, 0))
    return pl.pallas_call(
        _attn_in_kernel, grid=(b, s // tm),
        in_specs=[pl.BlockSpec((None, tm, dm), lambda i, j: (i, j, 0)),
                  pl.BlockSpec(w_attn.shape, lambda i, j: (0, 0), pipeline_mode=pl.Buffered(1)),
                  tab, tab],
        out_specs=[ospec(d) for _, d in DIL_GROUPS],
        out_shape=[jax.ShapeDtypeStruct((b, s // d, d * GROUP_QKV_W), F32) for _, d in DIL_GROUPS],
        scratch_shapes=[pltpu.VMEM((n_strided, tm, LANES), F32)],
        compiler_params=_cparams("parallel", "parallel"), name="in_attn_rope",
    )(x, w_attn.astype(BF16), cos_t, sin_t)


def _dil_attn_kernel(q_ref, kp_ref, kc_ref, vp_ref, vc_ref, o_ref, l_ref, *, span):
    nb = pl.program_id(2)
    qi = lax.broadcasted_iota(jnp.int32, (BLK, 2 * BLK), 0) + BLK
    ki = lax.broadcasted_iota(jnp.int32, (BLK, 2 * BLK), 1)
    dist = qi - ki
    mask = (dist >= 0) & (dist <= span) & ((nb > 0) | (ki >= BLK))
    scale = DIL_HEAD_DIM ** -0.5
    q = q_ref[...].astype(BF16)
    k = jnp.concatenate([kp_ref[...], kc_ref[...]], axis=0).astype(BF16)
    v = jnp.concatenate([vp_ref[...], vc_ref[...]], axis=0).astype(BF16)
    for h in range(DIL_HEADS):
        sl = slice(h * DIL_HEAD_DIM, (h + 1) * DIL_HEAD_DIM)
        s = lax.dot_general(q[:, sl], k[:, sl], (((1,), (1,)), ((), ())),
                            preferred_element_type=F32) * scale
        s = jnp.where(mask, s, NEG_INF)
        m = jnp.max(s, axis=-1, keepdims=True)
        p = jnp.exp(s - m)
        den = jnp.sum(p, axis=-1, keepdims=True)
        o = jnp.dot(p.astype(BF16), v[:, sl], preferred_element_type=F32)
        o_ref[:, sl] = o / den
        l_ref[:, sl] = jnp.broadcast_to(m + jnp.log(den), (BLK, DIL_HEAD_DIM))


def _dilated_attention(qkv_view, gi, window, dilation):
    b, n, _ = qkv_view.shape
    d = dilation
    assert n % BLK == 0
    nblk = n // BLK
    blk = (None, BLK, DIL_GROUP_W)

    def col(which):
        return lambda bi, r, nb: (bi, nb, r * 3 + which)

    def col_prev(which):
        return lambda bi, r, nb: (bi, jnp.maximum(nb - 1, 0), r * 3 + which)

    out_spec = pl.BlockSpec(blk, lambda bi, r, nb: (bi, nb, r))
    return pl.pallas_call(
        functools.partial(_dil_attn_kernel, span=window // dilation),
        grid=(b, d, nblk),
        in_specs=[pl.BlockSpec(blk, col(0)),
                  pl.BlockSpec(blk, col_prev(1)), pl.BlockSpec(blk, col(1)),
                  pl.BlockSpec(blk, col_prev(2)), pl.BlockSpec(blk, col(2))],
        out_specs=[out_spec, out_spec],
        out_shape=[jax.ShapeDtypeStruct((b, n, d * DIL_GROUP_W), F32)] * 2,
        compiler_params=_cparams("parallel", "parallel", "arbitrary"),
        name=f"dil_attn_g{gi}",
    )(qkv_view, qkv_view, qkv_view, qkv_view, qkv_view)


def _mix_merge_kernel(o0_ref, l0_ref, o1_ref, l1_ref, o2_ref, l2_ref, pa_ref, pg_ref, w_ref,
                      out_ref, scr_ref):
    tm, dm = pa_ref.shape
    slab = [0]

    def natural(ref, d):
        if d == 1:
            return [ref[:, c * LANES:(c + 1) * LANES] for c in range(LANE_CHUNKS)]
        chunks = []
        for c in range(LANE_CHUNKS):
            for r in range(d):
                scr_ref[slab[0], pl.ds(r, tm // d, stride=d), :] = (
                    ref[:, r * DIL_GROUP_W + c * LANES:r * DIL_GROUP_W + (c + 1) * LANES])
            chunks.append(scr_ref[slab[0]])
            slab[0] += 1
        return chunks

    dils = [d for _, d in DIL_GROUPS]
    o = [natural(r, d) for r, d in zip((o0_ref, o1_ref, o2_ref), dils)]
    l = [natural(r, d) for r, d in zip((l0_ref, l1_ref, l2_ref), dils)]
    yb = []
    for c in range(LANE_CHUNKS):
        m = jnp.maximum(jnp.maximum(l[0][c], l[1][c]), l[2][c])
        e = [jnp.exp(l[g][c] - m) for g in range(N_GROUPS)]
        den = e[0] + e[1] + e[2]
        yb.append((e[0] / den) * o[0][c] + (e[1] / den) * o[1][c] + (e[2] / den) * o[2][c])
    pb = jnp.dot(jnp.concatenate(yb, axis=-1).astype(BF16), w_ref[...], preferred_element_type=F32)
    out_ref[...] = (jax.nn.sigmoid(pg_ref[:, :dm]) * pa_ref[...]
                    + jax.nn.sigmoid(pg_ref[:, dm:]) * pb)


def _mix_merge(outs, lses, pa, p_gate, proj_b, bsz, s):
    n_tok, dm = pa.shape
    tm = min(ATTN_ROWS, s)
    per_b = s // tm
    n_strided = sum(2 * LANE_CHUNKS for _, d in DIL_GROUPS if d > 1)
    vspec = lambda d: pl.BlockSpec((None, tm // d, d * DIL_GROUP_W),
                                   lambda i: (i // per_b, i % per_b, 0))
    views = []
    for (_, d), o, l in zip(DIL_GROUPS, outs, lses):
        views += [(o, vspec(d)), (l, vspec(d))]
    row = lambda w: pl.BlockSpec((tm, w), lambda i: (i, 0))
    return pl.pallas_call(
        _mix_merge_kernel, grid=(n_tok // tm,),
        in_specs=[sp for _, sp in views] + [row(dm), row(2 * dm),
                                            pl.BlockSpec(proj_b.shape, lambda i: (0, 0))],
        out_specs=row(dm), out_shape=jax.ShapeDtypeStruct((n_tok, dm), F32),
        scratch_shapes=[pltpu.VMEM((n_strided, tm, LANES), F32)],
        compiler_params=_cparams("parallel"), name="attn_mix_merge",
    )(*[a for a, _ in views], pa, p_gate, proj_b.astype(BF16))


def _rwkv_scan_kernel(r_ref, w_ref, k_ref, v_ref, a_ref, b_ref, y_ref, s_ref, *, tc):
    n = RWKV_HEAD

    @pl.when(pl.program_id(0) == 0)
    def _():
        s_ref[...] = jnp.zeros_like(s_ref)

    def row(ref, t, j):
        return ref[t, pl.ds(j, 1), :]

    def step(t, carry):
        acc = [jnp.zeros(s_ref.shape[1:], F32), jnp.zeros(s_ref.shape[1:], F32)]
        for j in range(n):
            acc[j % 2] = acc[j % 2] + s_ref[j] * row(a_ref, t, j)
        sa = acc[0] + acc[1]
        vt = v_ref[t]
        yac = [jnp.zeros(s_ref.shape[1:], F32), jnp.zeros(s_ref.shape[1:], F32)]
        for j in range(n):
            sj = s_ref[j] * row(w_ref, t, j) + sa * row(b_ref, t, j) + vt * row(k_ref, t, j)
            s_ref[j] = sj
            yac[j % 2] = yac[j % 2] + sj * row(r_ref, t, j)
        y_ref[t] = yac[0] + yac[1]
        return carry

    lax.fori_loop(0, tc, step, 0)


def _rwkv_scan(r, w, k, v, a, b, *, tc=32):
    s, n, l = r.shape
    tc = min(tc, s)
    assert s % tc == 0
    blk = pl.BlockSpec((tc, n, l), lambda i: (i, 0, 0))
    return pl.pallas_call(
        functools.partial(_rwkv_scan_kernel, tc=tc),
        grid=(s // tc,), in_specs=[blk] * 6, out_specs=blk,
        out_shape=jax.ShapeDtypeStruct((s, n, l), F32),
        scratch_shapes=[pltpu.VMEM((n, n, l), F32)],
        compiler_params=_cparams("arbitrary"), name="rwkv_scan",
    )(r, w, k, v, a, b)


RWKV_SEG = 256
RWKV_PREP_ROWS = 128
SUBLANES = 8


def _head_sum(x):
    seg = RWKV_SEG
    src = lax.broadcasted_iota(jnp.int32, (seg, seg), 0) // RWKV_HEAD
    dst = lax.broadcasted_iota(jnp.int32, (seg, seg), 1) // RWKV_HEAD
    ones = (src == dst).astype(BF16)
    hi = x.astype(BF16)
    lo = (x - hi.astype(F32)).astype(BF16)
    parts = []
    for c in range(x.shape[-1] // seg):
        sl = slice(c * seg, (c + 1) * seg)
        parts.append(jnp.dot(hi[:, sl], ones, preferred_element_type=F32)
                     + jnp.dot(lo[:, sl], ones, preferred_element_type=F32))
    return jnp.concatenate(parts, axis=-1)


def _token_shift_rows(p, last_prev_row, mix, first):
    rolled = pltpu.roll(p, 1, 0)
    row0 = jnp.where(first, jnp.zeros_like(last_prev_row), last_prev_row)
    t = lax.broadcasted_iota(jnp.int32, p.shape, 0)
    prev = jnp.where(t == 0, row0, rolled)
    return p + (prev - p) * mix


def _rwkv_prep_kernel(rkv_ref, rkvp_ref, lo_ref, lop_ref, mixr_ref, mixl_ref, w0_ref, a0_ref,
                      kk_ref, ka_ref, rk_ref, wup_ref, aup_ref, gup_ref,
                      r_ref, w_ref, k_ref, v_ref, al_ref, be_ref, g_ref, bo_ref):
    first = pl.program_id(1) == 0
    d = D_MODEL

    def seg(c):
        sl = slice(c * d, (c + 1) * d)
        return _token_shift_rows(rkv_ref[:, sl], rkvp_ref[SUBLANES - 1:SUBLANES, sl],
                                 mixr_ref[:, sl], first)

    r, k, v = seg(0), seg(1), seg(2)
    zl = _token_shift_rows(lo_ref[...], lop_ref[SUBLANES - 1:SUBLANES, :], mixl_ref[...], first)
    wd = zl[:, :DECAY_LORA]
    ad = zl[:, DECAY_LORA:DECAY_LORA + AAA_LORA]
    gd = zl[:, DECAY_LORA + AAA_LORA:]
    mm = lambda a, w_ref_: jnp.dot(a.astype(BF16), w_ref_[...].astype(BF16),
                                   preferred_element_type=F32)
    z = -(w0_ref[...] + mm(jnp.tanh(wd), wup_ref))
    softplus = jnp.maximum(z, 0.0) + jnp.log1p(jnp.exp(-jnp.abs(z)))
    w = -softplus - 0.5
    a = jax.nn.sigmoid(a0_ref[...] + mm(ad, aup_ref))
    kk = k * kk_ref[...]
    kk = kk / jnp.maximum(jnp.sqrt(_head_sum(kk * kk)), 1e-12)
    k2 = k * (1.0 + (a - 1.0) * ka_ref[...])
    r_ref[...] = r
    w_ref[...] = jnp.exp(-jnp.exp(w))
    k_ref[...] = k2
    v_ref[...] = v
    al_ref[...] = -kk
    be_ref[...] = kk * a
    g_ref[...] = mm(jax.nn.sigmoid(gd), gup_ref)
    bo_ref[...] = _head_sum(r * k2 * rk_ref[...]) * v


def _rwkv_prep(p_rkv, p_lora, shift_mix, w0, w_up, a0, a_up, g_up, k_k, k_a, r_k):
    b, s, c_rkv = p_rkv.shape
    d, ts, c_lo = D_MODEL, min(RWKV_PREP_ROWS, s), p_lora.shape[-1]
    assert s % ts == 0 and ts % SUBLANES == 0
    cur = lambda c: pl.BlockSpec((None, ts, c), lambda i, j: (i, j, 0))
    prev = lambda c: pl.BlockSpec(
        (None, SUBLANES, c), lambda i, j: (i, jnp.maximum(j * (ts // SUBLANES) - 1, 0), 0))
    vec = lambda c: pl.BlockSpec((1, c), lambda i, j: (0, 0))
    full = lambda a: pl.BlockSpec(a.shape, lambda i, j: (0, 0))
    out = jax.ShapeDtypeStruct((b, s, d), F32)
    row = lambda t: t.reshape(1, -1)
    return pl.pallas_call(
        _rwkv_prep_kernel, grid=(b, s // ts),
        in_specs=[cur(c_rkv), prev(c_rkv), cur(c_lo), prev(c_lo), vec(c_rkv), vec(c_lo),
                  vec(d), vec(d), vec(d), vec(d), vec(d), full(w_up), full(a_up), full(g_up)],
        out_specs=[cur(d)] * 8, out_shape=[out] * 8,
        compiler_params=_cparams("parallel", "parallel"), name="rwkv_prep",
    )(p_rkv, p_rkv, p_lora, p_lora, row(shift_mix[:c_rkv]), row(shift_mix[c_rkv:]), row(w0),
      row(a0), row(k_k), row(k_a), row(r_k), w_up, a_up, g_up)


def _proj_a_kernel(y_ref, bo_ref, g_ref, lg_ref, lb_ref, w_ref, o_ref):
    y = y_ref[...]
    inv_n = 1.0 / RWKV_HEAD
    c = y - _head_sum(y) * inv_n
    var = _head_sum(c * c) * inv_n
    ya = (c * lax.rsqrt(var + GN_EPS) * lg_ref[...] + lb_ref[...] + bo_ref[...]) * g_ref[...]
    o_ref[...] = jnp.dot(ya.astype(BF16), w_ref[...], preferred_element_type=F32)


def _rwkv_post_proj(y, bonus, gate, lnx_g, lnx_b, proj_a, *, tm=256):
    m, d = y.shape
    assert m % tm == 0
    rowb = pl.BlockSpec((tm, d), lambda i: (i, 0))
    vec = pl.BlockSpec((1, d), lambda i: (0, 0))
    return pl.pallas_call(
        _proj_a_kernel, grid=(m // tm,),
        in_specs=[rowb, rowb, rowb, vec, vec, pl.BlockSpec((d, d), lambda i: (0, 0))],
        out_specs=rowb, out_shape=jax.ShapeDtypeStruct((m, d), F32),
        compiler_params=_cparams("parallel"), name="rwkv_post_proj_a",
    )(y, bonus, gate, lnx_g.reshape(1, d), lnx_b.reshape(1, d), proj_a.astype(BF16))


def _cross_attn_kernel(q_ref, k_ref, v_ref, o_ref):
    scale = CA_HEAD_DIM ** -0.5
    for h in range(CA_HEADS):
        sl = slice(h * CA_HEAD_DIM, (h + 1) * CA_HEAD_DIM)
        q = q_ref[:, sl].astype(BF16)
        k = k_ref[:, sl].astype(BF16)
        s = lax.dot_general(q, k, (((1,), (1,)), ((), ())), preferred_element_type=F32) * scale
        m = jnp.max(s, axis=-1, keepdims=True)
        p = jnp.exp(s - m)
        den = jnp.sum(p, axis=-1, keepdims=True)
        o = jnp.dot(p.astype(BF16), v_ref[:, sl].astype(BF16), preferred_element_type=F32)
        o_ref[:, sl] = o / den


def _cross_attention(q, kv, *, tq=512):
    b, s, d = q.shape
    mlen = kv.shape[1]
    tq = min(tq, s)
    return pl.pallas_call(
        _cross_attn_kernel, grid=(b, s // tq),
        in_specs=[pl.BlockSpec((None, tq, d), lambda i, j: (i, j, 0)),
                  pl.BlockSpec((None, mlen, d), lambda i, j: (i, 0, 0)),
                  pl.BlockSpec((None, mlen, d), lambda i, j: (i, 0, 1))],
        out_specs=pl.BlockSpec((None, tq, d), lambda i, j: (i, j, 0)),
        out_shape=jax.ShapeDtypeStruct((b, s, d), F32),
        compiler_params=_cparams("parallel", "parallel"), name="cross_attn",
    )(q, kv, kv)


def _router_kernel(x_ref, w_ref, b_ref, o_ref):
    o_ref[...] = jnp.dot(x_ref[...], w_ref[...], precision=lax.Precision.HIGHEST,
                         preferred_element_type=F32) + b_ref[...]


def _router_logits(x, w, b, *, tm=512):
    m, d = x.shape
    e = w.shape[1]
    return pl.pallas_call(
        _router_kernel, grid=(m // tm,),
        in_specs=[pl.BlockSpec((tm, d), lambda i: (i, 0)),
                  pl.BlockSpec((d, e), lambda i: (0, 0)),
                  pl.BlockSpec((1, e), lambda i: (0, 0))],
        out_specs=pl.BlockSpec((tm, e), lambda i: (i, 0)),
        out_shape=jax.ShapeDtypeStruct((m, e), F32),
        compiler_params=_cparams("parallel"), name="router",
    )(x, w, b.reshape(1, e))


W1_PREP_ROWS = 512
W1_PREP_SUB = 256


def _w1_split_kernel(w_ref, g_ref, l_ref):
    sub, half = W1_PREP_SUB, W1_PREP_SUB // 2
    src = lax.broadcasted_iota(jnp.int32, (sub, sub), 0)
    dst = lax.broadcasted_iota(jnp.int32, (sub, sub), 1)
    want = jnp.where(dst < half, 2 * dst, 2 * (dst - half) + 1)
    sel = (src == want).astype(BF16)
    for q in range(w_ref.shape[1] // sub):
        w = w_ref[:, q * sub:(q + 1) * sub].astype(BF16)
        r = jnp.dot(w, sel, preferred_element_type=F32).astype(BF16)
        g_ref[:, q * half:(q + 1) * half] = r[:, :half]
        l_ref[:, q * half:(q + 1) * half] = r[:, half:]


def _split_w1(w1):
    e, d, f2 = w1.shape
    rows = W1_PREP_ROWS
    out = jax.ShapeDtypeStruct((e, d, f2 // 2), BF16)
    ospec = pl.BlockSpec((None, rows, f2 // 2), lambda i, j: (i, j, 0))
    return pl.pallas_call(
        _w1_split_kernel, grid=(e, d // rows),
        in_specs=[pl.BlockSpec((None, rows, f2), lambda i, j: (i, j, 0))],
        out_specs=[ospec, ospec], out_shape=[out, out],
        compiler_params=_cparams("parallel", "parallel"), name="moe_w1_split",
    )(w1)


def _moe_kernel(be_ref, nused_ref, x_ref, g_ref, w1g_ref, w1l_ref, b1g_ref, b1l_ref, w2_ref,
                b2_ref, o_ref, w2b_ref):
    i = pl.program_id(0)

    @pl.when((i == 0) | (be_ref[i] != be_ref[jnp.maximum(i - 1, 0)]))
    def _():
        w2b_ref[...] = w2_ref[...].astype(BF16)

    @pl.when(i < nused_ref[0])
    def _():
        x = x_ref[...]
        glu = jnp.dot(x, w1g_ref[...], preferred_element_type=F32) + b1g_ref[...]
        lin = jnp.dot(x, w1l_ref[...], preferred_element_type=F32) + b1l_ref[...]
        glu = jnp.minimum(glu, SWIGLU_LIMIT)
        lin = jnp.clip(lin, -SWIGLU_LIMIT, SWIGLU_LIMIT)
        act = glu * jax.nn.sigmoid(SWIGLU_ALPHA * glu) * (lin + 1.0)
        y = jnp.dot(act.astype(BF16), w2b_ref[...], preferred_element_type=F32) + b2_ref[...]
        o_ref[...] = y * g_ref[...]

    @pl.when(i >= nused_ref[0])
    def _():
        o_ref[...] = jnp.zeros_like(o_ref)


def _moe_experts(blk_exp, n_used, xs, row_gate, w1g, w1l, b1g, b1l, w2, b2):
    rows, d = xs.shape
    nblk = rows // MOE_BLK
    f = w1g.shape[-1]
    wspec = lambda shape: pl.BlockSpec(shape, lambda i, be, nu: (be[i], 0, 0),
                                       pipeline_mode=pl.Buffered(1))
    grid_spec = pltpu.PrefetchScalarGridSpec(
        num_scalar_prefetch=2, grid=(nblk,),
        in_specs=[pl.BlockSpec((MOE_BLK, d), lambda i, be, nu: (i, 0)),
                  pl.BlockSpec((MOE_BLK, 1), lambda i, be, nu: (i, 0)),
                  wspec((None, d, f)), wspec((None, d, f)),
                  wspec((None, 1, f)), wspec((None, 1, f)),
                  wspec((None, f, d)), wspec((None, 1, d))],
        out_specs=pl.BlockSpec((MOE_BLK, d), lambda i, be, nu: (i, 0)),
        scratch_shapes=[pltpu.VMEM((f, d), BF16)],
    )
    return pl.pallas_call(
        _moe_kernel, grid_spec=grid_spec,
        out_shape=jax.ShapeDtypeStruct((rows, d), F32),
        compiler_params=_cparams("arbitrary"), name="moe_experts",
    )(blk_exp, n_used, xs, row_gate.reshape(rows, 1), w1g, w1l, b1g, b1l, w2, b2)


def _moe_ffn(x2d, router_w, router_b, w1, b1, w2, b2):
    n, d = x2d.shape
    logits = _router_logits(x2d, router_w, router_b)
    top_val, top_idx = lax.top_k(logits, TOP_K)
    top_w = jax.nn.softmax(top_val, axis=-1)
    flat_e = top_idx.reshape(-1)
    order = jnp.argsort(flat_e)
    e_sorted = flat_e[order]
    tok_sorted = (order // TOP_K).astype(jnp.int32)
    gate_sorted = top_w.reshape(-1)[order]
    counts = jnp.bincount(flat_e, length=N_EXPERTS)
    starts = jnp.cumsum(counts) - counts
    padded = (counts + MOE_BLK - 1) // MOE_BLK * MOE_BLK
    pends = jnp.cumsum(padded)
    pstarts = pends - padded
    dest = (pstarts[e_sorted] + jnp.arange(n * TOP_K) - starts[e_sorted]).astype(jnp.int32)
    rows = n * TOP_K + N_EXPERTS * MOE_BLK
    nblk = rows // MOE_BLK
    row_tok = jnp.full((rows,), n, jnp.int32).at[dest].set(tok_sorted)
    row_gate = jnp.zeros((rows,), F32).at[dest].set(gate_sorted)
    blk_exp = jnp.minimum(jnp.searchsorted(pends, jnp.arange(nblk) * MOE_BLK, side='right'),
                          N_EXPERTS - 1).astype(jnp.int32)
    n_used = (pends[-1] // MOE_BLK).astype(jnp.int32).reshape(1)
    xpad = jnp.concatenate([x2d.astype(BF16), jnp.zeros((1, d), BF16)], axis=0)
    xs = xpad[row_tok]
    w1g, w1l = _split_w1(w1)
    b1g = b1[:, None, 0::2]
    b1l = b1[:, None, 1::2]
    ys = _moe_experts(blk_exp, n_used, xs, row_gate, w1g, w1l, b1g, b1l, w2,
                      b2[:, None, :])
    pos = jnp.zeros((n * TOP_K,), jnp.int32).at[order].set(dest)
    return ys[pos].reshape(n, TOP_K, d).sum(axis=1)


def _to_scan_layout(t, bsz, s):
    return t.reshape(bsz, s, RWKV_HEADS, RWKV_HEAD).transpose(1, 3, 0, 2).reshape(
        s, RWKV_HEAD, bsz * RWKV_HEADS)


def _from_scan_layout(t, bsz, s):
    return t.reshape(s, RWKV_HEAD, bsz, RWKV_HEADS).transpose(2, 0, 3, 1).reshape(bsz, s, D_MODEL)


def _rwkv7_branch_proj(p_rkv, p_lora, shift_mix, w0, w_up, a0, a_up, g_up, k_k, k_a, r_k, lnx_g,
                       lnx_b, proj_a):
    bsz, s, _ = p_rkv.shape
    r, decay, k, v, alpha, beta, gate, bonus = _rwkv_prep(
        p_rkv, p_lora, shift_mix, w0, w_up, a0, a_up, g_up, k_k, k_a, r_k)
    lay = lambda t: _to_scan_layout(t, bsz, s)
    y = _rwkv_scan(lay(r), lay(decay), lay(k), lay(v), lay(alpha), lay(beta))
    y = _from_scan_layout(y, bsz, s)
    flat = lambda t: t.reshape(bsz * s, D_MODEL)
    return _rwkv_post_proj(flat(y), flat(bonus), flat(gate), lnx_g, lnx_b, proj_a)


def _hybrid_mixer(x, w_in, shift_mix, w0, w_up, a0, a_up, g_up, k_k, k_a, r_k, lnx_g, lnx_b,
                  proj_a, proj_b, w_out):
    bsz, s, d = x.shape
    n_tok = bsz * s
    x2d = x.reshape(n_tok, d)
    c0, c1, c2 = 3 * D_MODEL, RWKV_COLS, RWKV_COLS + ATTN_COLS
    p_rkv = _matmul(x2d, w_in[:, :c0], tm=512, tn=1024, name="in_rkv").reshape(bsz, s, -1)
    p_lora = _matmul(x2d, w_in[:, c0:c1], tm=1024, tn=512, name="in_lora").reshape(bsz, s, -1)
    p_gate = _matmul(x2d, w_in[:, c2:], tm=512, tn=1024, name="in_gate")

    pa = _rwkv7_branch_proj(p_rkv, p_lora, shift_mix, w0, w_up, a0, a_up, g_up, k_k, k_a, r_k,
                            lnx_g, lnx_b, proj_a)

    cos_t, sin_t = _rotary_tables(s)
    views = _attn_in_proj(x, w_in[:, c1:c2], cos_t, sin_t)
    outs, lses = [], []
    for gi, (window, dilation) in enumerate(DIL_GROUPS):
        o, l = _dilated_attention(views[gi], gi, window, dilation)
        outs.append(o)
        lses.append(l)
    merged = _mix_merge(outs, lses, pa, p_gate, proj_b, bsz, s)
    return _matmul(merged, w_out, tm=512, tn=1024, name="w_out")


def _memory_cross_attention(x2d, mem, wq, wkv, wo, bsz, s):
    d = x2d.shape[-1]
    q = _matmul(x2d, wq, tm=512, tn=1024, name="ca_q").reshape(bsz, s, d)
    kv = _matmul(mem.reshape(-1, d), wkv, tm=512, tn=1024, name="ca_kv").reshape(bsz, -1, 2 * d)
    o = _cross_attention(q, kv)
    return _matmul(o.reshape(-1, d), wo, tm=512, tn=1024, name="ca_o")


def kernel(x, mem, w_in, shift_mix, w0, w_up, a0, a_up, g_up, k_k, k_a, r_k, lnx_g, lnx_b, proj_a, proj_b, w_out, ln1_g, ln1_b, ca_wq, ca_wkv, ca_wo, ln2_g, ln2_b, router_w, router_b, moe_w1, moe_b1, moe_w2, moe_b2, ln3_g, ln3_b):
    bsz, s, d = x.shape
    n_tok = bsz * s
    for l in range(DEPTH):
        h = _hybrid_mixer(x, w_in[l], shift_mix[l], w0[l], w_up[l], a0[l], a_up[l], g_up[l],
                          k_k[l], k_a[l], r_k[l], lnx_g[l], lnx_b[l], proj_a[l], proj_b[l],
                          w_out[l])
        x1 = _res_layer_norm(x.reshape(n_tok, d), h, ln1_g[l], ln1_b[l], name="ln1")
        h = _memory_cross_attention(x1, mem, ca_wq[l], ca_wkv[l], ca_wo[l], bsz, s)
        x2 = _res_layer_norm(x1, h, ln2_g[l], ln2_b[l], name="ln2")
        h = _moe_ffn(x2, router_w[l], router_b[l], moe_w1[l], moe_b1[l], moe_w2[l], moe_b2[l])
        x = _res_layer_norm(x2, h, ln3_g[l], ln3_b[l], name="ln3").reshape(bsz, s, d)
    return x
```

```python
import functools

import jax
import jax.numpy as jnp
import numpy as np
from jax import lax
from jax.experimental import pallas as pl
from jax.experimental.pallas import tpu as pltpu

F32 = jnp.float32
BF16 = jnp.bfloat16

D_MODEL = 2048
RWKV_HEAD = 64
RWKV_HEADS = D_MODEL // RWKV_HEAD
DECAY_LORA = 96
AAA_LORA = 96
GATE_LORA = 256
GN_EPS = 64e-5
DIL_GROUPS = ((128, 1), (512, 4), (2048, 16))
N_GROUPS = len(DIL_GROUPS)
DIL_HEADS = 8
DIL_HEAD_DIM = 64
DIL_GROUP_W = DIL_HEADS * DIL_HEAD_DIM
DIL_DIM = N_GROUPS * DIL_GROUP_W
BLK = 128
ROPE_THETA = 10000.0
NEG_INF = -1e30
RWKV_COLS = 3 * D_MODEL + DECAY_LORA + AAA_LORA + GATE_LORA
ATTN_COLS = 3 * DIL_DIM
CA_HEADS = 4
CA_HEAD_DIM = D_MODEL // CA_HEADS
N_EXPERTS = 32
TOP_K = 4
D_FF = D_MODEL
SWIGLU_LIMIT = 7.0
SWIGLU_ALPHA = 1.702
MOE_BLK = 128
LN_EPS = 1e-5
DEPTH = 1
DEEPNORM_ALPHA = (2 * DEPTH) ** 0.25

V7X_VMEM_LIMIT_BYTES = 56 * 1024 * 1024


def _cparams(*sem):
    return pltpu.CompilerParams(dimension_semantics=sem, vmem_limit_bytes=V7X_VMEM_LIMIT_BYTES)


def _mm_kernel(a_ref, b_ref, o_ref):
    o_ref[...] = jnp.dot(a_ref[...].astype(BF16), b_ref[...].astype(BF16),
                         preferred_element_type=F32).astype(o_ref.dtype)


def _matmul(a, b, *, tm, tn, name, out_dtype=F32):
    m, k = a.shape
    _, n = b.shape
    tm, tn = min(tm, m), min(tn, n)
    assert m % tm == 0 and n % tn == 0, (m, n, tm, tn)
    return pl.pallas_call(
        _mm_kernel,
        grid=(n // tn, m // tm),
        in_specs=[pl.BlockSpec((tm, k), lambda j, i: (i, 0)),
                  pl.BlockSpec((k, tn), lambda j, i: (0, j))],
        out_specs=pl.BlockSpec((tm, tn), lambda j, i: (i, j)),
        out_shape=jax.ShapeDtypeStruct((m, n), out_dtype),
        compiler_params=_cparams("parallel", "parallel"),
        name=name,
    )(a, b)


def _ln_kernel(x_ref, h_ref, g_ref, b_ref, o_ref):
    t = DEEPNORM_ALPHA * x_ref[...] + h_ref[...]
    mu = jnp.mean(t, axis=-1, keepdims=True)
    c = t - mu
    var = jnp.mean(c * c, axis=-1, keepdims=True)
    o_ref[...] = c * lax.rsqrt(var + LN_EPS) * g_ref[...] + b_ref[...]


def _res_layer_norm(x, h, g, b, *, name, tm=256):
    m, d = x.shape
    assert m % tm == 0
    row = pl.BlockSpec((tm, d), lambda i: (i, 0))
    vec = pl.BlockSpec((1, d), lambda i: (0, 0))
    return pl.pallas_call(
        _ln_kernel, grid=(m // tm,), in_specs=[row, row, vec, vec], out_specs=row,
        out_shape=jax.ShapeDtypeStruct((m, d), F32),
        compiler_params=_cparams("parallel"), name=name,
    )(x, h, g.reshape(1, d), b.reshape(1, d))


LANES = 128
ATTN_ROWS = 256
GROUP_QKV_W = 3 * DIL_GROUP_W
LANE_CHUNKS = DIL_GROUP_W // LANES


def _rotary_tables(s):
    half = DIL_HEAD_DIM // 2
    inv = ROPE_THETA ** (-jnp.arange(half, dtype=F32) * 2.0 / DIL_HEAD_DIM)
    ang = jnp.arange(s, dtype=F32)[:, None] * inv[None, :]
    cos, sin = jnp.cos(ang), jnp.sin(ang)
    return (jnp.concatenate([cos, cos, cos, cos], axis=-1),
            jnp.concatenate([-sin, sin, -sin, sin], axis=-1))


def _attn_in_kernel(x_ref, w_ref, cos_ref, sin_ref, o0_ref, o1_ref, o2_ref, scr_ref):
    tm = x_ref.shape[0]
    acc = jnp.dot(x_ref[...].astype(BF16), w_ref[...], preferred_element_type=F32)
    cos, sin = cos_ref[...], sin_ref[...]
    lane = lax.broadcasted_iota(jnp.int32, (tm, LANES), 1)
    first_half = (lane % DIL_HEAD_DIM) < (DIL_HEAD_DIM // 2)
    outs = (o0_ref, o1_ref, o2_ref)
    slab = 0
    for part in range(3):
        for gi, (_, d) in enumerate(DIL_GROUPS):
            for c in range(LANE_CHUNKS):
                col = part * DIL_DIM + gi * DIL_GROUP_W + c * LANES
                x = acc[:, col:col + LANES]
                if part < 2:
                    partner = jnp.where(first_half, pltpu.roll(x, LANES - 32, 1),
                                        pltpu.roll(x, 32, 1))
                    x = x * cos + partner * sin
                dst = part * DIL_GROUP_W + c * LANES
                if d == 1:
                    outs[gi][:, dst:dst + LANES] = x
                else:
                    scr_ref[slab] = x
                    for r in range(d):
                        outs[gi][:, r * GROUP_QKV_W + dst:r * GROUP_QKV_W + dst + LANES] = (
                            scr_ref[slab, pl.ds(r, tm // d, stride=d), :])
                    slab += 1


def _attn_in_proj(x, w_attn, cos_t, sin_t):
    b, s, dm = x.shape
    tm = min(ATTN_ROWS, s)
    dmax = max(d for _, d in DIL_GROUPS)
    assert s % tm == 0 and tm % (dmax * 8) == 0
    n_strided = sum(3 * LANE_CHUNKS for _, d in DIL_GROUPS if d > 1)
    tab = pl.BlockSpec((tm, LANES), lambda i, j: (j, 0))
    ospec = lambda d: pl.BlockSpec((None, tm // d, d * GROUP_QKV_W), lambda i, j: (i, j, 0))
    return pl.pallas_call(
        _attn_in_kernel, grid=(b, s // tm),
        in_specs=[pl.BlockSpec((None, tm, dm), lambda i, j: (i, j, 0)),
                  pl.BlockSpec(w_attn.shape, lambda i, j: (0, 0), pipeline_mode=pl.Buffered(1)),
                  tab, tab],
        out_specs=[ospec(d) for _, d in DIL_GROUPS],
        out_shape=[jax.ShapeDtypeStruct((b, s // d, d * GROUP_QKV_W), F32) for _, d in DIL_GROUPS],
        scratch_shapes=[pltpu.VMEM((n_strided, tm, LANES), F32)],
        compiler_params=_cparams("parallel", "parallel"), name="in_attn_rope",
    )(x, w_attn.astype(BF16), cos_t, sin_t)


def _dil_attn_kernel(q_ref, kp_ref, kc_ref, vp_ref, vc_ref, o_ref, l_ref, *, span):
    nb = pl.program_id(2)
    qi = lax.broadcasted_iota(jnp.int32, (BLK, 2 * BLK), 0) + BLK
    ki = lax.broadcasted_iota(jnp.int32, (BLK, 2 * BLK), 1)
    dist = qi - ki
    mask = (dist >= 0) & (dist <= span) & ((nb > 0) | (ki >= BLK))
    scale = DIL_HEAD_DIM ** -0.5
    q = q_ref[...].astype(BF16)
    k = jnp.concatenate([kp_ref[...], kc_ref[...]], axis=0).astype(BF16)
    v = jnp.concatenate([vp_ref[...], vc_ref[...]], axis=0).astype(BF16)
    for h in range(DIL_HEADS):
        sl = slice(h * DIL_HEAD_DIM, (h + 1) * DIL_HEAD_DIM)
        s = lax.dot_general(q[:, sl], k[:, sl], (((1,), (1,)), ((), ())),
                            preferred_element_type=F32) * scale
        s = jnp.where(mask, s, NEG_INF)
        m = jnp.max(s, axis=-1, keepdims=True)
        p = jnp.exp(s - m)
        den = jnp.sum(p, axis=-1, keepdims=True)
        o = jnp.dot(p.astype(BF16), v[:, sl], preferred_element_type=F32)
        o_ref[:, sl] = o / den
        l_ref[:, sl] = jnp.broadcast_to(m + jnp.log(den), (BLK, DIL_HEAD_DIM))


def _dilated_attention(qkv_view, gi, window, dilation):
    b, n, _ = qkv_view.shape
    d = dilation
    assert n % BLK == 0
    nblk = n // BLK
    blk = (None, BLK, DIL_GROUP_W)

    def col(which):
        return lambda bi, r, nb: (bi, nb, r * 3 + which)

    def col_prev(which):
        return lambda bi, r, nb: (bi, jnp.maximum(nb - 1, 0), r * 3 + which)

    out_spec = pl.BlockSpec(blk, lambda bi, r, nb: (bi, nb, r))
    return pl.pallas_call(
        functools.partial(_dil_attn_kernel, span=window // dilation),
        grid=(b, d, nblk),
        in_specs=[pl.BlockSpec(blk, col(0)),
                  pl.BlockSpec(blk, col_prev(1)), pl.BlockSpec(blk, col(1)),
                  pl.BlockSpec(blk, col_prev(2)), pl.BlockSpec(blk, col(2))],
        out_specs=[out_spec, out_spec],
        out_shape=[jax.ShapeDtypeStruct((b, n, d * DIL_GROUP_W), F32)] * 2,
        compiler_params=_cparams("parallel", "parallel", "arbitrary"),
        name=f"dil_attn_g{gi}",
    )(qkv_view, qkv_view, qkv_view, qkv_view, qkv_view)


def _mix_merge_kernel(o0_ref, l0_ref, o1_ref, l1_ref, o2_ref, l2_ref, pa_ref, pg_ref, w_ref,
                      out_ref, scr_ref):
    tm, dm = pa_ref.shape
    slab = [0]

    def natural(ref, d):
        if d == 1:
            return [ref[:, c * LANES:(c + 1) * LANES] for c in range(LANE_CHUNKS)]
        chunks = []
        for c in range(LANE_CHUNKS):
            for r in range(d):
                scr_ref[slab[0], pl.ds(r, tm // d, stride=d), :] = (
                    ref[:, r * DIL_GROUP_W + c * LANES:r * DIL_GROUP_W + (c + 1) * LANES])
            chunks.append(scr_ref[slab[0]])
            slab[0] += 1
        return chunks

    dils = [d for _, d in DIL_GROUPS]
    o = [natural(r, d) for r, d in zip((o0_ref, o1_ref, o2_ref), dils)]
    l = [natural(r, d) for r, d in zip((l0_ref, l1_ref, l2_ref), dils)]
    yb = []
    for c in range(LANE_CHUNKS):
        m = jnp.maximum(jnp.maximum(l[0][c], l[1][c]), l[2][c])
        e = [jnp.exp(l[g][c] - m) for g in range(N_GROUPS)]
        den = e[0] + e[1] + e[2]
        yb.append((e[0] / den) * o[0][c] + (e[1] / den) * o[1][c] + (e[2] / den) * o[2][c])
    pb = jnp.dot(jnp.concatenate(yb, axis=-1).astype(BF16), w_ref[...], preferred_element_type=F32)
    out_ref[...] = (jax.nn.sigmoid(pg_ref[:, :dm]) * pa_ref[...]
                    + jax.nn.sigmoid(pg_ref[:, dm:]) * pb)


def _mix_merge(outs, lses, pa, p_gate, proj_b, bsz, s):
    n_tok, dm = pa.shape
    tm = min(ATTN_ROWS, s)
    per_b = s // tm
    n_strided = sum(2 * LANE_CHUNKS for _, d in DIL_GROUPS if d > 1)
    vspec = lambda d: pl.BlockSpec((None, tm // d, d * DIL_GROUP_W),
                                   lambda i: (i // per_b, i % per_b, 0))
    views = []
    for (_, d), o, l in zip(DIL_GROUPS, outs, lses):
        views += [(o, vspec(d)), (l, vspec(d))]
    row = lambda w: pl.BlockSpec((tm, w), lambda i: (i, 0))
    return pl.pallas_call(
        _mix_merge_kernel, grid=(n_tok // tm,),
        in_specs=[sp for _, sp in views] + [row(dm), row(2 * dm),
                                            pl.BlockSpec(proj_b.shape, lambda i: (0, 0))],
        out_specs=row(dm), out_shape=jax.ShapeDtypeStruct((n_tok, dm), F32),
        scratch_shapes=[pltpu.VMEM((n_strided, tm, LANES), F32)],
        compiler_params=_cparams("parallel"), name="attn_mix_merge",
    )(*[a for a, _ in views], pa, p_gate, proj_b.astype(BF16))


W1_SPLIT_SUB = 256
SCAN_STEPS = 32


def _split_even_odd(w_ref, g_ref, l_ref):
    sub, half = W1_SPLIT_SUB, W1_SPLIT_SUB // 2
    src = lax.broadcasted_iota(jnp.int32, (sub, sub), 0)
    dst = lax.broadcasted_iota(jnp.int32, (sub, sub), 1)
    want = jnp.where(dst < half, 2 * dst, 2 * (dst - half) + 1)
    sel = (src == want).astype(BF16)
    for q in range(w_ref.shape[1] // sub):
        w = w_ref[:, q * sub:(q + 1) * sub].astype(BF16)
        r = jnp.dot(w, sel, preferred_element_type=F32).astype(BF16)
        g_ref[:, q * half:(q + 1) * half] = r[:, :half]
        l_ref[:, q * half:(q + 1) * half] = r[:, half:]


def _rwkv_scan_kernel(r_ref, w_ref, k_ref, v_ref, a_ref, b_ref, w1_ref, y_ref, w1g_ref, w1l_ref,
                      s_ref, *, tc):
    n = RWKV_HEAD

    @pl.when(pl.program_id(0) == 0)
    def _():
        s_ref[...] = jnp.zeros_like(s_ref)

    _split_even_odd(w1_ref, w1g_ref, w1l_ref)

    def row(ref, t, j):
        return ref[t, pl.ds(j, 1), :]

    def step(t, carry):
        acc = [jnp.zeros(s_ref.shape[1:], F32), jnp.zeros(s_ref.shape[1:], F32)]
        for j in range(n):
            acc[j % 2] = acc[j % 2] + s_ref[j] * row(a_ref, t, j)
        sa = acc[0] + acc[1]
        vt = v_ref[t]
        yac = [jnp.zeros(s_ref.shape[1:], F32), jnp.zeros(s_ref.shape[1:], F32)]
        for j in range(n):
            sj = s_ref[j] * row(w_ref, t, j) + sa * row(b_ref, t, j) + vt * row(k_ref, t, j)
            s_ref[j] = sj
            yac[j % 2] = yac[j % 2] + sj * row(r_ref, t, j)
        y_ref[t] = yac[0] + yac[1]
        return carry

    lax.fori_loop(0, tc, step, 0)


def _rwkv_scan_and_w1_split(r, w, k, v, a, b, w1):
    s, n, l = r.shape
    e, d, f2 = w1.shape
    tc = min(SCAN_STEPS, s)
    assert s % tc == 0
    steps = s // tc
    assert (e * d) % steps == 0
    wrows = e * d // steps
    assert d % wrows == 0 and wrows % SUBLANES == 0 and f2 % W1_SPLIT_SUB == 0
    per_e = d // wrows
    blk = pl.BlockSpec((tc, n, l), lambda i: (i, 0, 0))
    wspec = lambda c: pl.BlockSpec((None, wrows, c), lambda i: (i // per_e, i % per_e, 0))
    wout = jax.ShapeDtypeStruct((e, d, f2 // 2), BF16)
    return pl.pallas_call(
        functools.partial(_rwkv_scan_kernel, tc=tc),
        grid=(steps,), in_specs=[blk] * 6 + [wspec(f2)],
        out_specs=[blk, wspec(f2 // 2), wspec(f2 // 2)],
        out_shape=[jax.ShapeDtypeStruct((s, n, l), F32), wout, wout],
        scratch_shapes=[pltpu.VMEM((n, n, l), F32)],
        compiler_params=_cparams("arbitrary"), name="rwkv_scan",
    )(r, w, k, v, a, b, w1)


RWKV_SEG = 256
RWKV_PREP_ROWS = 128
SUBLANES = 8


def _head_sum(x):
    seg = RWKV_SEG
    src = lax.broadcasted_iota(jnp.int32, (seg, seg), 0) // RWKV_HEAD
    dst = lax.broadcasted_iota(jnp.int32, (seg, seg), 1) // RWKV_HEAD
    ones = (src == dst).astype(BF16)
    hi = x.astype(BF16)
    lo = (x - hi.astype(F32)).astype(BF16)
    parts = []
    for c in range(x.shape[-1] // seg):
        sl = slice(c * seg, (c + 1) * seg)
        parts.append(jnp.dot(hi[:, sl], ones, preferred_element_type=F32)
                     + jnp.dot(lo[:, sl], ones, preferred_element_type=F32))
    return jnp.concatenate(parts, axis=-1)


def _token_shift_rows(p, last_prev_row, mix, first):
    rolled = pltpu.roll(p, 1, 0)
    row0 = jnp.where(first, jnp.zeros_like(last_prev_row), last_prev_row)
    t = lax.broadcasted_iota(jnp.int32, p.shape, 0)
    prev = jnp.where(t == 0, row0, rolled)
    return p + (prev - p) * mix


def _rwkv_prep_kernel(rkv_ref, rkvp_ref, lo_ref, lop_ref, mixr_ref, mixl_ref, w0_ref, a0_ref,
                      kk_ref, ka_ref, rk_ref, wup_ref, aup_ref, gup_ref,
                      r_ref, w_ref, k_ref, v_ref, al_ref, be_ref, g_ref, bo_ref):
    first = pl.program_id(1) == 0
    d = D_MODEL

    def seg(c):
        sl = slice(c * d, (c + 1) * d)
        return _token_shift_rows(rkv_ref[:, sl], rkvp_ref[SUBLANES - 1:SUBLANES, sl],
                                 mixr_ref[:, sl], first)

    r, k, v = seg(0), seg(1), seg(2)
    zl = _token_shift_rows(lo_ref[...], lop_ref[SUBLANES - 1:SUBLANES, :], mixl_ref[...], first)
    wd = zl[:, :DECAY_LORA]
    ad = zl[:, DECAY_LORA:DECAY_LORA + AAA_LORA]
    gd = zl[:, DECAY_LORA + AAA_LORA:]
    mm = lambda a, w_ref_: jnp.dot(a.astype(BF16), w_ref_[...].astype(BF16),
                                   preferred_element_type=F32)
    z = -(w0_ref[...] + mm(jnp.tanh(wd), wup_ref))
    softplus = jnp.maximum(z, 0.0) + jnp.log1p(jnp.exp(-jnp.abs(z)))
    w = -softplus - 0.5
    a = jax.nn.sigmoid(a0_ref[...] + mm(ad, aup_ref))
    kk = k * kk_ref[...]
    kk = kk / jnp.maximum(jnp.sqrt(_head_sum(kk * kk)), 1e-12)
    k2 = k * (1.0 + (a - 1.0) * ka_ref[...])
    r_ref[...] = r
    w_ref[...] = jnp.exp(-jnp.exp(w))
    k_ref[...] = k2
    v_ref[...] = v
    al_ref[...] = -kk
    be_ref[...] = kk * a
    g_ref[...] = mm(jax.nn.sigmoid(gd), gup_ref)
    bo_ref[...] = _head_sum(r * k2 * rk_ref[...]) * v


def _rwkv_prep(p_rkv, p_lora, shift_mix, w0, w_up, a0, a_up, g_up, k_k, k_a, r_k):
    b, s, c_rkv = p_rkv.shape
    d, ts, c_lo = D_MODEL, min(RWKV_PREP_ROWS, s), p_lora.shape[-1]
    assert s % ts == 0 and ts % SUBLANES == 0
    cur = lambda c: pl.BlockSpec((None, ts, c), lambda i, j: (i, j, 0))
    prev = lambda c: pl.BlockSpec(
        (None, SUBLANES, c), lambda i, j: (i, jnp.maximum(j * (ts // SUBLANES) - 1, 0), 0))
    vec = lambda c: pl.BlockSpec((1, c), lambda i, j: (0, 0))
    full = lambda a: pl.BlockSpec(a.shape, lambda i, j: (0, 0))
    out = jax.ShapeDtypeStruct((b, s, d), F32)
    row = lambda t: t.reshape(1, -1)
    return pl.pallas_call(
        _rwkv_prep_kernel, grid=(b, s // ts),
        in_specs=[cur(c_rkv), prev(c_rkv), cur(c_lo), prev(c_lo), vec(c_rkv), vec(c_lo),
                  vec(d), vec(d), vec(d), vec(d), vec(d), full(w_up), full(a_up), full(g_up)],
        out_specs=[cur(d)] * 8, out_shape=[out] * 8,
        compiler_params=_cparams("parallel", "parallel"), name="rwkv_prep",
    )(p_rkv, p_rkv, p_lora, p_lora, row(shift_mix[:c_rkv]), row(shift_mix[c_rkv:]), row(w0),
      row(a0), row(k_k), row(k_a), row(r_k), w_up, a_up, g_up)


def _proj_a_kernel(y_ref, bo_ref, g_ref, lg_ref, lb_ref, w_ref, o_ref):
    y = y_ref[...]
    inv_n = 1.0 / RWKV_HEAD
    c = y - _head_sum(y) * inv_n
    var = _head_sum(c * c) * inv_n
    ya = (c * lax.rsqrt(var + GN_EPS) * lg_ref[...] + lb_ref[...] + bo_ref[...]) * g_ref[...]
    o_ref[...] = jnp.dot(ya.astype(BF16), w_ref[...], preferred_element_type=F32)


def _rwkv_post_proj(y, bonus, gate, lnx_g, lnx_b, proj_a, *, tm=256):
    m, d = y.shape
    assert m % tm == 0
    rowb = pl.BlockSpec((tm, d), lambda i: (i, 0))
    vec = pl.BlockSpec((1, d), lambda i: (0, 0))
    return pl.pallas_call(
        _proj_a_kernel, grid=(m // tm,),
        in_specs=[rowb, rowb, rowb, vec, vec, pl.BlockSpec((d, d), lambda i: (0, 0))],
        out_specs=rowb, out_shape=jax.ShapeDtypeStruct((m, d), F32),
        compiler_params=_cparams("parallel"), name="rwkv_post_proj_a",
    )(y, bonus, gate, lnx_g.reshape(1, d), lnx_b.reshape(1, d), proj_a.astype(BF16))


def _cross_attn_kernel(q_ref, k_ref, v_ref, o_ref):
    scale = CA_HEAD_DIM ** -0.5
    for h in range(CA_HEADS):
        sl = slice(h * CA_HEAD_DIM, (h + 1) * CA_HEAD_DIM)
        q = q_ref[:, sl].astype(BF16)
        k = k_ref[:, sl].astype(BF16)
        s = lax.dot_general(q, k, (((1,), (1,)), ((), ())), preferred_element_type=F32) * scale
        m = jnp.max(s, axis=-1, keepdims=True)
        p = jnp.exp(s - m)
        den = jnp.sum(p, axis=-1, keepdims=True)
        o = jnp.dot(p.astype(BF16), v_ref[:, sl].astype(BF16), preferred_element_type=F32)
        o_ref[:, sl] = o / den


def _cross_attention(q, kv, *, tq=512):
    b, s, d = q.shape
    mlen = kv.shape[1]
    tq = min(tq, s)
    return pl.pallas_call(
        _cross_attn_kernel, grid=(b, s // tq),
        in_specs=[pl.BlockSpec((None, tq, d), lambda i, j: (i, j, 0)),
                  pl.BlockSpec((None, mlen, d), lambda i, j: (i, 0, 0)),
                  pl.BlockSpec((None, mlen, d), lambda i, j: (i, 0, 1))],
        out_specs=pl.BlockSpec((None, tq, d), lambda i, j: (i, j, 0)),
        out_shape=jax.ShapeDtypeStruct((b, s, d), F32),
        compiler_params=_cparams("parallel", "parallel"), name="cross_attn",
    )(q, kv, kv)


def _router_kernel(x_ref, w_ref, b_ref, o_ref):
    o_ref[...] = jnp.dot(x_ref[...], w_ref[...], precision=lax.Precision.HIGHEST,
                         preferred_element_type=F32) + b_ref[...]


def _router_logits(x, w, b, *, tm=512):
    m, d = x.shape
    e = w.shape[1]
    return pl.pallas_call(
        _router_kernel, grid=(m // tm,),
        in_specs=[pl.BlockSpec((tm, d), lambda i: (i, 0)),
                  pl.BlockSpec((d, e), lambda i: (0, 0)),
                  pl.BlockSpec((1, e), lambda i: (0, 0))],
        out_specs=pl.BlockSpec((tm, e), lambda i: (i, 0)),
        out_shape=jax.ShapeDtypeStruct((m, e), F32),
        compiler_params=_cparams("parallel"), name="router",
    )(x, w, b.reshape(1, e))


def _moe_kernel(be_ref, nused_ref, x_ref, g_ref, w1g_ref, w1l_ref, b1g_ref, b1l_ref, w2_ref,
                b2_ref, o_ref, w2b_ref):
    i = pl.program_id(0)

    @pl.when((i == 0) | (be_ref[i] != be_ref[jnp.maximum(i - 1, 0)]))
    def _():
        w2b_ref[...] = w2_ref[...].astype(BF16)

    @pl.when(i < nused_ref[0])
    def _():
        x = x_ref[...]
        glu = jnp.dot(x, w1g_ref[...], preferred_element_type=F32) + b1g_ref[...]
        lin = jnp.dot(x, w1l_ref[...], preferred_element_type=F32) + b1l_ref[...]
        glu = jnp.minimum(glu, SWIGLU_LIMIT)
        lin = jnp.clip(lin, -SWIGLU_LIMIT, SWIGLU_LIMIT)
        act = glu * jax.nn.sigmoid(SWIGLU_ALPHA * glu) * (lin + 1.0)
        y = jnp.dot(act.astype(BF16), w2b_ref[...], preferred_element_type=F32) + b2_ref[...]
        o_ref[...] = y * g_ref[...]

    @pl.when(i >= nused_ref[0])
    def _():
        o_ref[...] = jnp.zeros_like(o_ref)


def _moe_experts(blk_exp, n_used, xs, row_gate, w1g, w1l, b1g, b1l, w2, b2):
    rows, d = xs.shape
    nblk = rows // MOE_BLK
    f = w1g.shape[-1]
    wspec = lambda shape: pl.BlockSpec(shape, lambda i, be, nu: (be[i], 0, 0),
                                       pipeline_mode=pl.Buffered(1))
    grid_spec = pltpu.PrefetchScalarGridSpec(
        num_scalar_prefetch=2, grid=(nblk,),
        in_specs=[pl.BlockSpec((MOE_BLK, d), lambda i, be, nu: (i, 0)),
                  pl.BlockSpec((MOE_BLK, 1), lambda i, be, nu: (i, 0)),
                  wspec((None, d, f)), wspec((None, d, f)),
                  wspec((None, 1, f)), wspec((None, 1, f)),
                  wspec((None, f, d)), wspec((None, 1, d))],
        out_specs=pl.BlockSpec((MOE_BLK, d), lambda i, be, nu: (i, 0)),
        scratch_shapes=[pltpu.VMEM((f, d), BF16)],
    )
    return pl.pallas_call(
        _moe_kernel, grid_spec=grid_spec,
        out_shape=jax.ShapeDtypeStruct((rows, d), F32),
        compiler_params=_cparams("arbitrary"), name="moe_experts",
    )(blk_exp, n_used, xs, row_gate.reshape(rows, 1), w1g, w1l, b1g, b1l, w2, b2)


def _per_expert(table, idx):
    hit = idx[:, None] == jnp.arange(N_EXPERTS, dtype=idx.dtype)[None, :]
    return jnp.sum(jnp.where(hit, table[None, :], 0), axis=1)


def _moe_ffn(x2d, router_w, router_b, w1g, w1l, b1, w2, b2):
    n, d = x2d.shape
    logits = _router_logits(x2d, router_w, router_b)
    top_val, top_idx = lax.top_k(logits, TOP_K)
    top_w = jax.nn.softmax(top_val, axis=-1)
    flat_e = top_idx.reshape(-1).astype(jnp.int32)
    order = jnp.argsort(flat_e).astype(jnp.int32)
    rank = jnp.argsort(order).astype(jnp.int32)
    counts = jnp.sum(flat_e[:, None] == jnp.arange(N_EXPERTS, dtype=jnp.int32)[None, :],
                     axis=0, dtype=jnp.int32)
    starts = jnp.cumsum(counts) - counts
    padded = (counts + MOE_BLK - 1) // MOE_BLK * MOE_BLK
    pends = jnp.cumsum(padded)
    pstarts = pends - padded
    rows = n * TOP_K + N_EXPERTS * MOE_BLK
    nblk = rows // MOE_BLK
    blk_start = jnp.arange(nblk, dtype=jnp.int32) * MOE_BLK
    blk_exp = jnp.minimum(jnp.sum(pends[None, :] <= blk_start[:, None], axis=1, dtype=jnp.int32),
                          N_EXPERTS - 1)
    n_used = (pends[-1] // MOE_BLK).astype(jnp.int32).reshape(1)
    row_exp = jnp.repeat(blk_exp, MOE_BLK)
    off = jnp.arange(rows, dtype=jnp.int32) - _per_expert(pstarts, row_exp)
    valid = off < _per_expert(counts, row_exp)
    assign = order[jnp.where(valid, _per_expert(starts, row_exp) + off, 0)]
    row_tok = jnp.where(valid, assign // TOP_K, n)
    row_gate = jnp.where(valid, top_w.reshape(-1)[assign], 0.0)
    xpad = jnp.concatenate([x2d.astype(BF16), jnp.zeros((1, d), BF16)], axis=0)
    xs = xpad[row_tok]
    b1g = b1[:, None, 0::2]
    b1l = b1[:, None, 1::2]
    ys = _moe_experts(blk_exp, n_used, xs, row_gate, w1g, w1l, b1g, b1l, w2,
                      b2[:, None, :])
    pos = (_per_expert(pstarts, flat_e) + rank - _per_expert(starts, flat_e)).reshape(n, TOP_K)
    out = ys[pos[:, 0]]
    for kk in range(1, TOP_K):
        out = out + ys[pos[:, kk]]
    return out


def _to_scan_layout(t, bsz, s):
    return t.reshape(bsz, s, RWKV_HEADS, RWKV_HEAD).transpose(1, 3, 0, 2).reshape(
        s, RWKV_HEAD, bsz * RWKV_HEADS)


def _from_scan_layout(t, bsz, s):
    return t.reshape(s, RWKV_HEAD, bsz, RWKV_HEADS).transpose(2, 0, 3, 1).reshape(bsz, s, D_MODEL)


def _rwkv7_branch_proj(p_rkv, p_lora, shift_mix, w0, w_up, a0, a_up, g_up, k_k, k_a, r_k, lnx_g,
                       lnx_b, proj_a, moe_w1):
    bsz, s, _ = p_rkv.shape
    r, decay, k, v, alpha, beta, gate, bonus = _rwkv_prep(
        p_rkv, p_lora, shift_mix, w0, w_up, a0, a_up, g_up, k_k, k_a, r_k)
    lay = lambda t: _to_scan_layout(t, bsz, s)
    y, w1g, w1l = _rwkv_scan_and_w1_split(lay(r), lay(decay), lay(k), lay(v), lay(alpha),
                                          lay(beta), moe_w1)
    y = _from_scan_layout(y, bsz, s)
    flat = lambda t: t.reshape(bsz * s, D_MODEL)
    pa = _rwkv_post_proj(flat(y), flat(bonus), flat(gate), lnx_g, lnx_b, proj_a)
    return pa, w1g, w1l


def _hybrid_mixer(x, w_in, shift_mix, w0, w_up, a0, a_up, g_up, k_k, k_a, r_k, lnx_g, lnx_b,
                  proj_a, proj_b, w_out, moe_w1):
    bsz, s, d = x.shape
    n_tok = bsz * s
    x2d = x.reshape(n_tok, d)
    c0, c1, c2 = 3 * D_MODEL, RWKV_COLS, RWKV_COLS + ATTN_COLS
    p_rkv = _matmul(x2d, w_in[:, :c0], tm=512, tn=1024, name="in_rkv").reshape(bsz, s, -1)
    p_lora = _matmul(x2d, w_in[:, c0:c1], tm=1024, tn=512, name="in_lora").reshape(bsz, s, -1)
    p_gate = _matmul(x2d, w_in[:, c2:], tm=512, tn=1024, name="in_gate")

    pa, w1g, w1l = _rwkv7_branch_proj(p_rkv, p_lora, shift_mix, w0, w_up, a0, a_up, g_up, k_k,
                                      k_a, r_k, lnx_g, lnx_b, proj_a, moe_w1)

    cos_t, sin_t = _rotary_tables(s)
    views = _attn_in_proj(x, w_in[:, c1:c2], cos_t, sin_t)
    outs, lses = [], []
    for gi, (window, dilation) in enumerate(DIL_GROUPS):
        o, l = _dilated_attention(views[gi], gi, window, dilation)
        outs.append(o)
        lses.append(l)
    merged = _mix_merge(outs, lses, pa, p_gate, proj_b, bsz, s)
    return _matmul(merged, w_out, tm=512, tn=1024, name="w_out"), w1g, w1l


def _memory_cross_attention(x2d, mem, wq, wkv, wo, bsz, s):
    d = x2d.shape[-1]
    q = _matmul(x2d, wq, tm=512, tn=1024, name="ca_q").reshape(bsz, s, d)
    kv = _matmul(mem.reshape(-1, d), wkv, tm=512, tn=1024, name="ca_kv").reshape(bsz, -1, 2 * d)
    o = _cross_attention(q, kv)
    return _matmul(o.reshape(-1, d), wo, tm=512, tn=1024, name="ca_o")


def kernel(x, mem, w_in, shift_mix, w0, w_up, a0, a_up, g_up, k_k, k_a, r_k, lnx_g, lnx_b, proj_a, proj_b, w_out, ln1_g, ln1_b, ca_wq, ca_wkv, ca_wo, ln2_g, ln2_b, router_w, router_b, moe_w1, moe_b1, moe_w2, moe_b2, ln3_g, ln3_b):
    bsz, s, d = x.shape
    n_tok = bsz * s
    for l in range(DEPTH):
        h, w1g, w1l = _hybrid_mixer(x, w_in[l], shift_mix[l], w0[l], w_up[l], a0[l], a_up[l],
                                    g_up[l], k_k[l], k_a[l], r_k[l], lnx_g[l], lnx_b[l],
                                    proj_a[l], proj_b[l], w_out[l], moe_w1[l])
        x1 = _res_layer_norm(x.reshape(n_tok, d), h, ln1_g[l], ln1_b[l], name="ln1")
        h = _memory_cross_attention(x1, mem, ca_wq[l], ca_wkv[l], ca_wo[l], bsz, s)
        x2 = _res_layer_norm(x1, h, ln2_g[l], ln2_b[l], name="ln2")
        h = _moe_ffn(x2, router_w[l], router_b[l], w1g, w1l, moe_b1[l], moe_w2[l], moe_b2[l])
        x = _res_layer_norm(x2, h, ln3_g[l], ln3_b[l], name="ln3").reshape(bsz, s, d)
    return x
```

```python
import functools

import jax
import jax.numpy as jnp
import numpy as np
from jax import lax
from jax.experimental import pallas as pl
from jax.experimental.pallas import tpu as pltpu

F32 = jnp.float32
BF16 = jnp.bfloat16

D_MODEL = 2048
RWKV_HEAD = 64
RWKV_HEADS = D_MODEL // RWKV_HEAD
DECAY_LORA = 96
AAA_LORA = 96
GATE_LORA = 256
GN_EPS = 64e-5
DIL_GROUPS = ((128, 1), (512, 4), (2048, 16))
N_GROUPS = len(DIL_GROUPS)
DIL_HEADS = 8
DIL_HEAD_DIM = 64
DIL_GROUP_W = DIL_HEADS * DIL_HEAD_DIM
DIL_DIM = N_GROUPS * DIL_GROUP_W
BLK = 128
ROPE_THETA = 10000.0
NEG_INF = -1e30
RWKV_COLS = 3 * D_MODEL + DECAY_LORA + AAA_LORA + GATE_LORA
ATTN_COLS = 3 * DIL_DIM
CA_HEADS = 4
CA_HEAD_DIM = D_MODEL // CA_HEADS
N_EXPERTS = 32
TOP_K = 4
D_FF = D_MODEL
SWIGLU_LIMIT = 7.0
SWIGLU_ALPHA = 1.702
MOE_BLK = 128
LN_EPS = 1e-5
DEPTH = 1
DEEPNORM_ALPHA = (2 * DEPTH) ** 0.25

V7X_VMEM_LIMIT_BYTES = 56 * 1024 * 1024


def _cparams(*sem):
    return pltpu.CompilerParams(dimension_semantics=sem, vmem_limit_bytes=V7X_VMEM_LIMIT_BYTES)


def _mm_kernel(a_ref, b_ref, o_ref):
    o_ref[...] = jnp.dot(a_ref[...].astype(BF16), b_ref[...].astype(BF16),
                         preferred_element_type=F32).astype(o_ref.dtype)


def _matmul(a, b, *, tm, tn, name, out_dtype=F32):
    m, k = a.shape
    _, n = b.shape
    tm, tn = min(tm, m), min(tn, n)
    assert m % tm == 0 and n % tn == 0, (m, n, tm, tn)
    return pl.pallas_call(
        _mm_kernel,
        grid=(n // tn, m // tm),
        in_specs=[pl.BlockSpec((tm, k), lambda j, i: (i, 0)),
                  pl.BlockSpec((k, tn), lambda j, i: (0, j))],
        out_specs=pl.BlockSpec((tm, tn), lambda j, i: (i, j)),
        out_shape=jax.ShapeDtypeStruct((m, n), out_dtype),
        compiler_params=_cparams("parallel", "parallel"),
        name=name,
    )(a, b)


def _ln_kernel(x_ref, h_ref, g_ref, b_ref, o_ref):
    t = DEEPNORM_ALPHA * x_ref[...] + h_ref[...]
    mu = jnp.mean(t, axis=-1, keepdims=True)
    c = t - mu
    var = jnp.mean(c * c, axis=-1, keepdims=True)
    o_ref[...] = c * lax.rsqrt(var + LN_EPS) * g_ref[...] + b_ref[...]


def _res_layer_norm(x, h, g, b, *, name, tm=256):
    m, d = x.shape
    assert m % tm == 0
    row = pl.BlockSpec((tm, d), lambda i: (i, 0))
    vec = pl.BlockSpec((1, d), lambda i: (0, 0))
    return pl.pallas_call(
        _ln_kernel, grid=(m // tm,), in_specs=[row, row, vec, vec], out_specs=row,
        out_shape=jax.ShapeDtypeStruct((m, d), F32),
        compiler_params=_cparams("parallel"), name=name,
    )(x, h, g.reshape(1, d), b.reshape(1, d))


def _mm_res_ln_kernel(a_ref, w_ref, x_ref, g_ref, b_ref, *rest, with_router):
    if with_router:
        rw_ref, rb_ref, o_ref, ob_ref, lg_ref = rest
    else:
        o_ref, ob_ref = rest
    h = jnp.dot(a_ref[...].astype(BF16), w_ref[...], preferred_element_type=F32)
    t = DEEPNORM_ALPHA * x_ref[...] + h
    mu = jnp.mean(t, axis=-1, keepdims=True)
    c = t - mu
    var = jnp.mean(c * c, axis=-1, keepdims=True)
    y = c * lax.rsqrt(var + LN_EPS) * g_ref[...] + b_ref[...]
    o_ref[...] = y
    ob_ref[...] = y.astype(BF16)
    if with_router:
        lg_ref[...] = jnp.dot(y, rw_ref[...], precision=lax.Precision.HIGHEST,
                              preferred_element_type=F32) + rb_ref[...]


def _matmul_res_ln(a, w, x, g, b, *, name, router=None, tm=512):
    m, k = a.shape
    d = w.shape[1]
    assert m % tm == 0
    row = lambda c: pl.BlockSpec((tm, c), lambda i: (i, 0))
    const = lambda r, c: pl.BlockSpec((r, c), lambda i: (0, 0))
    in_specs = [row(k), const(k, d), row(d), const(1, d), const(1, d)]
    args = [a, w.astype(BF16), x, g.reshape(1, d), b.reshape(1, d)]
    out_specs = [row(d), row(d)]
    out_shape = [jax.ShapeDtypeStruct((m, d), F32), jax.ShapeDtypeStruct((m, d), BF16)]
    if router is not None:
        rw, rb = router
        e = rw.shape[1]
        in_specs += [const(d, e), const(1, e)]
        args += [rw, rb.reshape(1, e)]
        out_specs.append(row(e))
        out_shape.append(jax.ShapeDtypeStruct((m, e), F32))
    return pl.pallas_call(
        functools.partial(_mm_res_ln_kernel, with_router=router is not None),
        grid=(m // tm,), in_specs=in_specs, out_specs=out_specs, out_shape=out_shape,
        compiler_params=_cparams("parallel"), name=name,
    )(*args)


LANES = 128
ATTN_ROWS = 256
GROUP_QKV_W = 3 * DIL_GROUP_W
LANE_CHUNKS = DIL_GROUP_W // LANES


def _rotary_tables(s):
    half = DIL_HEAD_DIM // 2
    inv = ROPE_THETA ** (-jnp.arange(half, dtype=F32) * 2.0 / DIL_HEAD_DIM)
    ang = jnp.arange(s, dtype=F32)[:, None] * inv[None, :]
    cos, sin = jnp.cos(ang), jnp.sin(ang)
    return (jnp.concatenate([cos, cos, cos, cos], axis=-1),
            jnp.concatenate([-sin, sin, -sin, sin], axis=-1))


def _attn_in_kernel(x_ref, w_ref, cos_ref, sin_ref, o0_ref, o1_ref, o2_ref, scr_ref):
    tm = x_ref.shape[0]
    acc = jnp.dot(x_ref[...].astype(BF16), w_ref[...], preferred_element_type=F32)
    cos, sin = cos_ref[...], sin_ref[...]
    lane = lax.broadcasted_iota(jnp.int32, (tm, LANES), 1)
    first_half = (lane % DIL_HEAD_DIM) < (DIL_HEAD_DIM // 2)
    outs = (o0_ref, o1_ref, o2_ref)
    slab = 0
    for part in range(3):
        for gi, (_, d) in enumerate(DIL_GROUPS):
            for c in range(LANE_CHUNKS):
                col = part * DIL_DIM + gi * DIL_GROUP_W + c * LANES
                x = acc[:, col:col + LANES]
                if part < 2:
                    partner = jnp.where(first_half, pltpu.roll(x, LANES - 32, 1),
                                        pltpu.roll(x, 32, 1))
                    x = x * cos + partner * sin
                dst = part * DIL_GROUP_W + c * LANES
                if d == 1:
                    outs[gi][:, dst:dst + LANES] = x
                else:
                    scr_ref[slab] = x
                    for r in range(d):
                        outs[gi][:, r * GROUP_QKV_W + dst:r * GROUP_QKV_W + dst + LANES] = (
                            scr_ref[slab, pl.ds(r, tm // d, stride=d), :])
                    slab += 1


def _attn_in_proj(x, w_attn, cos_t, sin_t):
    b, s, dm = x.shape
    tm = min(ATTN_ROWS, s)
    dmax = max(d for _, d in DIL_GROUPS)
    assert s % tm == 0 and tm % (dmax * 8) == 0
    n_strided = sum(3 * LANE_CHUNKS for _, d in DIL_GROUPS if d > 1)
    tab = pl.BlockSpec((tm, LANES), lambda i, j: (j, 0))
    ospec = lambda d: pl.BlockSpec((None, tm // d, d * GROUP_QKV_W), lambda i, j: (i, j, 0))
    return pl.pallas_call(
        _attn_in_kernel, grid=(b, s // tm),
        in_specs=[pl.BlockSpec((None, tm, dm), lambda i, j: (i, j, 0)),
                  pl.BlockSpec(w_attn.shape, lambda i, j: (0, 0), pipeline_mode=pl.Buffered(1)),
                  tab, tab],
        out_specs=[ospec(d) for _, d in DIL_GROUPS],
        out_shape=[jax.ShapeDtypeStruct((b, s // d, d * GROUP_QKV_W), F32) for _, d in DIL_GROUPS],
        scratch_shapes=[pltpu.VMEM((n_strided, tm, LANES), F32)],
        compiler_params=_cparams("parallel", "parallel"), name="in_attn_rope",
    )(x, w_attn.astype(BF16), cos_t, sin_t)


def _dil_attn_kernel(q_ref, kp_ref, kc_ref, vp_ref, vc_ref, o_ref, l_ref, *, span):
    nb = pl.program_id(2)
    qi = lax.broadcasted_iota(jnp.int32, (BLK, 2 * BLK), 0) + BLK
    ki = lax.broadcasted_iota(jnp.int32, (BLK, 2 * BLK), 1)
    dist = qi - ki
    mask = (dist >= 0) & (dist <= span) & ((nb > 0) | (ki >= BLK))
    scale = DIL_HEAD_DIM ** -0.5
    q = q_ref[...].astype(BF16)
    k = jnp.concatenate([kp_ref[...], kc_ref[...]], axis=0).astype(BF16)
    v = jnp.concatenate([vp_ref[...], vc_ref[...]], axis=0).astype(BF16)
    for h in range(DIL_HEADS):
        sl = slice(h * DIL_HEAD_DIM, (h + 1) * DIL_HEAD_DIM)
        s = lax.dot_general(q[:, sl], k[:, sl], (((1,), (1,)), ((), ())),
                            preferred_element_type=F32) * scale
        s = jnp.where(mask, s, NEG_INF)
        m = jnp.max(s, axis=-1, keepdims=True)
        p = jnp.exp(s - m)
        den = jnp.sum(p, axis=-1, keepdims=True)
        o = jnp.dot(p.astype(BF16), v[:, sl], preferred_element_type=F32)
        o_ref[:, sl] = o / den
        l_ref[:, sl] = jnp.broadcast_to(m + jnp.log(den), (BLK, DIL_HEAD_DIM))


def _dilated_attention(qkv_view, gi, window, dilation):
    b, n, _ = qkv_view.shape
    d = dilation
    assert n % BLK == 0
    nblk = n // BLK
    blk = (None, BLK, DIL_GROUP_W)

    def col(which):
        return lambda bi, r, nb: (bi, nb, r * 3 + which)

    def col_prev(which):
        return lambda bi, r, nb: (bi, jnp.maximum(nb - 1, 0), r * 3 + which)

    out_spec = pl.BlockSpec(blk, lambda bi, r, nb: (bi, nb, r))
    return pl.pallas_call(
        functools.partial(_dil_attn_kernel, span=window // dilation),
        grid=(b, d, nblk),
        in_specs=[pl.BlockSpec(blk, col(0)),
                  pl.BlockSpec(blk, col_prev(1)), pl.BlockSpec(blk, col(1)),
                  pl.BlockSpec(blk, col_prev(2)), pl.BlockSpec(blk, col(2))],
        out_specs=[out_spec, out_spec],
        out_shape=[jax.ShapeDtypeStruct((b, n, d * DIL_GROUP_W), F32)] * 2,
        compiler_params=_cparams("parallel", "parallel", "arbitrary"),
        name=f"dil_attn_g{gi}",
    )(qkv_view, qkv_view, qkv_view, qkv_view, qkv_view)


def _mix_merge_kernel(o0_ref, l0_ref, o1_ref, l1_ref, o2_ref, l2_ref, pa_ref, pg_ref, w_ref,
                      out_ref, scr_ref):
    tm, dm = pa_ref.shape
    slab = [0]

    def natural(ref, d):
        if d == 1:
            return [ref[:, c * LANES:(c + 1) * LANES] for c in range(LANE_CHUNKS)]
        chunks = []
        for c in range(LANE_CHUNKS):
            for r in range(d):
                scr_ref[slab[0], pl.ds(r, tm // d, stride=d), :] = (
                    ref[:, r * DIL_GROUP_W + c * LANES:r * DIL_GROUP_W + (c + 1) * LANES])
            chunks.append(scr_ref[slab[0]])
            slab[0] += 1
        return chunks

    dils = [d for _, d in DIL_GROUPS]
    o = [natural(r, d) for r, d in zip((o0_ref, o1_ref, o2_ref), dils)]
    l = [natural(r, d) for r, d in zip((l0_ref, l1_ref, l2_ref), dils)]
    yb = []
    for c in range(LANE_CHUNKS):
        m = jnp.maximum(jnp.maximum(l[0][c], l[1][c]), l[2][c])
        e = [jnp.exp(l[g][c] - m) for g in range(N_GROUPS)]
        den = e[0] + e[1] + e[2]
        yb.append((e[0] / den) * o[0][c] + (e[1] / den) * o[1][c] + (e[2] / den) * o[2][c])
    pb = jnp.dot(jnp.concatenate(yb, axis=-1).astype(BF16), w_ref[...], preferred_element_type=F32)
    out_ref[...] = (jax.nn.sigmoid(pg_ref[:, :dm]) * pa_ref[...]
                    + jax.nn.sigmoid(pg_ref[:, dm:]) * pb).astype(out_ref.dtype)


def _mix_merge(outs, lses, pa, p_gate, proj_b, bsz, s):
    n_tok, dm = pa.shape
    tm = min(ATTN_ROWS, s)
    per_b = s // tm
    n_strided = sum(2 * LANE_CHUNKS for _, d in DIL_GROUPS if d > 1)
    vspec = lambda d: pl.BlockSpec((None, tm // d, d * DIL_GROUP_W),
                                   lambda i: (i // per_b, i % per_b, 0))
    views = []
    for (_, d), o, l in zip(DIL_GROUPS, outs, lses):
        views += [(o, vspec(d)), (l, vspec(d))]
    row = lambda w: pl.BlockSpec((tm, w), lambda i: (i, 0))
    return pl.pallas_call(
        _mix_merge_kernel, grid=(n_tok // tm,),
        in_specs=[sp for _, sp in views] + [row(dm), row(2 * dm),
                                            pl.BlockSpec(proj_b.shape, lambda i: (0, 0))],
        out_specs=row(dm), out_shape=jax.ShapeDtypeStruct((n_tok, dm), BF16),
        scratch_shapes=[pltpu.VMEM((n_strided, tm, LANES), F32)],
        compiler_params=_cparams("parallel"), name="attn_mix_merge",
    )(*[a for a, _ in views], pa, p_gate, proj_b.astype(BF16))


W1_SPLIT_SUB = 256
SCAN_STEPS = 32
SCAN_SLAB = 16


def _split_even_odd(w_ref, g_ref, l_ref):
    sub, half = W1_SPLIT_SUB, W1_SPLIT_SUB // 2
    src = lax.broadcasted_iota(jnp.int32, (sub, sub), 0)
    dst = lax.broadcasted_iota(jnp.int32, (sub, sub), 1)
    want = jnp.where(dst < half, 2 * dst, 2 * (dst - half) + 1)
    sel = (src == want).astype(BF16)
    for q in range(w_ref.shape[1] // sub):
        w = w_ref[:, q * sub:(q + 1) * sub].astype(BF16)
        r = jnp.dot(w, sel, preferred_element_type=F32).astype(BF16)
        g_ref[:, q * half:(q + 1) * half] = r[:, :half]
        l_ref[:, q * half:(q + 1) * half] = r[:, half:]


def _rwkv_scan_kernel(r_ref, w_ref, k_ref, v_ref, a_ref, b_ref, w1_ref, y_ref, w1g_ref, w1l_ref,
                      s_ref, *, tc):
    n = RWKV_HEAD

    @pl.when(pl.program_id(0) == 0)
    def _():
        s_ref[...] = jnp.zeros_like(s_ref)

    _split_even_odd(w1_ref, w1g_ref, w1l_ref)

    slabs = n // SCAN_SLAB
    zero = jnp.zeros((SCAN_SLAB, s_ref.shape[2]), F32)

    def row(ref, t, j):
        return ref[t, pl.ds(j, 1), :]

    def rows_of(slab):
        return pl.ds(pl.multiple_of(slab * SCAN_SLAB, SCAN_SLAB), SCAN_SLAB)

    def state_times_a(t, slab):
        rows = rows_of(slab)
        acc = [zero, zero]
        for j in range(n):
            acc[j % 2] = acc[j % 2] + s_ref[j, rows, :] * row(a_ref, t, j)
        return acc[0] + acc[1]

    def update_and_read(t, slab, sa):
        rows = rows_of(slab)
        vt = v_ref[t, rows, :]
        yac = [zero, zero]
        for j in range(n):
            sj = (s_ref[j, rows, :] * row(w_ref, t, j) + sa * row(b_ref, t, j)
                  + vt * row(k_ref, t, j))
            s_ref[j, rows, :] = sj
            yac[j % 2] = yac[j % 2] + sj * row(r_ref, t, j)
        y_ref[t, rows, :] = yac[0] + yac[1]

    def trip(q, sa):
        t, slab = q // slabs, q % slabs
        nxt = jnp.minimum(q + 1, tc * slabs - 1)
        update_and_read(t, slab, sa)
        return state_times_a(nxt // slabs, nxt % slabs)

    lax.fori_loop(0, tc * slabs, trip, state_times_a(0, 0))


def _rwkv_scan_and_w1_split(r, w, k, v, a, b, w1):
    s, n, l = r.shape
    e, d, f2 = w1.shape
    tc = min(SCAN_STEPS, s)
    assert s % tc == 0
    steps = s // tc
    assert (e * d) % steps == 0
    wrows = e * d // steps
    assert d % wrows == 0 and wrows % SUBLANES == 0 and f2 % W1_SPLIT_SUB == 0
    per_e = d // wrows
    blk = pl.BlockSpec((tc, n, l), lambda i: (i, 0, 0))
    wspec = lambda c: pl.BlockSpec((None, wrows, c), lambda i: (i // per_e, i % per_e, 0))
    wout = jax.ShapeDtypeStruct((e, d, f2 // 2), BF16)
    return pl.pallas_call(
        functools.partial(_rwkv_scan_kernel, tc=tc),
        grid=(steps,), in_specs=[blk] * 6 + [wspec(f2)],
        out_specs=[blk, wspec(f2 // 2), wspec(f2 // 2)],
        out_shape=[jax.ShapeDtypeStruct((s, n, l), F32), wout, wout],
        scratch_shapes=[pltpu.VMEM((n, n, l), F32)],
        compiler_params=_cparams("arbitrary"), name="rwkv_scan",
    )(r, w, k, v, a, b, w1)


RWKV_SEG = 256
RWKV_PREP_ROWS = 128
SUBLANES = 8


def _head_sum(x):
    seg = RWKV_SEG
    src = lax.broadcasted_iota(jnp.int32, (seg, seg), 0) // RWKV_HEAD
    dst = lax.broadcasted_iota(jnp.int32, (seg, seg), 1) // RWKV_HEAD
    ones = (src == dst).astype(BF16)
    hi = x.astype(BF16)
    lo = (x - hi.astype(F32)).astype(BF16)
    parts = []
    for c in range(x.shape[-1] // seg):
        sl = slice(c * seg, (c + 1) * seg)
        parts.append(jnp.dot(hi[:, sl], ones, preferred_element_type=F32)
                     + jnp.dot(lo[:, sl], ones, preferred_element_type=F32))
    return jnp.concatenate(parts, axis=-1)


def _token_shift_rows(p, last_prev_row, mix, first):
    rolled = pltpu.roll(p, 1, 0)
    row0 = jnp.where(first, jnp.zeros_like(last_prev_row), last_prev_row)
    t = lax.broadcasted_iota(jnp.int32, p.shape, 0)
    prev = jnp.where(t == 0, row0, rolled)
    return p + (prev - p) * mix


def _rwkv_prep_kernel(rkv_ref, rkvp_ref, lo_ref, lop_ref, mixr_ref, mixl_ref, w0_ref, a0_ref,
                      kk_ref, ka_ref, rk_ref, wup_ref, aup_ref, gup_ref,
                      r_ref, w_ref, k_ref, v_ref, al_ref, be_ref, g_ref, bo_ref):
    first = pl.program_id(1) == 0
    d = D_MODEL

    def seg(c):
        sl = slice(c * d, (c + 1) * d)
        return _token_shift_rows(rkv_ref[:, sl], rkvp_ref[SUBLANES - 1:SUBLANES, sl],
                                 mixr_ref[:, sl], first)

    r, k, v = seg(0), seg(1), seg(2)
    zl = _token_shift_rows(lo_ref[...], lop_ref[SUBLANES - 1:SUBLANES, :], mixl_ref[...], first)
    wd = zl[:, :DECAY_LORA]
    ad = zl[:, DECAY_LORA:DECAY_LORA + AAA_LORA]
    gd = zl[:, DECAY_LORA + AAA_LORA:]
    mm = lambda a, w_ref_: jnp.dot(a.astype(BF16), w_ref_[...].astype(BF16),
                                   preferred_element_type=F32)
    z = -(w0_ref[...] + mm(jnp.tanh(wd), wup_ref))
    softplus = jnp.maximum(z, 0.0) + jnp.log1p(jnp.exp(-jnp.abs(z)))
    w = -softplus - 0.5
    a = jax.nn.sigmoid(a0_ref[...] + mm(ad, aup_ref))
    kk = k * kk_ref[...]
    kk = kk / jnp.maximum(jnp.sqrt(_head_sum(kk * kk)), 1e-12)
    k2 = k * (1.0 + (a - 1.0) * ka_ref[...])
    r_ref[...] = r
    w_ref[...] = jnp.exp(-jnp.exp(w))
    k_ref[...] = k2
    v_ref[...] = v
    al_ref[...] = -kk
    be_ref[...] = kk * a
    g_ref[...] = mm(jax.nn.sigmoid(gd), gup_ref)
    bo_ref[...] = _head_sum(r * k2 * rk_ref[...]) * v


def _rwkv_prep(p_rkv, p_lora, shift_mix, w0, w_up, a0, a_up, g_up, k_k, k_a, r_k):
    b, s, c_rkv = p_rkv.shape
    d, ts, c_lo = D_MODEL, min(RWKV_PREP_ROWS, s), p_lora.shape[-1]
    assert s % ts == 0 and ts % SUBLANES == 0
    cur = lambda c: pl.BlockSpec((None, ts, c), lambda i, j: (i, j, 0))
    prev = lambda c: pl.BlockSpec(
        (None, SUBLANES, c), lambda i, j: (i, jnp.maximum(j * (ts // SUBLANES) - 1, 0), 0))
    vec = lambda c: pl.BlockSpec((1, c), lambda i, j: (0, 0))
    full = lambda a: pl.BlockSpec(a.shape, lambda i, j: (0, 0))
    out = jax.ShapeDtypeStruct((b, s, d), F32)
    row = lambda t: t.reshape(1, -1)
    return pl.pallas_call(
        _rwkv_prep_kernel, grid=(b, s // ts),
        in_specs=[cur(c_rkv), prev(c_rkv), cur(c_lo), prev(c_lo), vec(c_rkv), vec(c_lo),
                  vec(d), vec(d), vec(d), vec(d), vec(d), full(w_up), full(a_up), full(g_up)],
        out_specs=[cur(d)] * 8, out_shape=[out] * 8,
        compiler_params=_cparams("parallel", "parallel"), name="rwkv_prep",
    )(p_rkv, p_rkv, p_lora, p_lora, row(shift_mix[:c_rkv]), row(shift_mix[c_rkv:]), row(w0),
      row(a0), row(k_k), row(k_a), row(r_k), w_up, a_up, g_up)


def _proj_a_kernel(y_ref, bo_ref, g_ref, lg_ref, lb_ref, w_ref, o_ref):
    y = y_ref[...]
    inv_n = 1.0 / RWKV_HEAD
    c = y - _head_sum(y) * inv_n
    var = _head_sum(c * c) * inv_n
    ya = (c * lax.rsqrt(var + GN_EPS) * lg_ref[...] + lb_ref[...] + bo_ref[...]) * g_ref[...]
    o_ref[...] = jnp.dot(ya.astype(BF16), w_ref[...], preferred_element_type=F32)


def _rwkv_post_proj(y, bonus, gate, lnx_g, lnx_b, proj_a, *, tm=256):
    m, d = y.shape
    assert m % tm == 0
    rowb = pl.BlockSpec((tm, d), lambda i: (i, 0))
    vec = pl.BlockSpec((1, d), lambda i: (0, 0))
    return pl.pallas_call(
        _proj_a_kernel, grid=(m // tm,),
        in_specs=[rowb, rowb, rowb, vec, vec, pl.BlockSpec((d, d), lambda i: (0, 0))],
        out_specs=rowb, out_shape=jax.ShapeDtypeStruct((m, d), F32),
        compiler_params=_cparams("parallel"), name="rwkv_post_proj_a",
    )(y, bonus, gate, lnx_g.reshape(1, d), lnx_b.reshape(1, d), proj_a.astype(BF16))


def _cross_attn_kernel(q_ref, k_ref, v_ref, o_ref):
    scale = CA_HEAD_DIM ** -0.5
    for h in range(CA_HEADS):
        sl = slice(h * CA_HEAD_DIM, (h + 1) * CA_HEAD_DIM)
        q = q_ref[:, sl].astype(BF16)
        k = k_ref[:, sl].astype(BF16)
        s = lax.dot_general(q, k, (((1,), (1,)), ((), ())), preferred_element_type=F32) * scale
        m = jnp.max(s, axis=-1, keepdims=True)
        p = jnp.exp(s - m)
        den = jnp.sum(p, axis=-1, keepdims=True)
        o = jnp.dot(p.astype(BF16), v_ref[:, sl].astype(BF16), preferred_element_type=F32)
        o_ref[:, sl] = (o / den).astype(o_ref.dtype)


def _cross_attention(q, kv, *, tq=512):
    b, s, d = q.shape
    mlen = kv.shape[1]
    tq = min(tq, s)
    return pl.pallas_call(
        _cross_attn_kernel, grid=(b, s // tq),
        in_specs=[pl.BlockSpec((None, tq, d), lambda i, j: (i, j, 0)),
                  pl.BlockSpec((None, mlen, d), lambda i, j: (i, 0, 0)),
                  pl.BlockSpec((None, mlen, d), lambda i, j: (i, 0, 1))],
        out_specs=pl.BlockSpec((None, tq, d), lambda i, j: (i, j, 0)),
        out_shape=jax.ShapeDtypeStruct((b, s, d), BF16),
        compiler_params=_cparams("parallel", "parallel"), name="cross_attn",
    )(q, kv, kv)


def _moe_kernel(be_ref, nused_ref, x_ref, g_ref, w1g_ref, w1l_ref, b1g_ref, b1l_ref, w2_ref,
                b2_ref, o_ref, w2b_ref):
    i = pl.program_id(0)

    @pl.when((i == 0) | (be_ref[i] != be_ref[jnp.maximum(i - 1, 0)]))
    def _():
        w2b_ref[...] = w2_ref[...].astype(BF16)

    @pl.when(i < nused_ref[0])
    def _():
        x = x_ref[...]
        glu = jnp.dot(x, w1g_ref[...], preferred_element_type=F32) + b1g_ref[...]
        lin = jnp.dot(x, w1l_ref[...], preferred_element_type=F32) + b1l_ref[...]
        glu = jnp.minimum(glu, SWIGLU_LIMIT)
        lin = jnp.clip(lin, -SWIGLU_LIMIT, SWIGLU_LIMIT)
        act = glu * jax.nn.sigmoid(SWIGLU_ALPHA * glu) * (lin + 1.0)
        y = jnp.dot(act.astype(BF16), w2b_ref[...], preferred_element_type=F32) + b2_ref[...]
        o_ref[...] = y * g_ref[...]

    @pl.when(i >= nused_ref[0])
    def _():
        o_ref[...] = jnp.zeros_like(o_ref)


def _moe_experts(blk_exp, n_used, xs, row_gate, w1g, w1l, b1g, b1l, w2, b2):
    rows, d = xs.shape
    nblk = rows // MOE_BLK
    f = w1g.shape[-1]
    wspec = lambda shape: pl.BlockSpec(shape, lambda i, be, nu: (be[i], 0, 0),
                                       pipeline_mode=pl.Buffered(1))
    grid_spec = pltpu.PrefetchScalarGridSpec(
        num_scalar_prefetch=2, grid=(nblk,),
        in_specs=[pl.BlockSpec((MOE_BLK, d), lambda i, be, nu: (i, 0)),
                  pl.BlockSpec((MOE_BLK, 1), lambda i, be, nu: (i, 0)),
                  wspec((None, d, f)), wspec((None, d, f)),
                  wspec((None, 1, f)), wspec((None, 1, f)),
                  wspec((None, f, d)), wspec((None, 1, d))],
        out_specs=pl.BlockSpec((MOE_BLK, d), lambda i, be, nu: (i, 0)),
        scratch_shapes=[pltpu.VMEM((f, d), BF16)],
    )
    return pl.pallas_call(
        _moe_kernel, grid_spec=grid_spec,
        out_shape=jax.ShapeDtypeStruct((rows, d), F32),
        compiler_params=_cparams("arbitrary"), name="moe_experts",
    )(blk_exp, n_used, xs, row_gate.reshape(rows, 1), w1g, w1l, b1g, b1l, w2, b2)


def _per_expert(table, idx):
    hit = idx[:, None] == jnp.arange(N_EXPERTS, dtype=idx.dtype)[None, :]
    return jnp.sum(jnp.where(hit, table[None, :], 0), axis=1)


def _moe_ffn(x_bf, logits, w1g, w1l, b1, w2, b2):
    n, d = x_bf.shape
    top_val, top_idx = lax.top_k(logits, TOP_K)
    top_w = jax.nn.softmax(top_val, axis=-1)
    flat_e = top_idx.reshape(-1).astype(jnp.int32)
    order = jnp.argsort(flat_e).astype(jnp.int32)
    rank = jnp.argsort(order).astype(jnp.int32)
    counts = jnp.sum(flat_e[:, None] == jnp.arange(N_EXPERTS, dtype=jnp.int32)[None, :],
                     axis=0, dtype=jnp.int32)
    starts = jnp.cumsum(counts) - counts
    padded = (counts + MOE_BLK - 1) // MOE_BLK * MOE_BLK
    pends = jnp.cumsum(padded)
    pstarts = pends - padded
    rows = n * TOP_K + N_EXPERTS * MOE_BLK
    nblk = rows // MOE_BLK
    blk_start = jnp.arange(nblk, dtype=jnp.int32) * MOE_BLK
    blk_exp = jnp.minimum(jnp.sum(pends[None, :] <= blk_start[:, None], axis=1, dtype=jnp.int32),
                          N_EXPERTS - 1)
    n_used = (pends[-1] // MOE_BLK).astype(jnp.int32).reshape(1)
    row_exp = jnp.repeat(blk_exp, MOE_BLK)
    off = jnp.arange(rows, dtype=jnp.int32) - _per_expert(pstarts, row_exp)
    valid = off < _per_expert(counts, row_exp)
    assign = order[jnp.where(valid, _per_expert(starts, row_exp) + off, 0)]
    row_tok = jnp.where(valid, assign // TOP_K, 0)
    row_gate = jnp.where(valid, top_w.reshape(-1)[assign], 0.0)
    xs = x_bf[row_tok]
    b1g = b1[:, None, 0::2]
    b1l = b1[:, None, 1::2]
    ys = _moe_experts(blk_exp, n_used, xs, row_gate, w1g, w1l, b1g, b1l, w2,
                      b2[:, None, :])
    pos = (_per_expert(pstarts, flat_e) + rank - _per_expert(starts, flat_e)).reshape(n, TOP_K)
    out = ys[pos[:, 0]]
    for kk in range(1, TOP_K):
        out = out + ys[pos[:, kk]]
    return out


def _to_scan_layout(t, bsz, s):
    return t.reshape(bsz, s, RWKV_HEADS, RWKV_HEAD).transpose(1, 3, 0, 2).reshape(
        s, RWKV_HEAD, bsz * RWKV_HEADS)


def _from_scan_layout(t, bsz, s):
    return t.reshape(s, RWKV_HEAD, bsz, RWKV_HEADS).transpose(2, 0, 3, 1).reshape(bsz, s, D_MODEL)


def _rwkv7_branch_proj(p_rkv, p_lora, shift_mix, w0, w_up, a0, a_up, g_up, k_k, k_a, r_k, lnx_g,
                       lnx_b, proj_a, moe_w1):
    bsz, s, _ = p_rkv.shape
    r, decay, k, v, alpha, beta, gate, bonus = _rwkv_prep(
        p_rkv, p_lora, shift_mix, w0, w_up, a0, a_up, g_up, k_k, k_a, r_k)
    lay = lambda t: _to_scan_layout(t, bsz, s)
    y, w1g, w1l = _rwkv_scan_and_w1_split(lay(r), lay(decay), lay(k), lay(v), lay(alpha),
                                          lay(beta), moe_w1)
    y = _from_scan_layout(y, bsz, s)
    flat = lambda t: t.reshape(bsz * s, D_MODEL)
    pa = _rwkv_post_proj(flat(y), flat(bonus), flat(gate), lnx_g, lnx_b, proj_a)
    return pa, w1g, w1l


def _hybrid_mixer(x, w_in, shift_mix, w0, w_up, a0, a_up, g_up, k_k, k_a, r_k, lnx_g, lnx_b,
                  proj_a, proj_b, moe_w1):
    bsz, s, d = x.shape
    n_tok = bsz * s
    x_bf = x.astype(BF16)
    x2d = x_bf.reshape(n_tok, d)
    c0, c1, c2 = 3 * D_MODEL, RWKV_COLS, RWKV_COLS + ATTN_COLS
    p_rkv = _matmul(x2d, w_in[:, :c0], tm=1024, tn=1024, name="in_rkv").reshape(bsz, s, -1)
    p_lora = _matmul(x2d, w_in[:, c0:c1], tm=1024, tn=512, name="in_lora").reshape(bsz, s, -1)
    p_gate = _matmul(x2d, w_in[:, c2:], tm=1024, tn=1024, name="in_gate")

    pa, w1g, w1l = _rwkv7_branch_proj(p_rkv, p_lora, shift_mix, w0, w_up, a0, a_up, g_up, k_k,
                                      k_a, r_k, lnx_g, lnx_b, proj_a, moe_w1)

    cos_t, sin_t = _rotary_tables(s)
    views = _attn_in_proj(x_bf, w_in[:, c1:c2], cos_t, sin_t)
    outs, lses = [], []
    for gi, (window, dilation) in enumerate(DIL_GROUPS):
        o, l = _dilated_attention(views[gi], gi, window, dilation)
        outs.append(o)
        lses.append(l)
    return _mix_merge(outs, lses, pa, p_gate, proj_b, bsz, s), w1g, w1l


def _memory_cross_attention(x_bf, mem, wq, wkv, bsz, s):
    d = x_bf.shape[-1]
    q = _matmul(x_bf, wq, tm=1024, tn=1024, name="ca_q").reshape(bsz, s, d)
    kv = _matmul(mem.reshape(-1, d), wkv, tm=512, tn=1024, name="ca_kv").reshape(bsz, -1, 2 * d)
    return _cross_attention(q, kv).reshape(-1, d)


def kernel(x, mem, w_in, shift_mix, w0, w_up, a0, a_up, g_up, k_k, k_a, r_k, lnx_g, lnx_b, proj_a, proj_b, w_out, ln1_g, ln1_b, ca_wq, ca_wkv, ca_wo, ln2_g, ln2_b, router_w, router_b, moe_w1, moe_b1, moe_w2, moe_b2, ln3_g, ln3_b):
    bsz, s, d = x.shape
    n_tok = bsz * s
    for l in range(DEPTH):
        merged, w1g, w1l = _hybrid_mixer(x, w_in[l], shift_mix[l], w0[l], w_up[l], a0[l],
                                         a_up[l], g_up[l], k_k[l], k_a[l], r_k[l], lnx_g[l],
                                         lnx_b[l], proj_a[l], proj_b[l], moe_w1[l])
        x1, x1_bf = _matmul_res_ln(merged, w_out[l], x.reshape(n_tok, d), ln1_g[l], ln1_b[l],
                                   name="w_out_ln1")
        o = _memory_cross_attention(x1_bf, mem, ca_wq[l], ca_wkv[l], bsz, s)
        x2, x2_bf, logits = _matmul_res_ln(o, ca_wo[l], x1, ln2_g[l], ln2_b[l],
                                           router=(router_w[l], router_b[l]), name="ca_o_ln2")
        h = _moe_ffn(x2_bf, logits, w1g, w1l, moe_b1[l], moe_w2[l], moe_b2[l])
        x = _res_layer_norm(x2, h, ln3_g[l], ln3_b[l], name="ln3").reshape(bsz, s, d)
    return x
```

```python
import functools

import jax
import jax.numpy as jnp
import numpy as np
from jax import lax
from jax.experimental import pallas as pl
from jax.experimental.pallas import tpu as pltpu

F32 = jnp.float32
BF16 = jnp.bfloat16

D_MODEL = 2048
RWKV_HEAD = 64
RWKV_HEADS = D_MODEL // RWKV_HEAD
DECAY_LORA = 96
AAA_LORA = 96
GATE_LORA = 256
GN_EPS = 64e-5
DIL_GROUPS = ((128, 1), (512, 4), (2048, 16))
N_GROUPS = len(DIL_GROUPS)
DIL_HEADS = 8
DIL_HEAD_DIM = 64
DIL_GROUP_W = DIL_HEADS * DIL_HEAD_DIM
DIL_DIM = N_GROUPS * DIL_GROUP_W
BLK = 128
ROPE_THETA = 10000.0
NEG_INF = -1e30
RWKV_COLS = 3 * D_MODEL + DECAY_LORA + AAA_LORA + GATE_LORA
ATTN_COLS = 3 * DIL_DIM
CA_HEADS = 4
CA_HEAD_DIM = D_MODEL // CA_HEADS
N_EXPERTS = 32
TOP_K = 4
D_FF = D_MODEL
SWIGLU_LIMIT = 7.0
SWIGLU_ALPHA = 1.702
MOE_BLK = 128
LN_EPS = 1e-5
DEPTH = 1
DEEPNORM_ALPHA = (2 * DEPTH) ** 0.25

V7X_VMEM_LIMIT_BYTES = 56 * 1024 * 1024
V7X_VMEM_LIMIT_MOE_BYTES = 61 * 1024 * 1024


def _cparams(*sem):
    return pltpu.CompilerParams(dimension_semantics=sem, vmem_limit_bytes=V7X_VMEM_LIMIT_BYTES)


def _mm_kernel(a_ref, b_ref, o_ref):
    o_ref[...] = jnp.dot(a_ref[...].astype(BF16), b_ref[...].astype(BF16),
                         preferred_element_type=F32).astype(o_ref.dtype)


def _matmul(a, b, *, tm, tn, name, out_dtype=F32):
    m, k = a.shape
    _, n = b.shape
    tm, tn = min(tm, m), min(tn, n)
    assert m % tm == 0 and n % tn == 0, (m, n, tm, tn)
    return pl.pallas_call(
        _mm_kernel,
        grid=(n // tn, m // tm),
        in_specs=[pl.BlockSpec((tm, k), lambda j, i: (i, 0)),
                  pl.BlockSpec((k, tn), lambda j, i: (0, j))],
        out_specs=pl.BlockSpec((tm, tn), lambda j, i: (i, j)),
        out_shape=jax.ShapeDtypeStruct((m, n), out_dtype),
        compiler_params=_cparams("parallel", "parallel"),
        name=name,
    )(a, b)


def _ln_kernel(x_ref, *rest):
    *h_refs, g_ref, b_ref, o_ref = rest
    t = DEEPNORM_ALPHA * x_ref[...]
    for h_ref in h_refs:
        t = t + h_ref[...]
    mu = jnp.mean(t, axis=-1, keepdims=True)
    c = t - mu
    var = jnp.mean(c * c, axis=-1, keepdims=True)
    o_ref[...] = c * lax.rsqrt(var + LN_EPS) * g_ref[...] + b_ref[...]


def _res_layer_norm(x, hs, g, b, *, name, tm=256):
    m, d = x.shape
    assert m % tm == 0
    row = pl.BlockSpec((tm, d), lambda i: (i, 0))
    vec = pl.BlockSpec((1, d), lambda i: (0, 0))
    return pl.pallas_call(
        _ln_kernel, grid=(m // tm,), in_specs=[row] * (1 + len(hs)) + [vec, vec], out_specs=row,
        out_shape=jax.ShapeDtypeStruct((m, d), F32),
        compiler_params=_cparams("parallel"), name=name,
    )(x, *hs, g.reshape(1, d), b.reshape(1, d))


def _mm_res_ln_kernel(a_ref, w_ref, x_ref, g_ref, b_ref, o_ref, ob_ref):
    h = jnp.dot(a_ref[...].astype(BF16), w_ref[...], preferred_element_type=F32)
    t = DEEPNORM_ALPHA * x_ref[...] + h
    mu = jnp.mean(t, axis=-1, keepdims=True)
    c = t - mu
    var = jnp.mean(c * c, axis=-1, keepdims=True)
    y = c * lax.rsqrt(var + LN_EPS) * g_ref[...] + b_ref[...]
    o_ref[...] = y
    ob_ref[...] = y.astype(BF16)


def _matmul_res_ln(a, w, x, g, b, *, name, tm=512):
    m, k = a.shape
    d = w.shape[1]
    assert m % tm == 0
    row = lambda c: pl.BlockSpec((tm, c), lambda i: (i, 0))
    const = lambda r, c: pl.BlockSpec((r, c), lambda i: (0, 0))
    in_specs = [row(k), const(k, d), row(d), const(1, d), const(1, d)]
    args = [a, w.astype(BF16), x, g.reshape(1, d), b.reshape(1, d)]
    out_specs = [row(d), row(d)]
    out_shape = [jax.ShapeDtypeStruct((m, d), F32), jax.ShapeDtypeStruct((m, d), BF16)]
    return pl.pallas_call(
        _mm_res_ln_kernel,
        grid=(m // tm,), in_specs=in_specs, out_specs=out_specs, out_shape=out_shape,
        compiler_params=_cparams("parallel"), name=name,
    )(*args)


LANES = 128
ATTN_ROWS = 256
GROUP_QKV_W = 3 * DIL_GROUP_W
LANE_CHUNKS = DIL_GROUP_W // LANES


def _rotary_tables(s):
    half = DIL_HEAD_DIM // 2
    inv = ROPE_THETA ** (-jnp.arange(half, dtype=F32) * 2.0 / DIL_HEAD_DIM)
    ang = jnp.arange(s, dtype=F32)[:, None] * inv[None, :]
    cos, sin = jnp.cos(ang), jnp.sin(ang)
    return (jnp.concatenate([cos, cos, cos, cos], axis=-1),
            jnp.concatenate([-sin, sin, -sin, sin], axis=-1))


def _attn_in_kernel(x_ref, w_ref, cos_ref, sin_ref, o0_ref, o1_ref, o2_ref, scr_ref):
    tm = x_ref.shape[0]
    acc = jnp.dot(x_ref[...].astype(BF16), w_ref[...], preferred_element_type=F32)
    cos, sin = cos_ref[...], sin_ref[...]
    lane = lax.broadcasted_iota(jnp.int32, (tm, LANES), 1)
    first_half = (lane % DIL_HEAD_DIM) < (DIL_HEAD_DIM // 2)
    outs = (o0_ref, o1_ref, o2_ref)
    slab = 0
    for part in range(3):
        for gi, (_, d) in enumerate(DIL_GROUPS):
            for c in range(LANE_CHUNKS):
                col = part * DIL_DIM + gi * DIL_GROUP_W + c * LANES
                x = acc[:, col:col + LANES]
                if part < 2:
                    partner = jnp.where(first_half, pltpu.roll(x, LANES - 32, 1),
                                        pltpu.roll(x, 32, 1))
                    x = x * cos + partner * sin
                dst = part * DIL_GROUP_W + c * LANES
                if d == 1:
                    outs[gi][:, dst:dst + LANES] = x
                else:
                    scr_ref[slab] = x
                    for r in range(d):
                        outs[gi][:, r * GROUP_QKV_W + dst:r * GROUP_QKV_W + dst + LANES] = (
                            scr_ref[slab, pl.ds(r, tm // d, stride=d), :])
                    slab += 1


def _attn_in_proj(x, w_attn, cos_t, sin_t):
    b, s, dm = x.shape
    tm = min(ATTN_ROWS, s)
    dmax = max(d for _, d in DIL_GROUPS)
    assert s % tm == 0 and tm % (dmax * 8) == 0
    n_strided = sum(3 * LANE_CHUNKS for _, d in DIL_GROUPS if d > 1)
    tab = pl.BlockSpec((tm, LANES), lambda i, j: (j, 0))
    ospec = lambda d: pl.BlockSpec((None, tm // d, d * GROUP_QKV_W), lambda i, j: (i, j, 0))
    return pl.pallas_call(
        _attn_in_kernel, grid=(b, s // tm),
        in_specs=[pl.BlockSpec((None, tm, dm), lambda i, j: (i, j, 0)),
                  pl.BlockSpec(w_attn.shape, lambda i, j: (0, 0), pipeline_mode=pl.Buffered(1)),
                  tab, tab],
        out_specs=[ospec(d) for _, d in DIL_GROUPS],
        out_shape=[jax.ShapeDtypeStruct((b, s // d, d * GROUP_QKV_W), F32) for _, d in DIL_GROUPS],
        scratch_shapes=[pltpu.VMEM((n_strided, tm, LANES), F32)],
        compiler_params=_cparams("parallel", "parallel"), name="in_attn_rope",
    )(x, w_attn.astype(BF16), cos_t, sin_t)


def _dil_attn_kernel(q_ref, kp_ref, kc_ref, vp_ref, vc_ref, o_ref, l_ref, *, span):
    nb = pl.program_id(2)
    qi = lax.broadcasted_iota(jnp.int32, (BLK, 2 * BLK), 0) + BLK
    ki = lax.broadcasted_iota(jnp.int32, (BLK, 2 * BLK), 1)
    dist = qi - ki
    mask = (dist >= 0) & (dist <= span) & ((nb > 0) | (ki >= BLK))
    scale = DIL_HEAD_DIM ** -0.5
    q = q_ref[...].astype(BF16)
    k = jnp.concatenate([kp_ref[...], kc_ref[...]], axis=0).astype(BF16)
    v = jnp.concatenate([vp_ref[...], vc_ref[...]], axis=0).astype(BF16)
    for h in range(DIL_HEADS):
        sl = slice(h * DIL_HEAD_DIM, (h + 1) * DIL_HEAD_DIM)
        s = lax.dot_general(q[:, sl], k[:, sl], (((1,), (1,)), ((), ())),
                            preferred_element_type=F32) * scale
        s = jnp.where(mask, s, NEG_INF)
        m = jnp.max(s, axis=-1, keepdims=True)
        p = jnp.exp(s - m)
        den = jnp.sum(p, axis=-1, keepdims=True)
        o = jnp.dot(p.astype(BF16), v[:, sl], preferred_element_type=F32)
        o_ref[:, sl] = o / den
        l_ref[:, sl] = jnp.broadcast_to(m + jnp.log(den), (BLK, DIL_HEAD_DIM))


def _dilated_attention(qkv_view, gi, window, dilation):
    b, n, _ = qkv_view.shape
    d = dilation
    assert n % BLK == 0
    nblk = n // BLK
    blk = (None, BLK, DIL_GROUP_W)

    def col(which):
        return lambda bi, r, nb: (bi, nb, r * 3 + which)

    def col_prev(which):
        return lambda bi, r, nb: (bi, jnp.maximum(nb - 1, 0), r * 3 + which)

    out_spec = pl.BlockSpec(blk, lambda bi, r, nb: (bi, nb, r))
    return pl.pallas_call(
        functools.partial(_dil_attn_kernel, span=window // dilation),
        grid=(b, d, nblk),
        in_specs=[pl.BlockSpec(blk, col(0)),
                  pl.BlockSpec(blk, col_prev(1)), pl.BlockSpec(blk, col(1)),
                  pl.BlockSpec(blk, col_prev(2)), pl.BlockSpec(blk, col(2))],
        out_specs=[out_spec, out_spec],
        out_shape=[jax.ShapeDtypeStruct((b, n, d * DIL_GROUP_W), F32)] * 2,
        compiler_params=_cparams("parallel", "parallel", "arbitrary"),
        name=f"dil_attn_g{gi}",
    )(qkv_view, qkv_view, qkv_view, qkv_view, qkv_view)


def _mix_merge_kernel(o0_ref, l0_ref, o1_ref, l1_ref, o2_ref, l2_ref, pa_ref, pg_ref, w_ref,
                      out_ref, scr_ref):
    tm, dm = pa_ref.shape
    slab = [0]

    def natural(ref, d):
        if d == 1:
            return [ref[:, c * LANES:(c + 1) * LANES] for c in range(LANE_CHUNKS)]
        chunks = []
        for c in range(LANE_CHUNKS):
            for r in range(d):
                scr_ref[slab[0], pl.ds(r, tm // d, stride=d), :] = (
                    ref[:, r * DIL_GROUP_W + c * LANES:r * DIL_GROUP_W + (c + 1) * LANES])
            chunks.append(scr_ref[slab[0]])
            slab[0] += 1
        return chunks

    dils = [d for _, d in DIL_GROUPS]
    o = [natural(r, d) for r, d in zip((o0_ref, o1_ref, o2_ref), dils)]
    l = [natural(r, d) for r, d in zip((l0_ref, l1_ref, l2_ref), dils)]
    yb = []
    for c in range(LANE_CHUNKS):
        m = jnp.maximum(jnp.maximum(l[0][c], l[1][c]), l[2][c])
        e = [jnp.exp(l[g][c] - m) for g in range(N_GROUPS)]
        den = e[0] + e[1] + e[2]
        yb.append((e[0] / den) * o[0][c] + (e[1] / den) * o[1][c] + (e[2] / den) * o[2][c])
    pb = jnp.dot(jnp.concatenate(yb, axis=-1).astype(BF16), w_ref[...], preferred_element_type=F32)
    out_ref[...] = (jax.nn.sigmoid(pg_ref[:, :dm]) * pa_ref[...]
                    + jax.nn.sigmoid(pg_ref[:, dm:]) * pb).astype(out_ref.dtype)


def _mix_merge(outs, lses, pa, p_gate, proj_b, bsz, s):
    n_tok, dm = pa.shape
    tm = min(ATTN_ROWS, s)
    per_b = s // tm
    n_strided = sum(2 * LANE_CHUNKS for _, d in DIL_GROUPS if d > 1)
    vspec = lambda d: pl.BlockSpec((None, tm // d, d * DIL_GROUP_W),
                                   lambda i: (i // per_b, i % per_b, 0))
    views = []
    for (_, d), o, l in zip(DIL_GROUPS, outs, lses):
        views += [(o, vspec(d)), (l, vspec(d))]
    row = lambda w: pl.BlockSpec((tm, w), lambda i: (i, 0))
    return pl.pallas_call(
        _mix_merge_kernel, grid=(n_tok // tm,),
        in_specs=[sp for _, sp in views] + [row(dm), row(2 * dm),
                                            pl.BlockSpec(proj_b.shape, lambda i: (0, 0))],
        out_specs=row(dm), out_shape=jax.ShapeDtypeStruct((n_tok, dm), BF16),
        scratch_shapes=[pltpu.VMEM((n_strided, tm, LANES), F32)],
        compiler_params=_cparams("parallel"), name="attn_mix_merge",
    )(*[a for a, _ in views], pa, p_gate, proj_b.astype(BF16))


W1_SPLIT_SUB = 256
SCAN_STEPS = 32
SCAN_SLAB = 16


def _split_even_odd(w_ref, g_ref, l_ref):
    sub, half = W1_SPLIT_SUB, W1_SPLIT_SUB // 2
    src = lax.broadcasted_iota(jnp.int32, (sub, sub), 0)
    dst = lax.broadcasted_iota(jnp.int32, (sub, sub), 1)
    want = jnp.where(dst < half, 2 * dst, 2 * (dst - half) + 1)
    sel = (src == want).astype(BF16)
    for q in range(w_ref.shape[1] // sub):
        w = w_ref[:, q * sub:(q + 1) * sub].astype(BF16)
        r = jnp.dot(w, sel, preferred_element_type=F32).astype(BF16)
        g_ref[:, q * half:(q + 1) * half] = r[:, :half]
        l_ref[:, q * half:(q + 1) * half] = r[:, half:]


def _rwkv_scan_kernel(r_ref, w_ref, k_ref, v_ref, a_ref, b_ref, w1_ref, w2_ref, y_ref, w1g_ref,
                      w1l_ref, w2b_ref, s_ref, *, tc):
    n = RWKV_HEAD

    @pl.when(pl.program_id(0) == 0)
    def _():
        s_ref[...] = jnp.zeros_like(s_ref)

    _split_even_odd(w1_ref, w1g_ref, w1l_ref)
    w2b_ref[...] = w2_ref[...].astype(BF16)

    slabs = n // SCAN_SLAB
    zero = jnp.zeros((SCAN_SLAB, s_ref.shape[2]), F32)

    def row(ref, t, j):
        return ref[t, pl.ds(j, 1), :]

    def rows_of(slab):
        return pl.ds(pl.multiple_of(slab * SCAN_SLAB, SCAN_SLAB), SCAN_SLAB)

    def state_times_a(t, slab):
        rows = rows_of(slab)
        acc = [zero, zero]
        for j in range(n):
            acc[j % 2] = acc[j % 2] + s_ref[j, rows, :] * row(a_ref, t, j)
        return acc[0] + acc[1]

    def update_and_read(t, slab, sa):
        rows = rows_of(slab)
        vt = v_ref[t, rows, :]
        yac = [zero, zero]
        for j in range(n):
            sj = (s_ref[j, rows, :] * row(w_ref, t, j) + sa * row(b_ref, t, j)
                  + vt * row(k_ref, t, j))
            s_ref[j, rows, :] = sj
            yac[j % 2] = yac[j % 2] + sj * row(r_ref, t, j)
        y_ref[t, rows, :] = yac[0] + yac[1]

    def trip(q, sa):
        t, slab = q // slabs, q % slabs
        nxt = jnp.minimum(q + 1, tc * slabs - 1)
        update_and_read(t, slab, sa)
        return state_times_a(nxt // slabs, nxt % slabs)

    lax.fori_loop(0, tc * slabs, trip, state_times_a(0, 0))


def _rwkv_scan_and_moe_weight_prep(r, w, k, v, a, b, w1, w2):
    s, n, l = r.shape
    tc = min(SCAN_STEPS, s)
    assert s % tc == 0
    steps = s // tc
    assert w1.shape[-1] % W1_SPLIT_SUB == 0

    def sliced(wgt, cols_out):
        e, rows, cols = wgt.shape
        assert (e * rows) % steps == 0
        per_step = e * rows // steps
        assert rows % per_step == 0 and per_step % (2 * SUBLANES) == 0
        per_e = rows // per_step
        spec = lambda c: pl.BlockSpec((None, per_step, c), lambda i: (i // per_e, i % per_e, 0))
        return spec(cols), spec(cols_out), jax.ShapeDtypeStruct((e, rows, cols_out), BF16)

    w1_in, w1_out, w1_shape = sliced(w1, w1.shape[-1] // 2)
    w2_in, w2_out, w2_shape = sliced(w2, w2.shape[-1])
    blk = pl.BlockSpec((tc, n, l), lambda i: (i, 0, 0))
    return pl.pallas_call(
        functools.partial(_rwkv_scan_kernel, tc=tc),
        grid=(steps,), in_specs=[blk] * 6 + [w1_in, w2_in],
        out_specs=[blk, w1_out, w1_out, w2_out],
        out_shape=[jax.ShapeDtypeStruct((s, n, l), F32), w1_shape, w1_shape, w2_shape],
        scratch_shapes=[pltpu.VMEM((n, n, l), F32)],
        compiler_params=_cparams("arbitrary"), name="rwkv_scan",
    )(r, w, k, v, a, b, w1, w2)


RWKV_SEG = 256
RWKV_PREP_ROWS = 128
SUBLANES = 8


def _head_sum(x):
    seg = RWKV_SEG
    src = lax.broadcasted_iota(jnp.int32, (seg, seg), 0) // RWKV_HEAD
    dst = lax.broadcasted_iota(jnp.int32, (seg, seg), 1) // RWKV_HEAD
    ones = (src == dst).astype(BF16)
    hi = x.astype(BF16)
    lo = (x - hi.astype(F32)).astype(BF16)
    parts = []
    for c in range(x.shape[-1] // seg):
        sl = slice(c * seg, (c + 1) * seg)
        parts.append(jnp.dot(hi[:, sl], ones, preferred_element_type=F32)
                     + jnp.dot(lo[:, sl], ones, preferred_element_type=F32))
    return jnp.concatenate(parts, axis=-1)


def _token_shift_rows(p, last_prev_row, mix, first):
    rolled = pltpu.roll(p, 1, 0)
    row0 = jnp.where(first, jnp.zeros_like(last_prev_row), last_prev_row)
    t = lax.broadcasted_iota(jnp.int32, p.shape, 0)
    prev = jnp.where(t == 0, row0, rolled)
    return p + (prev - p) * mix


def _rwkv_prep_kernel(rkv_ref, rkvp_ref, lo_ref, lop_ref, mixr_ref, mixl_ref, w0_ref, a0_ref,
                      kk_ref, ka_ref, rk_ref, wup_ref, aup_ref, gup_ref,
                      r_ref, w_ref, k_ref, v_ref, al_ref, be_ref, g_ref, bo_ref):
    first = pl.program_id(1) == 0
    d = D_MODEL

    def seg(c):
        sl = slice(c * d, (c + 1) * d)
        return _token_shift_rows(rkv_ref[:, sl], rkvp_ref[SUBLANES - 1:SUBLANES, sl],
                                 mixr_ref[:, sl], first)

    r, k, v = seg(0), seg(1), seg(2)
    zl = _token_shift_rows(lo_ref[...], lop_ref[SUBLANES - 1:SUBLANES, :], mixl_ref[...], first)
    wd = zl[:, :DECAY_LORA]
    ad = zl[:, DECAY_LORA:DECAY_LORA + AAA_LORA]
    gd = zl[:, DECAY_LORA + AAA_LORA:]
    mm = lambda a, w_ref_: jnp.dot(a.astype(BF16), w_ref_[...].astype(BF16),
                                   preferred_element_type=F32)
    z = -(w0_ref[...] + mm(jnp.tanh(wd), wup_ref))
    softplus = jnp.maximum(z, 0.0) + jnp.log1p(jnp.exp(-jnp.abs(z)))
    w = -softplus - 0.5
    a = jax.nn.sigmoid(a0_ref[...] + mm(ad, aup_ref))
    kk = k * kk_ref[...]
    kk = kk / jnp.maximum(jnp.sqrt(_head_sum(kk * kk)), 1e-12)
    k2 = k * (1.0 + (a - 1.0) * ka_ref[...])
    r_ref[...] = r
    w_ref[...] = jnp.exp(-jnp.exp(w))
    k_ref[...] = k2
    v_ref[...] = v
    al_ref[...] = -kk
    be_ref[...] = kk * a
    g_ref[...] = mm(jax.nn.sigmoid(gd), gup_ref)
    bo_ref[...] = _head_sum(r * k2 * rk_ref[...]) * v


def _rwkv_prep(p_rkv, p_lora, shift_mix, w0, w_up, a0, a_up, g_up, k_k, k_a, r_k):
    b, s, c_rkv = p_rkv.shape
    d, ts, c_lo = D_MODEL, min(RWKV_PREP_ROWS, s), p_lora.shape[-1]
    assert s % ts == 0 and ts % SUBLANES == 0
    cur = lambda c: pl.BlockSpec((None, ts, c), lambda i, j: (i, j, 0))
    prev = lambda c: pl.BlockSpec(
        (None, SUBLANES, c), lambda i, j: (i, jnp.maximum(j * (ts // SUBLANES) - 1, 0), 0))
    vec = lambda c: pl.BlockSpec((1, c), lambda i, j: (0, 0))
    full = lambda a: pl.BlockSpec(a.shape, lambda i, j: (0, 0))
    out = jax.ShapeDtypeStruct((b, s, d), F32)
    row = lambda t: t.reshape(1, -1)
    return pl.pallas_call(
        _rwkv_prep_kernel, grid=(b, s // ts),
        in_specs=[cur(c_rkv), prev(c_rkv), cur(c_lo), prev(c_lo), vec(c_rkv), vec(c_lo),
                  vec(d), vec(d), vec(d), vec(d), vec(d), full(w_up), full(a_up), full(g_up)],
        out_specs=[cur(d)] * 8, out_shape=[out] * 8,
        compiler_params=_cparams("parallel", "parallel"), name="rwkv_prep",
    )(p_rkv, p_rkv, p_lora, p_lora, row(shift_mix[:c_rkv]), row(shift_mix[c_rkv:]), row(w0),
      row(a0), row(k_k), row(k_a), row(r_k), w_up, a_up, g_up)


def _proj_a_kernel(y_ref, bo_ref, g_ref, lg_ref, lb_ref, w_ref, o_ref):
    y = y_ref[...]
    inv_n = 1.0 / RWKV_HEAD
    c = y - _head_sum(y) * inv_n
    var = _head_sum(c * c) * inv_n
    ya = (c * lax.rsqrt(var + GN_EPS) * lg_ref[...] + lb_ref[...] + bo_ref[...]) * g_ref[...]
    o_ref[...] = jnp.dot(ya.astype(BF16), w_ref[...], preferred_element_type=F32)


def _rwkv_post_proj(y, bonus, gate, lnx_g, lnx_b, proj_a, *, tm=256):
    m, d = y.shape
    assert m % tm == 0
    rowb = pl.BlockSpec((tm, d), lambda i: (i, 0))
    vec = pl.BlockSpec((1, d), lambda i: (0, 0))
    return pl.pallas_call(
        _proj_a_kernel, grid=(m // tm,),
        in_specs=[rowb, rowb, rowb, vec, vec, pl.BlockSpec((d, d), lambda i: (0, 0))],
        out_specs=rowb, out_shape=jax.ShapeDtypeStruct((m, d), F32),
        compiler_params=_cparams("parallel"), name="rwkv_post_proj_a",
    )(y, bonus, gate, lnx_g.reshape(1, d), lnx_b.reshape(1, d), proj_a.astype(BF16))


def _cross_attn_kernel(q_ref, k_ref, v_ref, o_ref):
    scale = CA_HEAD_DIM ** -0.5
    for h in range(CA_HEADS):
        sl = slice(h * CA_HEAD_DIM, (h + 1) * CA_HEAD_DIM)
        q = q_ref[:, sl].astype(BF16)
        k = k_ref[:, sl].astype(BF16)
        s = lax.dot_general(q, k, (((1,), (1,)), ((), ())), preferred_element_type=F32) * scale
        m = jnp.max(s, axis=-1, keepdims=True)
        p = jnp.exp(s - m)
        den = jnp.sum(p, axis=-1, keepdims=True)
        o = jnp.dot(p.astype(BF16), v_ref[:, sl].astype(BF16), preferred_element_type=F32)
        o_ref[:, sl] = (o / den).astype(o_ref.dtype)


def _cross_attention(q, kv, *, tq=512):
    b, s, d = q.shape
    mlen = kv.shape[1]
    tq = min(tq, s)
    return pl.pallas_call(
        _cross_attn_kernel, grid=(b, s // tq),
        in_specs=[pl.BlockSpec((None, tq, d), lambda i, j: (i, j, 0)),
                  pl.BlockSpec((None, mlen, d), lambda i, j: (i, 0, 0)),
                  pl.BlockSpec((None, mlen, d), lambda i, j: (i, 0, 1))],
        out_specs=pl.BlockSpec((None, tq, d), lambda i, j: (i, j, 0)),
        out_shape=jax.ShapeDtypeStruct((b, s, d), BF16),
        compiler_params=_cparams("parallel", "parallel"), name="cross_attn",
    )(q, kv, kv)


def _router_kernel(x_ref, w_ref, b_ref, o_ref):
    o_ref[...] = jnp.dot(x_ref[...], w_ref[...], precision=lax.Precision.HIGHEST,
                         preferred_element_type=F32) + b_ref[...]


def _router_logits(x, w, b, *, tm=512):
    m, d = x.shape
    e = w.shape[1]
    return pl.pallas_call(
        _router_kernel, grid=(m // tm,),
        in_specs=[pl.BlockSpec((tm, d), lambda i: (i, 0)),
                  pl.BlockSpec((d, e), lambda i: (0, 0)),
                  pl.BlockSpec((1, e), lambda i: (0, 0))],
        out_specs=pl.BlockSpec((tm, e), lambda i: (i, 0)),
        out_shape=jax.ShapeDtypeStruct((m, e), F32),
        compiler_params=_cparams("parallel"), name="router",
    )(x, w, b.reshape(1, e))


def _moe_kernel(be_ref, nused_ref, x_ref, g_ref, w1g_ref, w1l_ref, b1g_ref, b1l_ref, w2_ref,
                b2_ref, o_ref):
    i = pl.program_id(0)

    @pl.when(i < nused_ref[0])
    def _():
        x = x_ref[...]
        glu = jnp.dot(x, w1g_ref[...], preferred_element_type=F32) + b1g_ref[...]
        lin = jnp.dot(x, w1l_ref[...], preferred_element_type=F32) + b1l_ref[...]
        glu = jnp.minimum(glu, SWIGLU_LIMIT)
        lin = jnp.clip(lin, -SWIGLU_LIMIT, SWIGLU_LIMIT)
        act = glu * jax.nn.sigmoid(SWIGLU_ALPHA * glu) * (lin + 1.0)
        y = jnp.dot(act.astype(BF16), w2_ref[...], preferred_element_type=F32) + b2_ref[...]
        o_ref[...] = y * g_ref[...]

    @pl.when(i >= nused_ref[0])
    def _():
        o_ref[...] = jnp.zeros_like(o_ref)


def _moe_experts(blk_exp, n_used, xs, row_gate, w1g, w1l, b1g, b1l, w2, b2):
    rows, d = xs.shape
    nblk = rows // MOE_BLK
    f = w1g.shape[-1]
    wspec = lambda shape: pl.BlockSpec(shape, lambda i, be, nu: (be[i], 0, 0))
    grid_spec = pltpu.PrefetchScalarGridSpec(
        num_scalar_prefetch=2, grid=(nblk,),
        in_specs=[pl.BlockSpec((MOE_BLK, d), lambda i, be, nu: (i, 0)),
                  pl.BlockSpec((MOE_BLK, 1), lambda i, be, nu: (i, 0)),
                  wspec((None, d, f)), wspec((None, d, f)),
                  wspec((None, 1, f)), wspec((None, 1, f)),
                  wspec((None, f, d)), wspec((None, 1, d))],
        out_specs=pl.BlockSpec((MOE_BLK, d), lambda i, be, nu: (i, 0)),
    )
    return pl.pallas_call(
        _moe_kernel, grid_spec=grid_spec,
        out_shape=jax.ShapeDtypeStruct((rows, d), F32),
        compiler_params=pltpu.CompilerParams(dimension_semantics=("arbitrary",),
                                             vmem_limit_bytes=V7X_VMEM_LIMIT_MOE_BYTES),
        name="moe_experts",
    )(blk_exp, n_used, xs, row_gate.reshape(rows, 1), w1g, w1l, b1g, b1l, w2, b2)


def _per_expert(table, idx):
    hit = idx[:, None] == jnp.arange(N_EXPERTS, dtype=idx.dtype)[None, :]
    return jnp.sum(jnp.where(hit, table[None, :], 0), axis=1)


def _moe_ffn(x_bf, logits, moe_weights, b1, b2):
    n, d = x_bf.shape
    top_val, top_idx = lax.top_k(logits, TOP_K)
    top_w = jax.nn.softmax(top_val, axis=-1)
    flat_e = top_idx.reshape(-1).astype(jnp.int32)
    order = jnp.argsort(flat_e).astype(jnp.int32)
    rank = jnp.argsort(order).astype(jnp.int32)
    counts = jnp.sum(flat_e[:, None] == jnp.arange(N_EXPERTS, dtype=jnp.int32)[None, :],
                     axis=0, dtype=jnp.int32)
    starts = jnp.cumsum(counts) - counts
    padded = (counts + MOE_BLK - 1) // MOE_BLK * MOE_BLK
    pends = jnp.cumsum(padded)
    pstarts = pends - padded
    rows = n * TOP_K + N_EXPERTS * MOE_BLK
    nblk = rows // MOE_BLK
    blk_start = jnp.arange(nblk, dtype=jnp.int32) * MOE_BLK
    blk_exp = jnp.minimum(jnp.sum(pends[None, :] <= blk_start[:, None], axis=1, dtype=jnp.int32),
                          N_EXPERTS - 1)
    n_used = (pends[-1] // MOE_BLK).astype(jnp.int32).reshape(1)
    row_exp = jnp.repeat(blk_exp, MOE_BLK)
    off = jnp.arange(rows, dtype=jnp.int32) - _per_expert(pstarts, row_exp)
    valid = off < _per_expert(counts, row_exp)
    assign = order[jnp.where(valid, _per_expert(starts, row_exp) + off, 0)]
    row_tok = jnp.where(valid, assign // TOP_K, 0)
    row_gate = jnp.where(valid, top_w.reshape(-1)[assign], 0.0)
    xs = x_bf[row_tok]
    b1g = b1[:, None, 0::2]
    b1l = b1[:, None, 1::2]
    w1g, w1l, w2 = moe_weights
    ys = _moe_experts(blk_exp, n_used, xs, row_gate, w1g, w1l, b1g, b1l, w2,
                      b2[:, None, :])
    pos = (_per_expert(pstarts, flat_e) + rank - _per_expert(starts, flat_e)).reshape(n, TOP_K)
    return [ys[pos[:, kk]] for kk in range(TOP_K)]


def _to_scan_layout(t, bsz, s):
    return t.reshape(bsz, s, RWKV_HEADS, RWKV_HEAD).transpose(1, 3, 0, 2).reshape(
        s, RWKV_HEAD, bsz * RWKV_HEADS)


def _from_scan_layout(t, bsz, s):
    return t.reshape(s, RWKV_HEAD, bsz, RWKV_HEADS).transpose(2, 0, 3, 1).reshape(bsz, s, D_MODEL)


def _rwkv7_branch_proj(p_rkv, p_lora, shift_mix, w0, w_up, a0, a_up, g_up, k_k, k_a, r_k, lnx_g,
                       lnx_b, proj_a, moe_w1, moe_w2):
    bsz, s, _ = p_rkv.shape
    r, decay, k, v, alpha, beta, gate, bonus = _rwkv_prep(
        p_rkv, p_lora, shift_mix, w0, w_up, a0, a_up, g_up, k_k, k_a, r_k)
    lay = lambda t: _to_scan_layout(t, bsz, s)
    y, *moe_weights = _rwkv_scan_and_moe_weight_prep(lay(r), lay(decay), lay(k), lay(v),
                                                     lay(alpha), lay(beta), moe_w1, moe_w2)
    y = _from_scan_layout(y, bsz, s)
    flat = lambda t: t.reshape(bsz * s, D_MODEL)
    pa = _rwkv_post_proj(flat(y), flat(bonus), flat(gate), lnx_g, lnx_b, proj_a)
    return pa, moe_weights


def _hybrid_mixer(x, w_in, shift_mix, w0, w_up, a0, a_up, g_up, k_k, k_a, r_k, lnx_g, lnx_b,
                  proj_a, proj_b, moe_w1, moe_w2):
    bsz, s, d = x.shape
    n_tok = bsz * s
    x_bf = x.astype(BF16)
    x2d = x_bf.reshape(n_tok, d)
    c0, c1, c2 = 3 * D_MODEL, RWKV_COLS, RWKV_COLS + ATTN_COLS
    p_rkv = _matmul(x2d, w_in[:, :c0], tm=1024, tn=1024, name="in_rkv").reshape(bsz, s, -1)
    p_lora = _matmul(x2d, w_in[:, c0:c1], tm=1024, tn=512, name="in_lora").reshape(bsz, s, -1)
    p_gate = _matmul(x2d, w_in[:, c2:], tm=1024, tn=1024, name="in_gate")

    pa, moe_weights = _rwkv7_branch_proj(p_rkv, p_lora, shift_mix, w0, w_up, a0, a_up, g_up, k_k,
                                         k_a, r_k, lnx_g, lnx_b, proj_a, moe_w1, moe_w2)

    cos_t, sin_t = _rotary_tables(s)
    views = _attn_in_proj(x_bf, w_in[:, c1:c2], cos_t, sin_t)
    outs, lses = [], []
    for gi, (window, dilation) in enumerate(DIL_GROUPS):
        o, l = _dilated_attention(views[gi], gi, window, dilation)
        outs.append(o)
        lses.append(l)
    return _mix_merge(outs, lses, pa, p_gate, proj_b, bsz, s), moe_weights


def _memory_cross_attention(x_bf, mem, wq, wkv, bsz, s):
    d = x_bf.shape[-1]
    q = _matmul(x_bf, wq, tm=1024, tn=1024, name="ca_q").reshape(bsz, s, d)
    kv = _matmul(mem.reshape(-1, d), wkv, tm=512, tn=1024, name="ca_kv").reshape(bsz, -1, 2 * d)
    return _cross_attention(q, kv).reshape(-1, d)


def kernel(x, mem, w_in, shift_mix, w0, w_up, a0, a_up, g_up, k_k, k_a, r_k, lnx_g, lnx_b, proj_a, proj_b, w_out, ln1_g, ln1_b, ca_wq, ca_wkv, ca_wo, ln2_g, ln2_b, router_w, router_b, moe_w1, moe_b1, moe_w2, moe_b2, ln3_g, ln3_b):
    bsz, s, d = x.shape
    n_tok = bsz * s
    for l in range(DEPTH):
        merged, moe_weights = _hybrid_mixer(x, w_in[l], shift_mix[l], w0[l], w_up[l], a0[l],
                                            a_up[l], g_up[l], k_k[l], k_a[l], r_k[l], lnx_g[l],
                                            lnx_b[l], proj_a[l], proj_b[l], moe_w1[l], moe_w2[l])
        x1, x1_bf = _matmul_res_ln(merged, w_out[l], x.reshape(n_tok, d), ln1_g[l], ln1_b[l],
                                   name="w_out_ln1")
        o = _memory_cross_attention(x1_bf, mem, ca_wq[l], ca_wkv[l], bsz, s)
        x2, x2_bf = _matmul_res_ln(o, ca_wo[l], x1, ln2_g[l], ln2_b[l], name="ca_o_ln2")
        logits = _router_logits(x2, router_w[l], router_b[l])
        hs = _moe_ffn(x2_bf, logits, moe_weights, moe_b1[l], moe_b2[l])
        x = _res_layer_norm(x2, hs, ln3_g[l], ln3_b[l], name="ln3").reshape(bsz, s, d)
    return x
```

```python
import functools

import jax
import jax.numpy as jnp
import numpy as np
from jax import lax
from jax.experimental import pallas as pl
from jax.experimental.pallas import tpu as pltpu

F32 = jnp.float32
BF16 = jnp.bfloat16

D_MODEL = 2048
RWKV_HEAD = 64
RWKV_HEADS = D_MODEL // RWKV_HEAD
DECAY_LORA = 96
AAA_LORA = 96
GATE_LORA = 256
GN_EPS = 64e-5
DIL_GROUPS = ((128, 1), (512, 4), (2048, 16))
N_GROUPS = len(DIL_GROUPS)
DIL_HEADS = 8
DIL_HEAD_DIM = 64
DIL_GROUP_W = DIL_HEADS * DIL_HEAD_DIM
DIL_DIM = N_GROUPS * DIL_GROUP_W
BLK = 128
ROPE_THETA = 10000.0
NEG_INF = -1e30
RWKV_COLS = 3 * D_MODEL + DECAY_LORA + AAA_LORA + GATE_LORA
ATTN_COLS = 3 * DIL_DIM
CA_HEADS = 4
CA_HEAD_DIM = D_MODEL // CA_HEADS
N_EXPERTS = 32
TOP_K = 4
D_FF = D_MODEL
SWIGLU_LIMIT = 7.0
SWIGLU_ALPHA = 1.702
MOE_BLK = 128
LN_EPS = 1e-5
DEPTH = 1
DEEPNORM_ALPHA = (2 * DEPTH) ** 0.25

V7X_VMEM_LIMIT_BYTES = 56 * 1024 * 1024
V7X_VMEM_LIMIT_MOE_BYTES = 61 * 1024 * 1024


def _cparams(*sem):
    return pltpu.CompilerParams(dimension_semantics=sem, vmem_limit_bytes=V7X_VMEM_LIMIT_BYTES)


def _mm_kernel(a_ref, b_ref, o_ref):
    o_ref[...] = jnp.dot(a_ref[...].astype(BF16), b_ref[...].astype(BF16),
                         preferred_element_type=F32).astype(o_ref.dtype)


def _matmul(a, b, *, tm, tn, name, out_dtype=F32):
    m, k = a.shape
    _, n = b.shape
    tm, tn = min(tm, m), min(tn, n)
    assert m % tm == 0 and n % tn == 0, (m, n, tm, tn)
    return pl.pallas_call(
        _mm_kernel,
        grid=(n // tn, m // tm),
        in_specs=[pl.BlockSpec((tm, k), lambda j, i: (i, 0)),
                  pl.BlockSpec((k, tn), lambda j, i: (0, j))],
        out_specs=pl.BlockSpec((tm, tn), lambda j, i: (i, j)),
        out_shape=jax.ShapeDtypeStruct((m, n), out_dtype),
        compiler_params=_cparams("parallel", "parallel"),
        name=name,
    )(a, b)


def _ln_kernel(x_ref, *rest):
    *h_refs, g_ref, b_ref, o_ref = rest
    t = DEEPNORM_ALPHA * x_ref[...]
    for h_ref in h_refs:
        t = t + h_ref[...]
    mu = jnp.mean(t, axis=-1, keepdims=True)
    c = t - mu
    var = jnp.mean(c * c, axis=-1, keepdims=True)
    o_ref[...] = c * lax.rsqrt(var + LN_EPS) * g_ref[...] + b_ref[...]


def _res_layer_norm(x, hs, g, b, *, name, tm=256):
    m, d = x.shape
    assert m % tm == 0
    row = pl.BlockSpec((tm, d), lambda i: (i, 0))
    vec = pl.BlockSpec((1, d), lambda i: (0, 0))
    return pl.pallas_call(
        _ln_kernel, grid=(m // tm,), in_specs=[row] * (1 + len(hs)) + [vec, vec], out_specs=row,
        out_shape=jax.ShapeDtypeStruct((m, d), F32),
        compiler_params=_cparams("parallel"), name=name,
    )(x, *hs, g.reshape(1, d), b.reshape(1, d))


def _mm_res_ln_kernel(a_ref, w_ref, x_ref, g_ref, b_ref, o_ref, ob_ref):
    h = jnp.dot(a_ref[...].astype(BF16), w_ref[...], preferred_element_type=F32)
    t = DEEPNORM_ALPHA * x_ref[...] + h
    mu = jnp.mean(t, axis=-1, keepdims=True)
    c = t - mu
    var = jnp.mean(c * c, axis=-1, keepdims=True)
    y = c * lax.rsqrt(var + LN_EPS) * g_ref[...] + b_ref[...]
    o_ref[...] = y
    ob_ref[...] = y.astype(BF16)


def _matmul_res_ln(a, w, x, g, b, *, name, tm=512):
    m, k = a.shape
    d = w.shape[1]
    assert m % tm == 0
    row = lambda c: pl.BlockSpec((tm, c), lambda i: (i, 0))
    const = lambda r, c: pl.BlockSpec((r, c), lambda i: (0, 0))
    in_specs = [row(k), const(k, d), row(d), const(1, d), const(1, d)]
    args = [a, w.astype(BF16), x, g.reshape(1, d), b.reshape(1, d)]
    out_specs = [row(d), row(d)]
    out_shape = [jax.ShapeDtypeStruct((m, d), F32), jax.ShapeDtypeStruct((m, d), BF16)]
    return pl.pallas_call(
        _mm_res_ln_kernel,
        grid=(m // tm,), in_specs=in_specs, out_specs=out_specs, out_shape=out_shape,
        compiler_params=_cparams("parallel"), name=name,
    )(*args)


LANES = 128
ATTN_ROWS = 256
GROUP_QKV_W = 3 * DIL_GROUP_W
LANE_CHUNKS = DIL_GROUP_W // LANES


def _rotary_tables(s):
    half = DIL_HEAD_DIM // 2
    inv = ROPE_THETA ** (-jnp.arange(half, dtype=F32) * 2.0 / DIL_HEAD_DIM)
    ang = jnp.arange(s, dtype=F32)[:, None] * inv[None, :]
    cos, sin = jnp.cos(ang), jnp.sin(ang)
    return (jnp.concatenate([cos, cos, cos, cos], axis=-1),
            jnp.concatenate([-sin, sin, -sin, sin], axis=-1))


def _attn_in_kernel(x_ref, w_ref, cos_ref, sin_ref, o0_ref, o1_ref, o2_ref, scr_ref):
    tm = x_ref.shape[0]
    acc = jnp.dot(x_ref[...].astype(BF16), w_ref[...], preferred_element_type=F32)
    cos, sin = cos_ref[...], sin_ref[...]
    lane = lax.broadcasted_iota(jnp.int32, (tm, LANES), 1)
    first_half = (lane % DIL_HEAD_DIM) < (DIL_HEAD_DIM // 2)
    outs = (o0_ref, o1_ref, o2_ref)
    slab = 0
    for part in range(3):
        for gi, (_, d) in enumerate(DIL_GROUPS):
            for c in range(LANE_CHUNKS):
                col = part * DIL_DIM + gi * DIL_GROUP_W + c * LANES
                x = acc[:, col:col + LANES]
                if part < 2:
                    partner = jnp.where(first_half, pltpu.roll(x, LANES - 32, 1),
                                        pltpu.roll(x, 32, 1))
                    x = x * cos + partner * sin
                dst = part * DIL_GROUP_W + c * LANES
                if d == 1:
                    outs[gi][:, dst:dst + LANES] = x
                else:
                    scr_ref[slab] = x
                    for r in range(d):
                        outs[gi][:, r * GROUP_QKV_W + dst:r * GROUP_QKV_W + dst + LANES] = (
                            scr_ref[slab, pl.ds(r, tm // d, stride=d), :])
                    slab += 1


def _attn_in_proj(x, w_attn, cos_t, sin_t):
    b, s, dm = x.shape
    tm = min(ATTN_ROWS, s)
    dmax = max(d for _, d in DIL_GROUPS)
    assert s % tm == 0 and tm % (dmax * 8) == 0
    n_strided = sum(3 * LANE_CHUNKS for _, d in DIL_GROUPS if d > 1)
    tab = pl.BlockSpec((tm, LANES), lambda i, j: (j, 0))
    ospec = lambda d: pl.BlockSpec((None, tm // d, d * GROUP_QKV_W), lambda i, j: (i, j, 0))
    return pl.pallas_call(
        _attn_in_kernel, grid=(b, s // tm),
        in_specs=[pl.BlockSpec((None, tm, dm), lambda i, j: (i, j, 0)),
                  pl.BlockSpec(w_attn.shape, lambda i, j: (0, 0), pipeline_mode=pl.Buffered(1)),
                  tab, tab],
        out_specs=[ospec(d) for _, d in DIL_GROUPS],
        out_shape=[jax.ShapeDtypeStruct((b, s // d, d * GROUP_QKV_W), F32) for _, d in DIL_GROUPS],
        scratch_shapes=[pltpu.VMEM((n_strided, tm, LANES), F32)],
        compiler_params=_cparams("parallel", "parallel"), name="in_attn_rope",
    )(x, w_attn.astype(BF16), cos_t, sin_t)


def _dil_attn_kernel(q_ref, kp_ref, kc_ref, vp_ref, vc_ref, o_ref, l_ref, *, span):
    nb = pl.program_id(2)
    qi = lax.broadcasted_iota(jnp.int32, (BLK, 2 * BLK), 0) + BLK
    ki = lax.broadcasted_iota(jnp.int32, (BLK, 2 * BLK), 1)
    dist = qi - ki
    mask = (dist >= 0) & (dist <= span) & ((nb > 0) | (ki >= BLK))
    scale = DIL_HEAD_DIM ** -0.5
    q = q_ref[...].astype(BF16)
    k = jnp.concatenate([kp_ref[...], kc_ref[...]], axis=0).astype(BF16)
    v = jnp.concatenate([vp_ref[...], vc_ref[...]], axis=0).astype(BF16)
    for h in range(DIL_HEADS):
        sl = slice(h * DIL_HEAD_DIM, (h + 1) * DIL_HEAD_DIM)
        s = lax.dot_general(q[:, sl], k[:, sl], (((1,), (1,)), ((), ())),
                            preferred_element_type=F32) * scale
        s = jnp.where(mask, s, NEG_INF)
        m = jnp.max(s, axis=-1, keepdims=True)
        p = jnp.exp(s - m)
        den = jnp.sum(p, axis=-1, keepdims=True)
        o = jnp.dot(p.astype(BF16), v[:, sl], preferred_element_type=F32)
        o_ref[:, sl] = o / den
        l_ref[:, sl] = jnp.broadcast_to(m + jnp.log(den), (BLK, DIL_HEAD_DIM))


def _dilated_attention(qkv_view, gi, window, dilation):
    b, n, _ = qkv_view.shape
    d = dilation
    assert n % BLK == 0
    nblk = n // BLK
    blk = (None, BLK, DIL_GROUP_W)

    def col(which):
        return lambda bi, r, nb: (bi, nb, r * 3 + which)

    def col_prev(which):
        return lambda bi, r, nb: (bi, jnp.maximum(nb - 1, 0), r * 3 + which)

    out_spec = pl.BlockSpec(blk, lambda bi, r, nb: (bi, nb, r))
    return pl.pallas_call(
        functools.partial(_dil_attn_kernel, span=window // dilation),
        grid=(b, d, nblk),
        in_specs=[pl.BlockSpec(blk, col(0)),
                  pl.BlockSpec(blk, col_prev(1)), pl.BlockSpec(blk, col(1)),
                  pl.BlockSpec(blk, col_prev(2)), pl.BlockSpec(blk, col(2))],
        out_specs=[out_spec, out_spec],
        out_shape=[jax.ShapeDtypeStruct((b, n, d * DIL_GROUP_W), F32)] * 2,
        compiler_params=_cparams("parallel", "parallel", "arbitrary"),
        name=f"dil_attn_g{gi}",
    )(qkv_view, qkv_view, qkv_view, qkv_view, qkv_view)


def _mix_merge_kernel(o0_ref, l0_ref, o1_ref, l1_ref, o2_ref, l2_ref, pa_ref, pg_ref, w_ref,
                      out_ref, scr_ref):
    tm, dm = pa_ref.shape
    slab = [0]

    def natural(ref, d):
        if d == 1:
            return [ref[:, c * LANES:(c + 1) * LANES] for c in range(LANE_CHUNKS)]
        chunks = []
        for c in range(LANE_CHUNKS):
            for r in range(d):
                scr_ref[slab[0], pl.ds(r, tm // d, stride=d), :] = (
                    ref[:, r * DIL_GROUP_W + c * LANES:r * DIL_GROUP_W + (c + 1) * LANES])
            chunks.append(scr_ref[slab[0]])
            slab[0] += 1
        return chunks

    dils = [d for _, d in DIL_GROUPS]
    o = [natural(r, d) for r, d in zip((o0_ref, o1_ref, o2_ref), dils)]
    l = [natural(r, d) for r, d in zip((l0_ref, l1_ref, l2_ref), dils)]
    yb = []
    for c in range(LANE_CHUNKS):
        m = jnp.maximum(jnp.maximum(l[0][c], l[1][c]), l[2][c])
        e = [jnp.exp(l[g][c] - m) for g in range(N_GROUPS)]
        den = e[0] + e[1] + e[2]
        yb.append((e[0] / den) * o[0][c] + (e[1] / den) * o[1][c] + (e[2] / den) * o[2][c])
    pb = jnp.dot(jnp.concatenate(yb, axis=-1).astype(BF16), w_ref[...], preferred_element_type=F32)
    out_ref[...] = (jax.nn.sigmoid(pg_ref[:, :dm]) * pa_ref[...]
                    + jax.nn.sigmoid(pg_ref[:, dm:]) * pb).astype(out_ref.dtype)


def _mix_merge(outs, lses, pa, p_gate, proj_b, bsz, s):
    n_tok, dm = pa.shape
    tm = min(ATTN_ROWS, s)
    per_b = s // tm
    n_strided = sum(2 * LANE_CHUNKS for _, d in DIL_GROUPS if d > 1)
    vspec = lambda d: pl.BlockSpec((None, tm // d, d * DIL_GROUP_W),
                                   lambda i: (i // per_b, i % per_b, 0))
    views = []
    for (_, d), o, l in zip(DIL_GROUPS, outs, lses):
        views += [(o, vspec(d)), (l, vspec(d))]
    row = lambda w: pl.BlockSpec((tm, w), lambda i: (i, 0))
    return pl.pallas_call(
        _mix_merge_kernel, grid=(n_tok // tm,),
        in_specs=[sp for _, sp in views] + [row(dm), row(2 * dm),
                                            pl.BlockSpec(proj_b.shape, lambda i: (0, 0))],
        out_specs=row(dm), out_shape=jax.ShapeDtypeStruct((n_tok, dm), BF16),
        scratch_shapes=[pltpu.VMEM((n_strided, tm, LANES), F32)],
        compiler_params=_cparams("parallel"), name="attn_mix_merge",
    )(*[a for a, _ in views], pa, p_gate, proj_b.astype(BF16))


W1_SPLIT_SUB = 256
SCAN_STEPS = 16
SCAN_SLAB = 16


def _split_even_odd(w_ref, g_ref, l_ref):
    sub, half = W1_SPLIT_SUB, W1_SPLIT_SUB // 2
    src = lax.broadcasted_iota(jnp.int32, (sub, sub), 0)
    dst = lax.broadcasted_iota(jnp.int32, (sub, sub), 1)
    want = jnp.where(dst < half, 2 * dst, 2 * (dst - half) + 1)
    sel = (src == want).astype(BF16)
    for q in range(w_ref.shape[1] // sub):
        w = w_ref[:, q * sub:(q + 1) * sub].astype(BF16)
        r = jnp.dot(w, sel, preferred_element_type=F32).astype(BF16)
        g_ref[:, q * half:(q + 1) * half] = r[:, :half]
        l_ref[:, q * half:(q + 1) * half] = r[:, half:]


def _rwkv_scan_kernel(r_ref, w_ref, k_ref, a_ref, b_ref, v_ref, w1_ref, w2_ref, y_ref, w1g_ref,
                      w1l_ref, w2b_ref, s_ref, rows_ref, *, tc):
    n = RWKV_HEAD
    nb = r_ref.shape[0]

    @pl.when(pl.program_id(0) == 0)
    def _():
        s_ref[...] = jnp.zeros_like(s_ref)

    _split_even_odd(w1_ref, w1g_ref, w1l_ref)
    w2b_ref[...] = w2_ref[...].astype(BF16)

    lane_group = lax.broadcasted_iota(jnp.int32, (tc, LANES), 1) // RWKV_HEADS
    for op, ref in enumerate((r_ref, w_ref, k_ref, a_ref, b_ref)):
        for c in range(ref.shape[-1] // LANES):
            cols = [ref[bi, :, c * LANES:(c + 1) * LANES] for bi in range(nb)]
            for jm in range(nb):
                out = None
                for bi in range(nb):
                    shift = ((bi - jm) % nb) * RWKV_HEADS
                    piece = cols[bi] if shift == 0 else pltpu.roll(cols[bi], shift, 1)
                    out = piece if out is None else jnp.where(lane_group == bi, piece, out)
                rows_ref[op, c * nb + jm] = out
    r_op, w_op, k_op, a_op, b_op = range(5)

    slabs = n // SCAN_SLAB
    zero = jnp.zeros((SCAN_SLAB, s_ref.shape[2]), F32)

    def row(op, t, j):
        return rows_ref[op, j, pl.ds(t, 1), :]

    def rows_of(slab):
        return pl.ds(pl.multiple_of(slab * SCAN_SLAB, SCAN_SLAB), SCAN_SLAB)

    def state_times_a(t, slab):
        rows = rows_of(slab)
        acc = [zero, zero]
        for j in range(n):
            acc[j % 2] = acc[j % 2] + s_ref[j, rows, :] * row(a_op, t, j)
        return acc[0] + acc[1]

    def update_and_read(t, slab, sa):
        rows = rows_of(slab)
        vt = v_ref[t, rows, :]
        yac = [zero, zero]
        for j in range(n):
            sj = (s_ref[j, rows, :] * row(w_op, t, j) + sa * row(b_op, t, j)
                  + vt * row(k_op, t, j))
            s_ref[j, rows, :] = sj
            yac[j % 2] = yac[j % 2] + sj * row(r_op, t, j)
        y_ref[t, rows, :] = yac[0] + yac[1]

    def trip(q, sa):
        t, slab = q // slabs, q % slabs
        nxt = jnp.minimum(q + 1, tc * slabs - 1)
        update_and_read(t, slab, sa)
        return state_times_a(nxt // slabs, nxt % slabs)

    lax.fori_loop(0, tc * slabs, trip, state_times_a(0, 0))


def _rwkv_scan_and_moe_weight_prep(r, w, k, a, b, v, w1, w2):
    s, n, l = v.shape
    bsz, _, d = r.shape
    assert bsz * RWKV_HEADS == LANES == l, "the scan packs exactly (batch, head) onto the 128 lanes"
    tc = min(SCAN_STEPS, s)
    assert s % tc == 0
    steps = s // tc
    assert w1.shape[-1] % W1_SPLIT_SUB == 0

    def sliced(wgt, cols_out):
        e, rows, cols = wgt.shape
        assert (e * rows) % steps == 0
        per_step = e * rows // steps
        assert rows % per_step == 0 and per_step % (2 * SUBLANES) == 0
        per_e = rows // per_step
        spec = lambda c: pl.BlockSpec((None, per_step, c), lambda i: (i // per_e, i % per_e, 0))
        return spec(cols), spec(cols_out), jax.ShapeDtypeStruct((e, rows, cols_out), BF16)

    w1_in, w1_out, w1_shape = sliced(w1, w1.shape[-1] // 2)
    w2_in, w2_out, w2_shape = sliced(w2, w2.shape[-1])
    tok = pl.BlockSpec((bsz, tc, d), lambda i: (0, i, 0))
    blk = pl.BlockSpec((tc, n, l), lambda i: (i, 0, 0))
    return pl.pallas_call(
        functools.partial(_rwkv_scan_kernel, tc=tc),
        grid=(steps,), in_specs=[tok] * 5 + [blk, w1_in, w2_in],
        out_specs=[blk, w1_out, w1_out, w2_out],
        out_shape=[jax.ShapeDtypeStruct((s, n, l), F32), w1_shape, w1_shape, w2_shape],
        scratch_shapes=[pltpu.VMEM((n, n, l), F32), pltpu.VMEM((5, n, tc, l), F32)],
        compiler_params=_cparams("arbitrary"), name="rwkv_scan",
    )(r, w, k, a, b, v, w1, w2)


RWKV_PREP_ROWS = 128
SUBLANES = 8


def _head_sum(x):
    cols = x.shape[-1] // LANES
    acc = x[:, :LANES]
    for c in range(1, cols):
        acc = acc + x[:, c * LANES:(c + 1) * LANES]
    acc = acc + pltpu.roll(acc, RWKV_HEADS, 1)
    acc = acc + pltpu.roll(acc, 2 * RWKV_HEADS, 1)
    return jnp.concatenate([acc] * cols, axis=-1)


def _token_shift_rows(p, last_prev_row, mix, first):
    rolled = pltpu.roll(p, 1, 0)
    row0 = jnp.where(first, jnp.zeros_like(last_prev_row), last_prev_row)
    t = lax.broadcasted_iota(jnp.int32, p.shape, 0)
    prev = jnp.where(t == 0, row0, rolled)
    return p + (prev - p) * mix


def _rwkv_prep_kernel(rkv_ref, rkvp_ref, lo_ref, lop_ref, mixr_ref, mixl_ref, w0_ref, a0_ref,
                      kk_ref, ka_ref, rk_ref, wup_ref, aup_ref, gup_ref,
                      r_ref, w_ref, k_ref, v_ref, al_ref, be_ref, g_ref, bo_ref):
    first = pl.program_id(1) == 0
    d = D_MODEL

    def seg(c):
        sl = slice(c * d, (c + 1) * d)
        return _token_shift_rows(rkv_ref[:, sl], rkvp_ref[SUBLANES - 1:SUBLANES, sl],
                                 mixr_ref[:, sl], first)

    r, k, v = seg(0), seg(1), seg(2)
    zl = _token_shift_rows(lo_ref[...], lop_ref[SUBLANES - 1:SUBLANES, :], mixl_ref[...], first)
    wd = zl[:, :DECAY_LORA]
    ad = zl[:, DECAY_LORA:DECAY_LORA + AAA_LORA]
    gd = zl[:, DECAY_LORA + AAA_LORA:]
    mm = lambda a, w_ref_: jnp.dot(a.astype(BF16), w_ref_[...].astype(BF16),
                                   preferred_element_type=F32)
    z = -(w0_ref[...] + mm(jnp.tanh(wd), wup_ref))
    softplus = jnp.maximum(z, 0.0) + jnp.log1p(jnp.exp(-jnp.abs(z)))
    w = -softplus - 0.5
    a = jax.nn.sigmoid(a0_ref[...] + mm(ad, aup_ref))
    kk = k * kk_ref[...]
    kk = kk / jnp.maximum(jnp.sqrt(_head_sum(kk * kk)), 1e-12)
    k2 = k * (1.0 + (a - 1.0) * ka_ref[...])
    r_ref[...] = r
    w_ref[...] = jnp.exp(-jnp.exp(w))
    k_ref[...] = k2
    v_ref[...] = v
    al_ref[...] = -kk
    be_ref[...] = kk * a
    g_ref[...] = mm(jax.nn.sigmoid(gd), gup_ref)
    bo_ref[...] = _head_sum(r * k2 * rk_ref[...]) * v


def _rwkv_prep(p_rkv, p_lora, shift_mix, w0, w_up, a0, a_up, g_up, k_k, k_a, r_k):
    b, s, c_rkv = p_rkv.shape
    d, ts, c_lo = D_MODEL, min(RWKV_PREP_ROWS, s), p_lora.shape[-1]
    assert s % ts == 0 and ts % SUBLANES == 0
    cur = lambda c: pl.BlockSpec((None, ts, c), lambda i, j: (i, j, 0))
    prev = lambda c: pl.BlockSpec(
        (None, SUBLANES, c), lambda i, j: (i, jnp.maximum(j * (ts // SUBLANES) - 1, 0), 0))
    vec = lambda c: pl.BlockSpec((1, c), lambda i, j: (0, 0))
    full = lambda a: pl.BlockSpec(a.shape, lambda i, j: (0, 0))
    out = jax.ShapeDtypeStruct((b, s, d), F32)
    row = lambda t: t.reshape(1, -1)
    return pl.pallas_call(
        _rwkv_prep_kernel, grid=(b, s // ts),
        in_specs=[cur(c_rkv), prev(c_rkv), cur(c_lo), prev(c_lo), vec(c_rkv), vec(c_lo),
                  vec(d), vec(d), vec(d), vec(d), vec(d), full(w_up), full(a_up), full(g_up)],
        out_specs=[cur(d)] * 8, out_shape=[out] * 8,
        compiler_params=_cparams("parallel", "parallel"), name="rwkv_prep",
    )(p_rkv, p_rkv, p_lora, p_lora, row(shift_mix[:c_rkv]), row(shift_mix[c_rkv:]), row(w0),
      row(a0), row(k_k), row(k_a), row(r_k), w_up, a_up, g_up)


def _proj_a_kernel(y_ref, bo_ref, g_ref, lg_ref, lb_ref, w_ref, o_ref):
    y = y_ref[...]
    inv_n = 1.0 / RWKV_HEAD
    c = y - _head_sum(y) * inv_n
    var = _head_sum(c * c) * inv_n
    ya = (c * lax.rsqrt(var + GN_EPS) * lg_ref[...] + lb_ref[...] + bo_ref[...]) * g_ref[...]
    o_ref[...] = jnp.dot(ya.astype(BF16), w_ref[...], preferred_element_type=F32)


def _rwkv_post_proj(y, bonus, gate, lnx_g, lnx_b, proj_a, *, tm=256):
    m, d = y.shape
    assert m % tm == 0
    rowb = pl.BlockSpec((tm, d), lambda i: (i, 0))
    vec = pl.BlockSpec((1, d), lambda i: (0, 0))
    return pl.pallas_call(
        _proj_a_kernel, grid=(m // tm,),
        in_specs=[rowb, rowb, rowb, vec, vec, pl.BlockSpec((d, d), lambda i: (0, 0))],
        out_specs=rowb, out_shape=jax.ShapeDtypeStruct((m, d), F32),
        compiler_params=_cparams("parallel"), name="rwkv_post_proj_a",
    )(y, bonus, gate, lnx_g.reshape(1, d), lnx_b.reshape(1, d), proj_a.astype(BF16))


def _cross_attn_kernel(q_ref, k_ref, v_ref, o_ref):
    scale = CA_HEAD_DIM ** -0.5
    for h in range(CA_HEADS):
        sl = slice(h * CA_HEAD_DIM, (h + 1) * CA_HEAD_DIM)
        q = q_ref[:, sl].astype(BF16)
        k = k_ref[:, sl].astype(BF16)
        s = lax.dot_general(q, k, (((1,), (1,)), ((), ())), preferred_element_type=F32) * scale
        m = jnp.max(s, axis=-1, keepdims=True)
        p = jnp.exp(s - m)
        den = jnp.sum(p, axis=-1, keepdims=True)
        o = jnp.dot(p.astype(BF16), v_ref[:, sl].astype(BF16), preferred_element_type=F32)
        o_ref[:, sl] = (o / den).astype(o_ref.dtype)


def _cross_attention(q, kv, *, tq=512):
    b, s, d = q.shape
    mlen = kv.shape[1]
    tq = min(tq, s)
    return pl.pallas_call(
        _cross_attn_kernel, grid=(b, s // tq),
        in_specs=[pl.BlockSpec((None, tq, d), lambda i, j: (i, j, 0)),
                  pl.BlockSpec((None, mlen, d), lambda i, j: (i, 0, 0)),
                  pl.BlockSpec((None, mlen, d), lambda i, j: (i, 0, 1))],
        out_specs=pl.BlockSpec((None, tq, d), lambda i, j: (i, j, 0)),
        out_shape=jax.ShapeDtypeStruct((b, s, d), BF16),
        compiler_params=_cparams("parallel", "parallel"), name="cross_attn",
    )(q, kv, kv)


def _router_kernel(x_ref, w_ref, b_ref, o_ref):
    o_ref[...] = jnp.dot(x_ref[...], w_ref[...], precision=lax.Precision.HIGHEST,
                         preferred_element_type=F32) + b_ref[...]


def _router_logits(x, w, b, *, tm=512):
    m, d = x.shape
    e = w.shape[1]
    return pl.pallas_call(
        _router_kernel, grid=(m // tm,),
        in_specs=[pl.BlockSpec((tm, d), lambda i: (i, 0)),
                  pl.BlockSpec((d, e), lambda i: (0, 0)),
                  pl.BlockSpec((1, e), lambda i: (0, 0))],
        out_specs=pl.BlockSpec((tm, e), lambda i: (i, 0)),
        out_shape=jax.ShapeDtypeStruct((m, e), F32),
        compiler_params=_cparams("parallel"), name="router",
    )(x, w, b.reshape(1, e))


def _moe_kernel(be_ref, nused_ref, x_ref, g_ref, w1g_ref, w1l_ref, b1g_ref, b1l_ref, w2_ref,
                b2_ref, o_ref):
    i = pl.program_id(0)

    @pl.when(i < nused_ref[0])
    def _():
        x = x_ref[...]
        glu = jnp.dot(x, w1g_ref[...], preferred_element_type=F32) + b1g_ref[...]
        lin = jnp.dot(x, w1l_ref[...], preferred_element_type=F32) + b1l_ref[...]
        glu = jnp.minimum(glu, SWIGLU_LIMIT)
        lin = jnp.clip(lin, -SWIGLU_LIMIT, SWIGLU_LIMIT)
        act = glu * jax.nn.sigmoid(SWIGLU_ALPHA * glu) * (lin + 1.0)
        y = jnp.dot(act.astype(BF16), w2_ref[...], preferred_element_type=F32) + b2_ref[...]
        o_ref[...] = y * g_ref[...]

    @pl.when(i >= nused_ref[0])
    def _():
        o_ref[...] = jnp.zeros_like(o_ref)


def _moe_experts(blk_exp, n_used, xs, row_gate, w1g, w1l, b1g, b1l, w2, b2):
    rows, d = xs.shape
    nblk = rows // MOE_BLK
    f = w1g.shape[-1]
    wspec = lambda shape: pl.BlockSpec(shape, lambda i, be, nu: (be[i], 0, 0))
    grid_spec = pltpu.PrefetchScalarGridSpec(
        num_scalar_prefetch=2, grid=(nblk,),
        in_specs=[pl.BlockSpec((MOE_BLK, d), lambda i, be, nu: (i, 0)),
                  pl.BlockSpec((MOE_BLK, 1), lambda i, be, nu: (i, 0)),
                  wspec((None, d, f)), wspec((None, d, f)),
                  wspec((None, 1, f)), wspec((None, 1, f)),
                  wspec((None, f, d)), wspec((None, 1, d))],
        out_specs=pl.BlockSpec((MOE_BLK, d), lambda i, be, nu: (i, 0)),
    )
    return pl.pallas_call(
        _moe_kernel, grid_spec=grid_spec,
        out_shape=jax.ShapeDtypeStruct((rows, d), F32),
        compiler_params=pltpu.CompilerParams(dimension_semantics=("arbitrary",),
                                             vmem_limit_bytes=V7X_VMEM_LIMIT_MOE_BYTES),
        name="moe_experts",
    )(blk_exp, n_used, xs, row_gate.reshape(rows, 1), w1g, w1l, b1g, b1l, w2, b2)


def _per_expert(table, idx):
    hit = idx[:, None] == jnp.arange(N_EXPERTS, dtype=idx.dtype)[None, :]
    return jnp.sum(jnp.where(hit, table[None, :], 0), axis=1)


def _moe_ffn(x_bf, logits, moe_weights, b1, b2):
    n, d = x_bf.shape
    top_val, top_idx = lax.top_k(logits, TOP_K)
    top_w = jax.nn.softmax(top_val, axis=-1)
    flat_e = top_idx.reshape(-1).astype(jnp.int32)
    order = jnp.argsort(flat_e).astype(jnp.int32)
    rank = jnp.argsort(order).astype(jnp.int32)
    counts = jnp.sum(flat_e[:, None] == jnp.arange(N_EXPERTS, dtype=jnp.int32)[None, :],
                     axis=0, dtype=jnp.int32)
    starts = jnp.cumsum(counts) - counts
    padded = (counts + MOE_BLK - 1) // MOE_BLK * MOE_BLK
    pends = jnp.cumsum(padded)
    pstarts = pends - padded
    rows = n * TOP_K + N_EXPERTS * MOE_BLK
    nblk = rows // MOE_BLK
    blk_start = jnp.arange(nblk, dtype=jnp.int32) * MOE_BLK
    blk_exp = jnp.minimum(jnp.sum(pends[None, :] <= blk_start[:, None], axis=1, dtype=jnp.int32),
                          N_EXPERTS - 1)
    n_used = (pends[-1] // MOE_BLK).astype(jnp.int32).reshape(1)
    row_exp = jnp.repeat(blk_exp, MOE_BLK)
    off = jnp.arange(rows, dtype=jnp.int32) - _per_expert(pstarts, row_exp)
    valid = off < _per_expert(counts, row_exp)
    assign = order[jnp.where(valid, _per_expert(starts, row_exp) + off, 0)]
    row_tok = jnp.where(valid, assign // TOP_K, 0)
    row_gate = jnp.where(valid, top_w.reshape(-1)[assign], 0.0)
    xs = x_bf[row_tok]
    b1g = b1[:, None, 0::2]
    b1l = b1[:, None, 1::2]
    w1g, w1l, w2 = moe_weights
    ys = _moe_experts(blk_exp, n_used, xs, row_gate, w1g, w1l, b1g, b1l, w2,
                      b2[:, None, :])
    pos = (_per_expert(pstarts, flat_e) + rank - _per_expert(starts, flat_e)).reshape(n, TOP_K)
    return [ys[pos[:, kk]] for kk in range(TOP_K)]


def _heads_minor(p):
    lead = p.shape[:-1]
    return p.reshape(lead + (RWKV_HEADS, RWKV_HEAD)).swapaxes(-1, -2).reshape(p.shape)


def _to_scan_layout(t, bsz, s):
    return t.reshape(bsz, s, RWKV_HEAD, RWKV_HEADS).transpose(1, 2, 0, 3).reshape(
        s, RWKV_HEAD, bsz * RWKV_HEADS)


def _from_scan_layout(t, bsz, s):
    return t.reshape(s, RWKV_HEAD, bsz, RWKV_HEADS).transpose(2, 0, 1, 3).reshape(bsz, s, D_MODEL)


def _rwkv7_branch_proj(p_rkv, p_lora, shift_mix, w0, w_up, a0, a_up, g_up, k_k, k_a, r_k, lnx_g,
                       lnx_b, proj_a, moe_w1, moe_w2):
    bsz, s, c_rkv = p_rkv.shape
    hm = _heads_minor
    mix = jnp.concatenate([hm(shift_mix[:c_rkv].reshape(3, D_MODEL)).reshape(-1),
                           shift_mix[c_rkv:]])
    r, decay, k, v, alpha, beta, gate, bonus = _rwkv_prep(
        p_rkv, p_lora, mix, hm(w0), hm(w_up), hm(a0), hm(a_up), hm(g_up), hm(k_k), hm(k_a),
        hm(r_k.reshape(-1)))
    y, *moe_weights = _rwkv_scan_and_moe_weight_prep(r, decay, k, alpha, beta,
                                                     _to_scan_layout(v, bsz, s), moe_w1, moe_w2)
    y = _from_scan_layout(y, bsz, s)
    flat = lambda t: t.reshape(bsz * s, D_MODEL)
    proj_a_hm = proj_a.reshape(RWKV_HEADS, RWKV_HEAD, -1).swapaxes(0, 1).reshape(proj_a.shape)
    pa = _rwkv_post_proj(flat(y), flat(bonus), flat(gate), hm(lnx_g), hm(lnx_b), proj_a_hm)
    return pa, moe_weights


def _hybrid_mixer(x, w_in, shift_mix, w0, w_up, a0, a_up, g_up, k_k, k_a, r_k, lnx_g, lnx_b,
                  proj_a, proj_b, moe_w1, moe_w2):
    bsz, s, d = x.shape
    n_tok = bsz * s
    x_bf = x.astype(BF16)
    x2d = x_bf.reshape(n_tok, d)
    c0, c1, c2 = 3 * D_MODEL, RWKV_COLS, RWKV_COLS + ATTN_COLS
    w_rkv = _heads_minor(w_in[:, :c0].reshape(d, 3, D_MODEL)).reshape(d, c0)
    p_rkv = _matmul(x2d, w_rkv, tm=1024, tn=1024, name="in_rkv").reshape(bsz, s, -1)
    p_lora = _matmul(x2d, w_in[:, c0:c1], tm=1024, tn=512, name="in_lora").reshape(bsz, s, -1)
    p_gate = _matmul(x2d, w_in[:, c2:], tm=1024, tn=1024, name="in_gate")

    pa, moe_weights = _rwkv7_branch_proj(p_rkv, p_lora, shift_mix, w0, w_up, a0, a_up, g_up, k_k,
                                         k_a, r_k, lnx_g, lnx_b, proj_a, moe_w1, moe_w2)

    cos_t, sin_t = _rotary_tables(s)
    views = _attn_in_proj(x_bf, w_in[:, c1:c2], cos_t, sin_t)
    outs, lses = [], []
    for gi, (window, dilation) in enumerate(DIL_GROUPS):
        o, l = _dilated_attention(views[gi], gi, window, dilation)
        outs.append(o)
        lses.append(l)
    return _mix_merge(outs, lses, pa, p_gate, proj_b, bsz, s), moe_weights


def _memory_cross_attention(x_bf, mem, wq, wkv, bsz, s):
    d = x_bf.shape[-1]
    q = _matmul(x_bf, wq, tm=1024, tn=1024, name="ca_q").reshape(bsz, s, d)
    kv = _matmul(mem.reshape(-1, d), wkv, tm=512, tn=1024, name="ca_kv").reshape(bsz, -1, 2 * d)
    return _cross_attention(q, kv).reshape(-1, d)


def kernel(x, mem, w_in, shift_mix, w0, w_up, a0, a_up, g_up, k_k, k_a, r_k, lnx_g, lnx_b, proj_a, proj_b, w_out, ln1_g, ln1_b, ca_wq, ca_wkv, ca_wo, ln2_g, ln2_b, router_w, router_b, moe_w1, moe_b1, moe_w2, moe_b2, ln3_g, ln3_b):
    bsz, s, d = x.shape
    n_tok = bsz * s
    for l in range(DEPTH):
        merged, moe_weights = _hybrid_mixer(x, w_in[l], shift_mix[l], w0[l], w_up[l], a0[l],
                                            a_up[l], g_up[l], k_k[l], k_a[l], r_k[l], lnx_g[l],
                                            lnx_b[l], proj_a[l], proj_b[l], moe_w1[l], moe_w2[l])
        x1, x1_bf = _matmul_res_ln(merged, w_out[l], x.reshape(n_tok, d), ln1_g[l], ln1_b[l],
                                   name="w_out_ln1")
        o = _memory_cross_attention(x1_bf, mem, ca_wq[l], ca_wkv[l], bsz, s)
        x2, x2_bf = _matmul_res_ln(o, ca_wo[l], x1, ln2_g[l], ln2_b[l], name="ca_o_ln2")
        logits = _router_logits(x2, router_w[l], router_b[l])
        hs = _moe_ffn(x2_bf, logits, moe_weights, moe_b1[l], moe_b2[l])
        x = _res_layer_norm(x2, hs, ln3_g[l], ln3_b[l], name="ln3").reshape(bsz, s, d)
    return x
```

```python
import functools

import jax
import jax.numpy as jnp
import numpy as np
from jax import lax
from jax.experimental import pallas as pl
from jax.experimental.pallas import tpu as pltpu

F32 = jnp.float32
BF16 = jnp.bfloat16

D_MODEL = 2048
RWKV_HEAD = 64
RWKV_HEADS = D_MODEL // RWKV_HEAD
DECAY_LORA = 96
AAA_LORA = 96
GATE_LORA = 256
GN_EPS = 64e-5
DIL_GROUPS = ((128, 1), (512, 4), (2048, 16))
N_GROUPS = len(DIL_GROUPS)
DIL_HEADS = 8
DIL_HEAD_DIM = 64
DIL_GROUP_W = DIL_HEADS * DIL_HEAD_DIM
DIL_DIM = N_GROUPS * DIL_GROUP_W
BLK = 128
ROPE_THETA = 10000.0
NEG_INF = -1e30
RWKV_COLS = 3 * D_MODEL + DECAY_LORA + AAA_LORA + GATE_LORA
ATTN_COLS = 3 * DIL_DIM
CA_HEADS = 4
CA_HEAD_DIM = D_MODEL // CA_HEADS
N_EXPERTS = 32
TOP_K = 4
D_FF = D_MODEL
SWIGLU_LIMIT = 7.0
SWIGLU_ALPHA = 1.702
MOE_BLK = 128
LN_EPS = 1e-5
DEPTH = 1
DEEPNORM_ALPHA = (2 * DEPTH) ** 0.25

V7X_VMEM_LIMIT_BYTES = 56 * 1024 * 1024
V7X_VMEM_LIMIT_MOE_BYTES = 61 * 1024 * 1024


def _cparams(*sem):
    return pltpu.CompilerParams(dimension_semantics=sem, vmem_limit_bytes=V7X_VMEM_LIMIT_BYTES)


def _mm_kernel(a_ref, b_ref, o_ref):
    o_ref[...] = jnp.dot(a_ref[...].astype(BF16), b_ref[...].astype(BF16),
                         preferred_element_type=F32).astype(o_ref.dtype)


def _matmul(a, b, *, tm, tn, name, out_dtype=F32):
    m, k = a.shape
    _, n = b.shape
    tm, tn = min(tm, m), min(tn, n)
    assert m % tm == 0 and n % tn == 0, (m, n, tm, tn)
    return pl.pallas_call(
        _mm_kernel,
        grid=(n // tn, m // tm),
        in_specs=[pl.BlockSpec((tm, k), lambda j, i: (i, 0)),
                  pl.BlockSpec((k, tn), lambda j, i: (0, j))],
        out_specs=pl.BlockSpec((tm, tn), lambda j, i: (i, j)),
        out_shape=jax.ShapeDtypeStruct((m, n), out_dtype),
        compiler_params=_cparams("parallel", "parallel"),
        name=name,
    )(a, b)


def _ln_kernel(x_ref, *rest):
    *h_refs, g_ref, b_ref, o_ref = rest
    t = DEEPNORM_ALPHA * x_ref[...]
    for h_ref in h_refs:
        t = t + h_ref[...]
    mu = jnp.mean(t, axis=-1, keepdims=True)
    c = t - mu
    var = jnp.mean(c * c, axis=-1, keepdims=True)
    o_ref[...] = c * lax.rsqrt(var + LN_EPS) * g_ref[...] + b_ref[...]


def _res_layer_norm(x, hs, g, b, *, name, tm=256):
    m, d = x.shape
    assert m % tm == 0
    row = pl.BlockSpec((tm, d), lambda i: (i, 0))
    vec = pl.BlockSpec((1, d), lambda i: (0, 0))
    return pl.pallas_call(
        _ln_kernel, grid=(m // tm,), in_specs=[row] * (1 + len(hs)) + [vec, vec], out_specs=row,
        out_shape=jax.ShapeDtypeStruct((m, d), F32),
        compiler_params=_cparams("parallel"), name=name,
    )(x, *hs, g.reshape(1, d), b.reshape(1, d))


def _mm_res_ln_kernel(a_ref, w_ref, x_ref, g_ref, b_ref, o_ref, ob_ref):
    h = jnp.dot(a_ref[...].astype(BF16), w_ref[...], preferred_element_type=F32)
    t = DEEPNORM_ALPHA * x_ref[...] + h
    mu = jnp.mean(t, axis=-1, keepdims=True)
    c = t - mu
    var = jnp.mean(c * c, axis=-1, keepdims=True)
    y = c * lax.rsqrt(var + LN_EPS) * g_ref[...] + b_ref[...]
    o_ref[...] = y
    ob_ref[...] = y.astype(BF16)


def _matmul_res_ln(a, w, x, g, b, *, name, tm=512):
    m, k = a.shape
    d = w.shape[1]
    assert m % tm == 0
    row = lambda c: pl.BlockSpec((tm, c), lambda i: (i, 0))
    const = lambda r, c: pl.BlockSpec((r, c), lambda i: (0, 0))
    in_specs = [row(k), const(k, d), row(d), const(1, d), const(1, d)]
    args = [a, w.astype(BF16), x, g.reshape(1, d), b.reshape(1, d)]
    out_specs = [row(d), row(d)]
    out_shape = [jax.ShapeDtypeStruct((m, d), F32), jax.ShapeDtypeStruct((m, d), BF16)]
    return pl.pallas_call(
        _mm_res_ln_kernel,
        grid=(m // tm,), in_specs=in_specs, out_specs=out_specs, out_shape=out_shape,
        compiler_params=_cparams("parallel"), name=name,
    )(*args)


LANES = 128
ATTN_ROWS = 256
GROUP_QKV_W = 3 * DIL_GROUP_W
LANE_CHUNKS = DIL_GROUP_W // LANES


def _rotary_tables(s):
    half = DIL_HEAD_DIM // 2
    inv = ROPE_THETA ** (-jnp.arange(half, dtype=F32) * 2.0 / DIL_HEAD_DIM)
    ang = jnp.arange(s, dtype=F32)[:, None] * inv[None, :]
    cos, sin = jnp.cos(ang), jnp.sin(ang)
    return (jnp.concatenate([cos, cos, cos, cos], axis=-1),
            jnp.concatenate([-sin, sin, -sin, sin], axis=-1))


def _attn_in_kernel(x_ref, w_ref, cos_ref, sin_ref, o0_ref, o1_ref, o2_ref, scr_ref):
    tm = x_ref.shape[0]
    acc = jnp.dot(x_ref[...].astype(BF16), w_ref[...], preferred_element_type=F32)
    cos, sin = cos_ref[...], sin_ref[...]
    lane = lax.broadcasted_iota(jnp.int32, (tm, LANES), 1)
    first_half = (lane % DIL_HEAD_DIM) < (DIL_HEAD_DIM // 2)
    outs = (o0_ref, o1_ref, o2_ref)
    slab = 0
    for part in range(3):
        for gi, (_, d) in enumerate(DIL_GROUPS):
            for c in range(LANE_CHUNKS):
                col = part * DIL_DIM + gi * DIL_GROUP_W + c * LANES
                x = acc[:, col:col + LANES]
                if part < 2:
                    partner = jnp.where(first_half, pltpu.roll(x, LANES - 32, 1),
                                        pltpu.roll(x, 32, 1))
                    x = x * cos + partner * sin
                dst = part * DIL_GROUP_W + c * LANES
                if d == 1:
                    outs[gi][:, dst:dst + LANES] = x
                else:
                    scr_ref[slab] = x
                    for r in range(d):
                        outs[gi][:, r * GROUP_QKV_W + dst:r * GROUP_QKV_W + dst + LANES] = (
                            scr_ref[slab, pl.ds(r, tm // d, stride=d), :])
                    slab += 1


def _attn_in_proj(x, w_attn, cos_t, sin_t):
    b, s, dm = x.shape
    tm = min(ATTN_ROWS, s)
    dmax = max(d for _, d in DIL_GROUPS)
    assert s % tm == 0 and tm % (dmax * 8) == 0
    n_strided = sum(3 * LANE_CHUNKS for _, d in DIL_GROUPS if d > 1)
    tab = pl.BlockSpec((tm, LANES), lambda i, j: (j, 0))
    ospec = lambda d: pl.BlockSpec((None, tm // d, d * GROUP_QKV_W), lambda i, j: (i, j, 0))
    return pl.pallas_call(
        _attn_in_kernel, grid=(b, s // tm),
        in_specs=[pl.BlockSpec((None, tm, dm), lambda i, j: (i, j, 0)),
                  pl.BlockSpec(w_attn.shape, lambda i, j: (0, 0), pipeline_mode=pl.Buffered(1)),
                  tab, tab],
        out_specs=[ospec(d) for _, d in DIL_GROUPS],
        out_shape=[jax.ShapeDtypeStruct((b, s // d, d * GROUP_QKV_W), F32) for _, d in DIL_GROUPS],
        scratch_shapes=[pltpu.VMEM((n_strided, tm, LANES), F32)],
        compiler_params=_cparams("parallel", "parallel"), name="in_attn_rope",
    )(x, w_attn.astype(BF16), cos_t, sin_t)


def _dil_attn_kernel(q_ref, kp_ref, kc_ref, vp_ref, vc_ref, o_ref, l_ref, *, span):
    nb = pl.program_id(2)
    qi = lax.broadcasted_iota(jnp.int32, (BLK, 2 * BLK), 0) + BLK
    ki = lax.broadcasted_iota(jnp.int32, (BLK, 2 * BLK), 1)
    dist = qi - ki
    mask = (dist >= 0) & (dist <= span) & ((nb > 0) | (ki >= BLK))
    scale = DIL_HEAD_DIM ** -0.5
    q = q_ref[...].astype(BF16)
    k = jnp.concatenate([kp_ref[...], kc_ref[...]], axis=0).astype(BF16)
    v = jnp.concatenate([vp_ref[...], vc_ref[...]], axis=0).astype(BF16)
    for h in range(DIL_HEADS):
        sl = slice(h * DIL_HEAD_DIM, (h + 1) * DIL_HEAD_DIM)
        s = lax.dot_general(q[:, sl], k[:, sl], (((1,), (1,)), ((), ())),
                            preferred_element_type=F32) * scale
        s = jnp.where(mask, s, NEG_INF)
        m = jnp.max(s, axis=-1, keepdims=True)
        p = jnp.exp(s - m)
        den = jnp.sum(p, axis=-1, keepdims=True)
        o = jnp.dot(p.astype(BF16), v[:, sl], preferred_element_type=F32)
        o_ref[:, sl] = o / den
        l_ref[:, sl] = jnp.broadcast_to(m + jnp.log(den), (BLK, DIL_HEAD_DIM))


def _dilated_attention(qkv_view, gi, window, dilation):
    b, n, _ = qkv_view.shape
    d = dilation
    assert n % BLK == 0
    nblk = n // BLK
    blk = (None, BLK, DIL_GROUP_W)

    def col(which):
        return lambda bi, r, nb: (bi, nb, r * 3 + which)

    def col_prev(which):
        return lambda bi, r, nb: (bi, jnp.maximum(nb - 1, 0), r * 3 + which)

    out_spec = pl.BlockSpec(blk, lambda bi, r, nb: (bi, nb, r))
    return pl.pallas_call(
        functools.partial(_dil_attn_kernel, span=window // dilation),
        grid=(b, d, nblk),
        in_specs=[pl.BlockSpec(blk, col(0)),
                  pl.BlockSpec(blk, col_prev(1)), pl.BlockSpec(blk, col(1)),
                  pl.BlockSpec(blk, col_prev(2)), pl.BlockSpec(blk, col(2))],
        out_specs=[out_spec, out_spec],
        out_shape=[jax.ShapeDtypeStruct((b, n, d * DIL_GROUP_W), F32)] * 2,
        compiler_params=_cparams("parallel", "parallel", "arbitrary"),
        name=f"dil_attn_g{gi}",
    )(qkv_view, qkv_view, qkv_view, qkv_view, qkv_view)


def _mix_merge_kernel(o0_ref, l0_ref, o1_ref, l1_ref, o2_ref, l2_ref, pa_ref, pg_ref, w_ref,
                      out_ref, scr_ref):
    tm, dm = pa_ref.shape
    slab = [0]

    def natural(ref, d):
        if d == 1:
            return [ref[:, c * LANES:(c + 1) * LANES] for c in range(LANE_CHUNKS)]
        chunks = []
        for c in range(LANE_CHUNKS):
            for r in range(d):
                scr_ref[slab[0], pl.ds(r, tm // d, stride=d), :] = (
                    ref[:, r * DIL_GROUP_W + c * LANES:r * DIL_GROUP_W + (c + 1) * LANES])
            chunks.append(scr_ref[slab[0]])
            slab[0] += 1
        return chunks

    dils = [d for _, d in DIL_GROUPS]
    o = [natural(r, d) for r, d in zip((o0_ref, o1_ref, o2_ref), dils)]
    l = [natural(r, d) for r, d in zip((l0_ref, l1_ref, l2_ref), dils)]
    yb = []
    for c in range(LANE_CHUNKS):
        m = jnp.maximum(jnp.maximum(l[0][c], l[1][c]), l[2][c])
        e = [jnp.exp(l[g][c] - m) for g in range(N_GROUPS)]
        den = e[0] + e[1] + e[2]
        yb.append((e[0] / den) * o[0][c] + (e[1] / den) * o[1][c] + (e[2] / den) * o[2][c])
    pb = jnp.dot(jnp.concatenate(yb, axis=-1).astype(BF16), w_ref[...], preferred_element_type=F32)
    out_ref[...] = (jax.nn.sigmoid(pg_ref[:, :dm]) * pa_ref[...]
                    + jax.nn.sigmoid(pg_ref[:, dm:]) * pb).astype(out_ref.dtype)


def _mix_merge(outs, lses, pa, p_gate, proj_b, bsz, s):
    n_tok, dm = pa.shape
    tm = min(ATTN_ROWS, s)
    per_b = s // tm
    n_strided = sum(2 * LANE_CHUNKS for _, d in DIL_GROUPS if d > 1)
    vspec = lambda d: pl.BlockSpec((None, tm // d, d * DIL_GROUP_W),
                                   lambda i: (i // per_b, i % per_b, 0))
    views = []
    for (_, d), o, l in zip(DIL_GROUPS, outs, lses):
        views += [(o, vspec(d)), (l, vspec(d))]
    row = lambda w: pl.BlockSpec((tm, w), lambda i: (i, 0))
    return pl.pallas_call(
        _mix_merge_kernel, grid=(n_tok // tm,),
        in_specs=[sp for _, sp in views] + [row(dm), row(2 * dm),
                                            pl.BlockSpec(proj_b.shape, lambda i: (0, 0))],
        out_specs=row(dm), out_shape=jax.ShapeDtypeStruct((n_tok, dm), BF16),
        scratch_shapes=[pltpu.VMEM((n_strided, tm, LANES), F32)],
        compiler_params=_cparams("parallel"), name="attn_mix_merge",
    )(*[a for a, _ in views], pa, p_gate, proj_b.astype(BF16))


W1_SPLIT_SUB = 256
SCAN_STEPS = 32
SCAN_SLAB = 16


def _split_even_odd(w_ref, g_ref, l_ref):
    sub, half = W1_SPLIT_SUB, W1_SPLIT_SUB // 2
    src = lax.broadcasted_iota(jnp.int32, (sub, sub), 0)
    dst = lax.broadcasted_iota(jnp.int32, (sub, sub), 1)
    want = jnp.where(dst < half, 2 * dst, 2 * (dst - half) + 1)
    sel = (src == want).astype(BF16)
    for q in range(w_ref.shape[1] // sub):
        w = w_ref[:, q * sub:(q + 1) * sub].astype(BF16)
        r = jnp.dot(w, sel, preferred_element_type=F32).astype(BF16)
        g_ref[:, q * half:(q + 1) * half] = r[:, :half]
        l_ref[:, q * half:(q + 1) * half] = r[:, half:]


def _rwkv_scan_kernel(r_ref, w_ref, k_ref, a_ref, b_ref, v_ref, w1_ref, w2_ref, y_ref, w1g_ref,
                      w1l_ref, w2b_ref, s_ref, *, tc):
    n = RWKV_HEAD

    @pl.when(pl.program_id(0) == 0)
    def _():
        s_ref[...] = jnp.zeros_like(s_ref)

    _split_even_odd(w1_ref, w1g_ref, w1l_ref)
    w2b_ref[...] = w2_ref[...].astype(BF16)

    slabs = n // SCAN_SLAB
    zero = jnp.zeros((SCAN_SLAB, s_ref.shape[2]), F32)

    def row(ref, t, j):
        return ref[t // SUBLANES, j, pl.ds(t % SUBLANES, 1), :]

    def rows_of(slab):
        return pl.ds(pl.multiple_of(slab * SCAN_SLAB, SCAN_SLAB), SCAN_SLAB)

    def state_times_a(t, slab):
        rows = rows_of(slab)
        acc = [zero, zero]
        for j in range(n):
            acc[j % 2] = acc[j % 2] + s_ref[j, rows, :] * row(a_ref, t, j)
        return acc[0] + acc[1]

    def update_and_read(t, slab, sa):
        rows = rows_of(slab)
        vt = v_ref[t, rows, :]
        yac = [zero, zero]
        for j in range(n):
            sj = (s_ref[j, rows, :] * row(w_ref, t, j) + sa * row(b_ref, t, j)
                  + vt * row(k_ref, t, j))
            s_ref[j, rows, :] = sj
            yac[j % 2] = yac[j % 2] + sj * row(r_ref, t, j)
        y_ref[t, rows, :] = yac[0] + yac[1]

    def trip(q, sa):
        t, slab = q // slabs, q % slabs
        nxt = jnp.minimum(q + 1, tc * slabs - 1)
        update_and_read(t, slab, sa)
        return state_times_a(nxt // slabs, nxt % slabs)

    lax.fori_loop(0, tc * slabs, trip, state_times_a(0, 0))


def _rwkv_scan_and_moe_weight_prep(r, w, k, a, b, v, w1, w2):
    s, n, l = v.shape
    tc = min(SCAN_STEPS, s)
    assert s % tc == 0
    steps = s // tc
    assert w1.shape[-1] % W1_SPLIT_SUB == 0

    def sliced(wgt, cols_out):
        e, rows, cols = wgt.shape
        assert (e * rows) % steps == 0
        per_step = e * rows // steps
        assert rows % per_step == 0 and per_step % (2 * SUBLANES) == 0
        per_e = rows // per_step
        spec = lambda c: pl.BlockSpec((None, per_step, c), lambda i: (i // per_e, i % per_e, 0))
        return spec(cols), spec(cols_out), jax.ShapeDtypeStruct((e, rows, cols_out), BF16)

    w1_in, w1_out, w1_shape = sliced(w1, w1.shape[-1] // 2)
    w2_in, w2_out, w2_shape = sliced(w2, w2.shape[-1])
    rows = pl.BlockSpec((tc // SUBLANES, n, SUBLANES, l), lambda i: (i, 0, 0, 0))
    blk = pl.BlockSpec((tc, n, l), lambda i: (i, 0, 0))
    return pl.pallas_call(
        functools.partial(_rwkv_scan_kernel, tc=tc),
        grid=(steps,), in_specs=[rows] * 5 + [blk, w1_in, w2_in],
        out_specs=[blk, w1_out, w1_out, w2_out],
        out_shape=[jax.ShapeDtypeStruct((s, n, l), F32), w1_shape, w1_shape, w2_shape],
        scratch_shapes=[pltpu.VMEM((n, n, l), F32)],
        compiler_params=_cparams("arbitrary"), name="rwkv_scan",
    )(r, w, k, a, b, v, w1, w2)


RWKV_PREP_ROWS = 32
SUBLANES = 8


def _head_sum(x):
    cols = x.shape[-1] // LANES
    acc = x[:, :LANES]
    for c in range(1, cols):
        acc = acc + x[:, c * LANES:(c + 1) * LANES]
    acc = acc + pltpu.roll(acc, RWKV_HEADS, 1)
    acc = acc + pltpu.roll(acc, 2 * RWKV_HEADS, 1)
    return jnp.concatenate([acc] * cols, axis=-1)


def _token_shift_rows(p, last_prev_row, mix, first):
    rolled = pltpu.roll(p, 1, 0)
    row0 = jnp.where(first, jnp.zeros_like(last_prev_row), last_prev_row)
    t = lax.broadcasted_iota(jnp.int32, p.shape, 0)
    prev = jnp.where(t == 0, row0, rolled)
    return p + (prev - p) * mix


def _rwkv_prep_kernel(rkv_ref, rkvp_ref, lo_ref, lop_ref, mixr_ref, mixl_ref, w0_ref, a0_ref,
                      kk_ref, ka_ref, rk_ref, wup_ref, aup_ref, gup_ref,
                      r_ref, w_ref, k_ref, al_ref, be_ref, v_ref, g_ref, bo_ref, tiles_ref):
    first = pl.program_id(0) == 0
    d = D_MODEL
    nb, ts = rkv_ref.shape[0], rkv_ref.shape[1]
    mm = lambda a, w_ref_: jnp.dot(a.astype(BF16), w_ref_[...].astype(BF16),
                                   preferred_element_type=F32)
    for bi in range(nb):
        def seg(c):
            sl = slice(c * d, (c + 1) * d)
            return _token_shift_rows(rkv_ref[bi, :, sl], rkvp_ref[bi, SUBLANES - 1:SUBLANES, sl],
                                     mixr_ref[:, sl], first)

        r, k, v = seg(0), seg(1), seg(2)
        zl = _token_shift_rows(lo_ref[bi], lop_ref[bi, SUBLANES - 1:SUBLANES, :], mixl_ref[...],
                               first)
        wd = zl[:, :DECAY_LORA]
        ad = zl[:, DECAY_LORA:DECAY_LORA + AAA_LORA]
        gd = zl[:, DECAY_LORA + AAA_LORA:]
        z = -(w0_ref[...] + mm(jnp.tanh(wd), wup_ref))
        softplus = jnp.maximum(z, 0.0) + jnp.log1p(jnp.exp(-jnp.abs(z)))
        w = -softplus - 0.5
        a = jax.nn.sigmoid(a0_ref[...] + mm(ad, aup_ref))
        kk = k * kk_ref[...]
        kk = kk / jnp.maximum(jnp.sqrt(_head_sum(kk * kk)), 1e-12)
        k2 = k * (1.0 + (a - 1.0) * ka_ref[...])
        for op, val in enumerate((r, jnp.exp(-jnp.exp(w)), k2, -kk, kk * a)):
            for c in range(d // LANES):
                col = val[:, c * LANES:(c + 1) * LANES]
                tiles_ref[op, bi, :, c * LANES:(c + 1) * LANES] = (
                    col if bi == 0 else pltpu.roll(col, bi * RWKV_HEADS, 1))
        v_ref[bi] = v
        g_ref[bi] = mm(jax.nn.sigmoid(gd), gup_ref)
        bo_ref[bi] = _head_sum(r * k2 * rk_ref[...]) * v

    lane_group = lax.broadcasted_iota(jnp.int32, (ts, LANES), 1) // RWKV_HEADS
    for op, out_ref in enumerate((r_ref, w_ref, k_ref, al_ref, be_ref)):
        for c in range(d // LANES):
            cols = [tiles_ref[op, bi, :, c * LANES:(c + 1) * LANES] for bi in range(nb)]
            for jm in range(nb):
                sel = cols[(0 - jm) % nb]
                for g in range(1, nb):
                    sel = jnp.where(lane_group == g, cols[(g - jm) % nb], sel)
                out = sel if jm == 0 else pltpu.roll(sel, (nb - jm) * RWKV_HEADS, 1)
                for tb in range(ts // SUBLANES):
                    out_ref[tb, c * nb + jm] = out[tb * SUBLANES:(tb + 1) * SUBLANES]


def _rwkv_prep(p_rkv, p_lora, shift_mix, w0, w_up, a0, a_up, g_up, k_k, k_a, r_k):
    b, s, c_rkv = p_rkv.shape
    d, ts, c_lo = D_MODEL, min(RWKV_PREP_ROWS, s), p_lora.shape[-1]
    assert s % ts == 0 and ts % SUBLANES == 0
    assert b * RWKV_HEADS == LANES, "the scan packs exactly (batch, head) onto the 128 lanes"
    cur = lambda c: pl.BlockSpec((b, ts, c), lambda i: (0, i, 0))
    prev = lambda c: pl.BlockSpec(
        (b, SUBLANES, c), lambda i: (0, jnp.maximum(i * (ts // SUBLANES) - 1, 0), 0))
    vec = lambda c: pl.BlockSpec((1, c), lambda i: (0, 0))
    full = lambda a: pl.BlockSpec(a.shape, lambda i: (0, 0))
    rows = pl.BlockSpec((ts // SUBLANES, RWKV_HEAD, SUBLANES, LANES), lambda i: (i, 0, 0, 0))
    rows_shape = jax.ShapeDtypeStruct((s // SUBLANES, RWKV_HEAD, SUBLANES, LANES), F32)
    tok_shape = jax.ShapeDtypeStruct((b, s, d), F32)
    row = lambda t: t.reshape(1, -1)
    return pl.pallas_call(
        _rwkv_prep_kernel, grid=(s // ts,),
        in_specs=[cur(c_rkv), prev(c_rkv), cur(c_lo), prev(c_lo), vec(c_rkv), vec(c_lo),
                  vec(d), vec(d), vec(d), vec(d), vec(d), full(w_up), full(a_up), full(g_up)],
        out_specs=[rows] * 5 + [cur(d)] * 3, out_shape=[rows_shape] * 5 + [tok_shape] * 3,
        scratch_shapes=[pltpu.VMEM((5, b, ts, d), F32)],
        compiler_params=_cparams("arbitrary"), name="rwkv_prep",
    )(p_rkv, p_rkv, p_lora, p_lora, row(shift_mix[:c_rkv]), row(shift_mix[c_rkv:]), row(w0),
      row(a0), row(k_k), row(k_a), row(r_k), w_up, a_up, g_up)


def _proj_a_kernel(y_ref, bo_ref, g_ref, lg_ref, lb_ref, w_ref, o_ref):
    y = y_ref[...]
    inv_n = 1.0 / RWKV_HEAD
    c = y - _head_sum(y) * inv_n
    var = _head_sum(c * c) * inv_n
    ya = (c * lax.rsqrt(var + GN_EPS) * lg_ref[...] + lb_ref[...] + bo_ref[...]) * g_ref[...]
    o_ref[...] = jnp.dot(ya.astype(BF16), w_ref[...], preferred_element_type=F32)


def _rwkv_post_proj(y, bonus, gate, lnx_g, lnx_b, proj_a, *, tm=256):
    m, d = y.shape
    assert m % tm == 0
    rowb = pl.BlockSpec((tm, d), lambda i: (i, 0))
    vec = pl.BlockSpec((1, d), lambda i: (0, 0))
    return pl.pallas_call(
        _proj_a_kernel, grid=(m // tm,),
        in_specs=[rowb, rowb, rowb, vec, vec, pl.BlockSpec((d, d), lambda i: (0, 0))],
        out_specs=rowb, out_shape=jax.ShapeDtypeStruct((m, d), F32),
        compiler_params=_cparams("parallel"), name="rwkv_post_proj_a",
    )(y, bonus, gate, lnx_g.reshape(1, d), lnx_b.reshape(1, d), proj_a.astype(BF16))


def _cross_attn_kernel(q_ref, k_ref, v_ref, o_ref):
    scale = CA_HEAD_DIM ** -0.5
    for h in range(CA_HEADS):
        sl = slice(h * CA_HEAD_DIM, (h + 1) * CA_HEAD_DIM)
        q = q_ref[:, sl].astype(BF16)
        k = k_ref[:, sl].astype(BF16)
        s = lax.dot_general(q, k, (((1,), (1,)), ((), ())), preferred_element_type=F32) * scale
        m = jnp.max(s, axis=-1, keepdims=True)
        p = jnp.exp(s - m)
        den = jnp.sum(p, axis=-1, keepdims=True)
        o = jnp.dot(p.astype(BF16), v_ref[:, sl].astype(BF16), preferred_element_type=F32)
        o_ref[:, sl] = (o / den).astype(o_ref.dtype)


def _cross_attention(q, kv, *, tq=512):
    b, s, d = q.shape
    mlen = kv.shape[1]
    tq = min(tq, s)
    return pl.pallas_call(
        _cross_attn_kernel, grid=(b, s // tq),
        in_specs=[pl.BlockSpec((None, tq, d), lambda i, j: (i, j, 0)),
                  pl.BlockSpec((None, mlen, d), lambda i, j: (i, 0, 0)),
                  pl.BlockSpec((None, mlen, d), lambda i, j: (i, 0, 1))],
        out_specs=pl.BlockSpec((None, tq, d), lambda i, j: (i, j, 0)),
        out_shape=jax.ShapeDtypeStruct((b, s, d), BF16),
        compiler_params=_cparams("parallel", "parallel"), name="cross_attn",
    )(q, kv, kv)


def _router_kernel(x_ref, w_ref, b_ref, o_ref):
    o_ref[...] = jnp.dot(x_ref[...], w_ref[...], precision=lax.Precision.HIGHEST,
                         preferred_element_type=F32) + b_ref[...]


def _router_logits(x, w, b, *, tm=512):
    m, d = x.shape
    e = w.shape[1]
    return pl.pallas_call(
        _router_kernel, grid=(m // tm,),
        in_specs=[pl.BlockSpec((tm, d), lambda i: (i, 0)),
                  pl.BlockSpec((d, e), lambda i: (0, 0)),
                  pl.BlockSpec((1, e), lambda i: (0, 0))],
        out_specs=pl.BlockSpec((tm, e), lambda i: (i, 0)),
        out_shape=jax.ShapeDtypeStruct((m, e), F32),
        compiler_params=_cparams("parallel"), name="router",
    )(x, w, b.reshape(1, e))


def _moe_kernel(be_ref, nused_ref, x_ref, g_ref, w1g_ref, w1l_ref, b1g_ref, b1l_ref, w2_ref,
                b2_ref, o_ref):
    i = pl.program_id(0)

    @pl.when(i < nused_ref[0])
    def _():
        x = x_ref[...]
        glu = jnp.dot(x, w1g_ref[...], preferred_element_type=F32) + b1g_ref[...]
        lin = jnp.dot(x, w1l_ref[...], preferred_element_type=F32) + b1l_ref[...]
        glu = jnp.minimum(glu, SWIGLU_LIMIT)
        lin = jnp.clip(lin, -SWIGLU_LIMIT, SWIGLU_LIMIT)
        act = glu * jax.nn.sigmoid(SWIGLU_ALPHA * glu) * (lin + 1.0)
        y = jnp.dot(act.astype(BF16), w2_ref[...], preferred_element_type=F32) + b2_ref[...]
        o_ref[...] = y * g_ref[...]

    @pl.when(i >= nused_ref[0])
    def _():
        o_ref[...] = jnp.zeros_like(o_ref)


def _moe_experts(blk_exp, n_used, xs, row_gate, w1g, w1l, b1g, b1l, w2, b2):
    rows, d = xs.shape
    nblk = rows // MOE_BLK
    f = w1g.shape[-1]
    wspec = lambda shape: pl.BlockSpec(shape, lambda i, be, nu: (be[i], 0, 0))
    grid_spec = pltpu.PrefetchScalarGridSpec(
        num_scalar_prefetch=2, grid=(nblk,),
        in_specs=[pl.BlockSpec((MOE_BLK, d), lambda i, be, nu: (i, 0)),
                  pl.BlockSpec((MOE_BLK, 1), lambda i, be, nu: (i, 0)),
                  wspec((None, d, f)), wspec((None, d, f)),
                  wspec((None, 1, f)), wspec((None, 1, f)),
                  wspec((None, f, d)), wspec((None, 1, d))],
        out_specs=pl.BlockSpec((MOE_BLK, d), lambda i, be, nu: (i, 0)),
    )
    return pl.pallas_call(
        _moe_kernel, grid_spec=grid_spec,
        out_shape=jax.ShapeDtypeStruct((rows, d), F32),
        compiler_params=pltpu.CompilerParams(dimension_semantics=("arbitrary",),
                                             vmem_limit_bytes=V7X_VMEM_LIMIT_MOE_BYTES),
        name="moe_experts",
    )(blk_exp, n_used, xs, row_gate.reshape(rows, 1), w1g, w1l, b1g, b1l, w2, b2)


def _per_expert(table, idx):
    hit = idx[:, None] == jnp.arange(N_EXPERTS, dtype=idx.dtype)[None, :]
    return jnp.sum(jnp.where(hit, table[None, :], 0), axis=1)


def _moe_ffn(x_bf, logits, moe_weights, b1, b2):
    n, d = x_bf.shape
    top_val, top_idx = lax.top_k(logits, TOP_K)
    top_w = jax.nn.softmax(top_val, axis=-1)
    flat_e = top_idx.reshape(-1).astype(jnp.int32)
    order = jnp.argsort(flat_e).astype(jnp.int32)
    rank = jnp.argsort(order).astype(jnp.int32)
    counts = jnp.sum(flat_e[:, None] == jnp.arange(N_EXPERTS, dtype=jnp.int32)[None, :],
                     axis=0, dtype=jnp.int32)
    starts = jnp.cumsum(counts) - counts
    padded = (counts + MOE_BLK - 1) // MOE_BLK * MOE_BLK
    pends = jnp.cumsum(padded)
    pstarts = pends - padded
    rows = n * TOP_K + N_EXPERTS * MOE_BLK
    nblk = rows // MOE_BLK
    blk_start = jnp.arange(nblk, dtype=jnp.int32) * MOE_BLK
    blk_exp = jnp.minimum(jnp.sum(pends[None, :] <= blk_start[:, None], axis=1, dtype=jnp.int32),
                          N_EXPERTS - 1)
    n_used = (pends[-1] // MOE_BLK).astype(jnp.int32).reshape(1)
    row_exp = jnp.repeat(blk_exp, MOE_BLK)
    off = jnp.arange(rows, dtype=jnp.int32) - _per_expert(pstarts, row_exp)
    valid = off < _per_expert(counts, row_exp)
    assign = order[jnp.where(valid, _per_expert(starts, row_exp) + off, 0)]
    row_tok = jnp.where(valid, assign // TOP_K, 0)
    row_gate = jnp.where(valid, top_w.reshape(-1)[assign], 0.0)
    xs = x_bf[row_tok]
    b1g = b1[:, None, 0::2]
    b1l = b1[:, None, 1::2]
    w1g, w1l, w2 = moe_weights
    ys = _moe_experts(blk_exp, n_used, xs, row_gate, w1g, w1l, b1g, b1l, w2,
                      b2[:, None, :])
    pos = (_per_expert(pstarts, flat_e) + rank - _per_expert(starts, flat_e)).reshape(n, TOP_K)
    return [ys[pos[:, kk]] for kk in range(TOP_K)]


def _heads_minor(p):
    lead = p.shape[:-1]
    return p.reshape(lead + (RWKV_HEADS, RWKV_HEAD)).swapaxes(-1, -2).reshape(p.shape)


def _to_scan_layout(t, bsz, s):
    return t.reshape(bsz, s, RWKV_HEAD, RWKV_HEADS).transpose(1, 2, 0, 3).reshape(
        s, RWKV_HEAD, bsz * RWKV_HEADS)


def _from_scan_layout(t, bsz, s):
    return t.reshape(s, RWKV_HEAD, bsz, RWKV_HEADS).transpose(2, 0, 1, 3).reshape(bsz, s, D_MODEL)


def _rwkv7_branch_proj(p_rkv, p_lora, shift_mix, w0, w_up, a0, a_up, g_up, k_k, k_a, r_k, lnx_g,
                       lnx_b, proj_a, moe_w1, moe_w2):
    bsz, s, c_rkv = p_rkv.shape
    hm = _heads_minor
    mix = jnp.concatenate([hm(shift_mix[:c_rkv].reshape(3, D_MODEL)).reshape(-1),
                           shift_mix[c_rkv:]])
    r, decay, k, alpha, beta, v, gate, bonus = _rwkv_prep(
        p_rkv, p_lora, mix, hm(w0), hm(w_up), hm(a0), hm(a_up), hm(g_up), hm(k_k), hm(k_a),
        hm(r_k.reshape(-1)))
    y, *moe_weights = _rwkv_scan_and_moe_weight_prep(r, decay, k, alpha, beta,
                                                     _to_scan_layout(v, bsz, s), moe_w1, moe_w2)
    y = _from_scan_layout(y, bsz, s)
    flat = lambda t: t.reshape(bsz * s, D_MODEL)
    proj_a_hm = proj_a.reshape(RWKV_HEADS, RWKV_HEAD, -1).swapaxes(0, 1).reshape(proj_a.shape)
    pa = _rwkv_post_proj(flat(y), flat(bonus), flat(gate), hm(lnx_g), hm(lnx_b), proj_a_hm)
    return pa, moe_weights


def _hybrid_mixer(x, w_in, shift_mix, w0, w_up, a0, a_up, g_up, k_k, k_a, r_k, lnx_g, lnx_b,
                  proj_a, proj_b, moe_w1, moe_w2):
    bsz, s, d = x.shape
    n_tok = bsz * s
    x_bf = x.astype(BF16)
    x2d = x_bf.reshape(n_tok, d)
    c0, c1, c2 = 3 * D_MODEL, RWKV_COLS, RWKV_COLS + ATTN_COLS
    w_rkv = _heads_minor(w_in[:, :c0].reshape(d, 3, D_MODEL)).reshape(d, c0)
    p_rkv = _matmul(x2d, w_rkv, tm=1024, tn=1024, name="in_rkv").reshape(bsz, s, -1)
    p_lora = _matmul(x2d, w_in[:, c0:c1], tm=1024, tn=512, name="in_lora").reshape(bsz, s, -1)
    p_gate = _matmul(x2d, w_in[:, c2:], tm=1024, tn=1024, name="in_gate")

    pa, moe_weights = _rwkv7_branch_proj(p_rkv, p_lora, shift_mix, w0, w_up, a0, a_up, g_up, k_k,
                                         k_a, r_k, lnx_g, lnx_b, proj_a, moe_w1, moe_w2)

    cos_t, sin_t = _rotary_tables(s)
    views = _attn_in_proj(x_bf, w_in[:, c1:c2], cos_t, sin_t)
    outs, lses = [], []
    for gi, (window, dilation) in enumerate(DIL_GROUPS):
        o, l = _dilated_attention(views[gi], gi, window, dilation)
        outs.append(o)
        lses.append(l)
    return _mix_merge(outs, lses, pa, p_gate, proj_b, bsz, s), moe_weights


def _memory_cross_attention(x_bf, mem, wq, wkv, bsz, s):
    d = x_bf.shape[-1]
    q = _matmul(x_bf, wq, tm=1024, tn=1024, name="ca_q").reshape(bsz, s, d)
    kv = _matmul(mem.reshape(-1, d), wkv, tm=512, tn=1024, name="ca_kv").reshape(bsz, -1, 2 * d)
    return _cross_attention(q, kv).reshape(-1, d)


def kernel(x, mem, w_in, shift_mix, w0, w_up, a0, a_up, g_up, k_k, k_a, r_k, lnx_g, lnx_b, proj_a, proj_b, w_out, ln1_g, ln1_b, ca_wq, ca_wkv, ca_wo, ln2_g, ln2_b, router_w, router_b, moe_w1, moe_b1, moe_w2, moe_b2, ln3_g, ln3_b):
    bsz, s, d = x.shape
    n_tok = bsz * s
    for l in range(DEPTH):
        merged, moe_weights = _hybrid_mixer(x, w_in[l], shift_mix[l], w0[l], w_up[l], a0[l],
                                            a_up[l], g_up[l], k_k[l], k_a[l], r_k[l], lnx_g[l],
                                            lnx_b[l], proj_a[l], proj_b[l], moe_w1[l], moe_w2[l])
        x1, x1_bf = _matmul_res_ln(merged, w_out[l], x.reshape(n_tok, d), ln1_g[l], ln1_b[l],
                                   name="w_out_ln1")
        o = _memory_cross_attention(x1_bf, mem, ca_wq[l], ca_wkv[l], bsz, s)
        x2, x2_bf = _matmul_res_ln(o, ca_wo[l], x1, ln2_g[l], ln2_b[l], name="ca_o_ln2")
        logits = _router_logits(x2, router_w[l], router_b[l])
        hs = _moe_ffn(x2_bf, logits, moe_weights, moe_b1[l], moe_b2[l])
        x = _res_layer_norm(x2, hs, ln3_g[l], ln3_b[l], name="ln3").reshape(bsz, s, d)
    return x
```

```python
import functools

import jax
import jax.numpy as jnp
import numpy as np
from jax import lax
from jax.experimental import pallas as pl
from jax.experimental.pallas import tpu as pltpu

F32 = jnp.float32
BF16 = jnp.bfloat16

D_MODEL = 2048
RWKV_HEAD = 64
RWKV_HEADS = D_MODEL // RWKV_HEAD
DECAY_LORA = 96
AAA_LORA = 96
GATE_LORA = 256
GN_EPS = 64e-5
DIL_GROUPS = ((128, 1), (512, 4), (2048, 16))
N_GROUPS = len(DIL_GROUPS)
DIL_HEADS = 8
DIL_HEAD_DIM = 64
DIL_GROUP_W = DIL_HEADS * DIL_HEAD_DIM
DIL_DIM = N_GROUPS * DIL_GROUP_W
BLK = 128
ROPE_THETA = 10000.0
NEG_INF = -1e30
RWKV_COLS = 3 * D_MODEL + DECAY_LORA + AAA_LORA + GATE_LORA
ATTN_COLS = 3 * DIL_DIM
CA_HEADS = 4
CA_HEAD_DIM = D_MODEL // CA_HEADS
N_EXPERTS = 32
TOP_K = 4
D_FF = D_MODEL
SWIGLU_LIMIT = 7.0
SWIGLU_ALPHA = 1.702
MOE_BLK = 128
LN_EPS = 1e-5
DEPTH = 1
DEEPNORM_ALPHA = (2 * DEPTH) ** 0.25

V7X_VMEM_LIMIT_BYTES = 56 * 1024 * 1024
V7X_VMEM_LIMIT_MOE_BYTES = 61 * 1024 * 1024


def _cparams(*sem):
    return pltpu.CompilerParams(dimension_semantics=sem, vmem_limit_bytes=V7X_VMEM_LIMIT_BYTES)


def _mm_kernel(a_ref, b_ref, o_ref):
    o_ref[...] = jnp.dot(a_ref[...].astype(BF16), b_ref[...].astype(BF16),
                         preferred_element_type=F32).astype(o_ref.dtype)


def _matmul(a, b, *, tm, tn, name, out_dtype=F32):
    m, k = a.shape
    _, n = b.shape
    tm, tn = min(tm, m), min(tn, n)
    assert m % tm == 0 and n % tn == 0, (m, n, tm, tn)
    return pl.pallas_call(
        _mm_kernel,
        grid=(n // tn, m // tm),
        in_specs=[pl.BlockSpec((tm, k), lambda j, i: (i, 0)),
                  pl.BlockSpec((k, tn), lambda j, i: (0, j))],
        out_specs=pl.BlockSpec((tm, tn), lambda j, i: (i, j)),
        out_shape=jax.ShapeDtypeStruct((m, n), out_dtype),
        compiler_params=_cparams("parallel", "parallel"),
        name=name,
    )(a, b)


def _ln_kernel(x_ref, *rest):
    *h_refs, g_ref, b_ref, o_ref = rest
    t = DEEPNORM_ALPHA * x_ref[...]
    for h_ref in h_refs:
        t = t + h_ref[...]
    mu = jnp.mean(t, axis=-1, keepdims=True)
    c = t - mu
    var = jnp.mean(c * c, axis=-1, keepdims=True)
    o_ref[...] = c * lax.rsqrt(var + LN_EPS) * g_ref[...] + b_ref[...]


def _res_layer_norm(x, hs, g, b, *, name, tm=256):
    m, d = x.shape
    assert m % tm == 0
    row = pl.BlockSpec((tm, d), lambda i: (i, 0))
    vec = pl.BlockSpec((1, d), lambda i: (0, 0))
    return pl.pallas_call(
        _ln_kernel, grid=(m // tm,), in_specs=[row] * (1 + len(hs)) + [vec, vec], out_specs=row,
        out_shape=jax.ShapeDtypeStruct((m, d), F32),
        compiler_params=_cparams("parallel"), name=name,
    )(x, *hs, g.reshape(1, d), b.reshape(1, d))


def _mm_res_ln_kernel(a_ref, w_ref, x_ref, g_ref, b_ref, o_ref, ob_ref):
    h = jnp.dot(a_ref[...].astype(BF16), w_ref[...], preferred_element_type=F32)
    t = DEEPNORM_ALPHA * x_ref[...] + h
    mu = jnp.mean(t, axis=-1, keepdims=True)
    c = t - mu
    var = jnp.mean(c * c, axis=-1, keepdims=True)
    y = c * lax.rsqrt(var + LN_EPS) * g_ref[...] + b_ref[...]
    o_ref[...] = y
    ob_ref[...] = y.astype(BF16)


def _matmul_res_ln(a, w, x, g, b, *, name, tm=512):
    m, k = a.shape
    d = w.shape[1]
    assert m % tm == 0
    row = lambda c: pl.BlockSpec((tm, c), lambda i: (i, 0))
    const = lambda r, c: pl.BlockSpec((r, c), lambda i: (0, 0))
    in_specs = [row(k), const(k, d), row(d), const(1, d), const(1, d)]
    args = [a, w.astype(BF16), x, g.reshape(1, d), b.reshape(1, d)]
    out_specs = [row(d), row(d)]
    out_shape = [jax.ShapeDtypeStruct((m, d), F32), jax.ShapeDtypeStruct((m, d), BF16)]
    return pl.pallas_call(
        _mm_res_ln_kernel,
        grid=(m // tm,), in_specs=in_specs, out_specs=out_specs, out_shape=out_shape,
        compiler_params=_cparams("parallel"), name=name,
    )(*args)


LANES = 128
ATTN_ROWS = 256
GROUP_QKV_W = 3 * DIL_GROUP_W
LANE_CHUNKS = DIL_GROUP_W // LANES


def _rotary_tables(s):
    half = DIL_HEAD_DIM // 2
    inv = ROPE_THETA ** (-jnp.arange(half, dtype=F32) * 2.0 / DIL_HEAD_DIM)
    ang = jnp.arange(s, dtype=F32)[:, None] * inv[None, :]
    cos, sin = jnp.cos(ang), jnp.sin(ang)
    return (jnp.concatenate([cos, cos, cos, cos], axis=-1),
            jnp.concatenate([-sin, sin, -sin, sin], axis=-1))


def _attn_in_kernel(x_ref, w_ref, cos_ref, sin_ref, o0_ref, o1_ref, o2_ref, scr_ref):
    tm = x_ref.shape[0]
    acc = jnp.dot(x_ref[...].astype(BF16), w_ref[...], preferred_element_type=F32)
    cos, sin = cos_ref[...], sin_ref[...]
    lane = lax.broadcasted_iota(jnp.int32, (tm, LANES), 1)
    first_half = (lane % DIL_HEAD_DIM) < (DIL_HEAD_DIM // 2)
    outs = (o0_ref, o1_ref, o2_ref)
    slab = 0
    for part in range(3):
        for gi, (_, d) in enumerate(DIL_GROUPS):
            for c in range(LANE_CHUNKS):
                col = part * DIL_DIM + gi * DIL_GROUP_W + c * LANES
                x = acc[:, col:col + LANES]
                if part < 2:
                    partner = jnp.where(first_half, pltpu.roll(x, LANES - 32, 1),
                                        pltpu.roll(x, 32, 1))
                    x = x * cos + partner * sin
                dst = part * DIL_GROUP_W + c * LANES
                if d == 1:
                    outs[gi][:, dst:dst + LANES] = x
                else:
                    scr_ref[slab] = x
                    for r in range(d):
                        outs[gi][:, r * GROUP_QKV_W + dst:r * GROUP_QKV_W + dst + LANES] = (
                            scr_ref[slab, pl.ds(r, tm // d, stride=d), :])
                    slab += 1


def _attn_in_proj(x, w_attn, cos_t, sin_t):
    b, s, dm = x.shape
    tm = min(ATTN_ROWS, s)
    dmax = max(d for _, d in DIL_GROUPS)
    assert s % tm == 0 and tm % (dmax * 8) == 0
    n_strided = sum(3 * LANE_CHUNKS for _, d in DIL_GROUPS if d > 1)
    tab = pl.BlockSpec((tm, LANES), lambda i, j: (j, 0))
    ospec = lambda d: pl.BlockSpec((None, tm // d, d * GROUP_QKV_W), lambda i, j: (i, j, 0))
    return pl.pallas_call(
        _attn_in_kernel, grid=(b, s // tm),
        in_specs=[pl.BlockSpec((None, tm, dm), lambda i, j: (i, j, 0)),
                  pl.BlockSpec(w_attn.shape, lambda i, j: (0, 0), pipeline_mode=pl.Buffered(1)),
                  tab, tab],
        out_specs=[ospec(d) for _, d in DIL_GROUPS],
        out_shape=[jax.ShapeDtypeStruct((b, s // d, d * GROUP_QKV_W), F32) for _, d in DIL_GROUPS],
        scratch_shapes=[pltpu.VMEM((n_strided, tm, LANES), F32)],
        compiler_params=_cparams("parallel", "parallel"), name="in_attn_rope",
    )(x, w_attn.astype(BF16), cos_t, sin_t)


def _dil_attn_kernel(q_ref, kp_ref, kc_ref, vp_ref, vc_ref, o_ref, l_ref, *, span):
    nb = pl.program_id(2)
    qi = lax.broadcasted_iota(jnp.int32, (BLK, 2 * BLK), 0) + BLK
    ki = lax.broadcasted_iota(jnp.int32, (BLK, 2 * BLK), 1)
    dist = qi - ki
    mask = (dist >= 0) & (dist <= span) & ((nb > 0) | (ki >= BLK))
    scale = DIL_HEAD_DIM ** -0.5
    q = q_ref[...].astype(BF16)
    k = jnp.concatenate([kp_ref[...], kc_ref[...]], axis=0).astype(BF16)
    v = jnp.concatenate([vp_ref[...], vc_ref[...]], axis=0).astype(BF16)
    for h in range(DIL_HEADS):
        sl = slice(h * DIL_HEAD_DIM, (h + 1) * DIL_HEAD_DIM)
        s = lax.dot_general(q[:, sl], k[:, sl], (((1,), (1,)), ((), ())),
                            preferred_element_type=F32) * scale
        s = jnp.where(mask, s, NEG_INF)
        m = jnp.max(s, axis=-1, keepdims=True)
        p = jnp.exp(s - m)
        den = jnp.sum(p, axis=-1, keepdims=True)
        o = jnp.dot(p.astype(BF16), v[:, sl], preferred_element_type=F32)
        o_ref[:, sl] = o / den
        l_ref[:, sl] = jnp.broadcast_to(m + jnp.log(den), (BLK, DIL_HEAD_DIM))


def _dilated_attention(qkv_view, gi, window, dilation):
    b, n, _ = qkv_view.shape
    d = dilation
    assert n % BLK == 0
    nblk = n // BLK
    blk = (None, BLK, DIL_GROUP_W)

    def col(which):
        return lambda bi, r, nb: (bi, nb, r * 3 + which)

    def col_prev(which):
        return lambda bi, r, nb: (bi, jnp.maximum(nb - 1, 0), r * 3 + which)

    out_spec = pl.BlockSpec(blk, lambda bi, r, nb: (bi, nb, r))
    return pl.pallas_call(
        functools.partial(_dil_attn_kernel, span=window // dilation),
        grid=(b, d, nblk),
        in_specs=[pl.BlockSpec(blk, col(0)),
                  pl.BlockSpec(blk, col_prev(1)), pl.BlockSpec(blk, col(1)),
                  pl.BlockSpec(blk, col_prev(2)), pl.BlockSpec(blk, col(2))],
        out_specs=[out_spec, out_spec],
        out_shape=[jax.ShapeDtypeStruct((b, n, d * DIL_GROUP_W), F32)] * 2,
        compiler_params=_cparams("parallel", "parallel", "arbitrary"),
        name=f"dil_attn_g{gi}",
    )(qkv_view, qkv_view, qkv_view, qkv_view, qkv_view)


def _mix_merge_kernel(o0_ref, l0_ref, o1_ref, l1_ref, o2_ref, l2_ref, pa_ref, pg_ref, w_ref,
                      out_ref, scr_ref):
    tm, dm = pa_ref.shape
    slab = [0]

    def natural(ref, d):
        if d == 1:
            return [ref[:, c * LANES:(c + 1) * LANES] for c in range(LANE_CHUNKS)]
        chunks = []
        for c in range(LANE_CHUNKS):
            for r in range(d):
                scr_ref[slab[0], pl.ds(r, tm // d, stride=d), :] = (
                    ref[:, r * DIL_GROUP_W + c * LANES:r * DIL_GROUP_W + (c + 1) * LANES])
            chunks.append(scr_ref[slab[0]])
            slab[0] += 1
        return chunks

    dils = [d for _, d in DIL_GROUPS]
    o = [natural(r, d) for r, d in zip((o0_ref, o1_ref, o2_ref), dils)]
    l = [natural(r, d) for r, d in zip((l0_ref, l1_ref, l2_ref), dils)]
    yb = []
    for c in range(LANE_CHUNKS):
        m = jnp.maximum(jnp.maximum(l[0][c], l[1][c]), l[2][c])
        e = [jnp.exp(l[g][c] - m) for g in range(N_GROUPS)]
        den = e[0] + e[1] + e[2]
        yb.append((e[0] / den) * o[0][c] + (e[1] / den) * o[1][c] + (e[2] / den) * o[2][c])
    pb = jnp.dot(jnp.concatenate(yb, axis=-1).astype(BF16), w_ref[...], preferred_element_type=F32)
    out_ref[...] = (jax.nn.sigmoid(pg_ref[:, :dm]) * pa_ref[...]
                    + jax.nn.sigmoid(pg_ref[:, dm:]) * pb).astype(out_ref.dtype)


def _mix_merge(outs, lses, pa, p_gate, proj_b, bsz, s):
    n_tok, dm = pa.shape
    tm = min(ATTN_ROWS, s)
    per_b = s // tm
    n_strided = sum(2 * LANE_CHUNKS for _, d in DIL_GROUPS if d > 1)
    vspec = lambda d: pl.BlockSpec((None, tm // d, d * DIL_GROUP_W),
                                   lambda i: (i // per_b, i % per_b, 0))
    views = []
    for (_, d), o, l in zip(DIL_GROUPS, outs, lses):
        views += [(o, vspec(d)), (l, vspec(d))]
    row = lambda w: pl.BlockSpec((tm, w), lambda i: (i, 0))
    return pl.pallas_call(
        _mix_merge_kernel, grid=(n_tok // tm,),
        in_specs=[sp for _, sp in views] + [row(dm), row(2 * dm),
                                            pl.BlockSpec(proj_b.shape, lambda i: (0, 0))],
        out_specs=row(dm), out_shape=jax.ShapeDtypeStruct((n_tok, dm), BF16),
        scratch_shapes=[pltpu.VMEM((n_strided, tm, LANES), F32)],
        compiler_params=_cparams("parallel"), name="attn_mix_merge",
    )(*[a for a, _ in views], pa, p_gate, proj_b.astype(BF16))


W1_SPLIT_SUB = 256
SCAN_STEPS = 32
SCAN_SLAB = 16


def _split_even_odd(w_ref, g_ref, l_ref):
    sub, half = W1_SPLIT_SUB, W1_SPLIT_SUB // 2
    src = lax.broadcasted_iota(jnp.int32, (sub, sub), 0)
    dst = lax.broadcasted_iota(jnp.int32, (sub, sub), 1)
    want = jnp.where(dst < half, 2 * dst, 2 * (dst - half) + 1)
    sel = (src == want).astype(BF16)
    for q in range(w_ref.shape[1] // sub):
        w = w_ref[:, q * sub:(q + 1) * sub].astype(BF16)
        r = jnp.dot(w, sel, preferred_element_type=F32).astype(BF16)
        g_ref[:, q * half:(q + 1) * half] = r[:, :half]
        l_ref[:, q * half:(q + 1) * half] = r[:, half:]


def _rwkv_scan_kernel(r_ref, w_ref, k_ref, a_ref, b_ref, v_ref, w1_ref, w2_ref, y_ref, w1g_ref,
                      w1l_ref, w2b_ref, s_ref, *, tc):
    n = RWKV_HEAD

    @pl.when(pl.program_id(0) == 0)
    def _():
        s_ref[...] = jnp.zeros_like(s_ref)

    _split_even_odd(w1_ref, w1g_ref, w1l_ref)
    w2b_ref[...] = w2_ref[...].astype(BF16)

    slabs = n // SCAN_SLAB
    zero = jnp.zeros((SCAN_SLAB, s_ref.shape[2]), F32)

    def row(ref, tb, tt, j):
        return ref[tb, j, tt:tt + 1, :]

    def rows_of(slab):
        if isinstance(slab, int):
            return pl.ds(slab * SCAN_SLAB, SCAN_SLAB)
        return pl.ds(pl.multiple_of(slab * SCAN_SLAB, SCAN_SLAB), SCAN_SLAB)

    def state_times_a(tb, tt, slab):
        rows = rows_of(slab)
        acc = [zero, zero]
        for j in range(n):
            acc[j % 2] = acc[j % 2] + s_ref[j, rows, :] * row(a_ref, tb, tt, j)
        return acc[0] + acc[1]

    def update_and_read(tb, tt, slab, sa):
        rows = rows_of(slab)
        t = tb * SUBLANES + tt
        vt = v_ref[t, rows, :]
        yac = [zero, zero]
        for j in range(n):
            sj = (s_ref[j, rows, :] * row(w_ref, tb, tt, j) + sa * row(b_ref, tb, tt, j)
                  + vt * row(k_ref, tb, tt, j))
            s_ref[j, rows, :] = sj
            yac[j % 2] = yac[j % 2] + sj * row(r_ref, tb, tt, j)
        y_ref[t, rows, :] = yac[0] + yac[1]

    nblk = tc // SUBLANES

    def block(tb, sa):
        for tt in range(SUBLANES):
            def trip(slab, sa, tt=tt):
                update_and_read(tb, tt, slab, sa)
                return state_times_a(tb, tt, slab + 1)

            sa = lax.fori_loop(0, slabs - 1, trip, sa)
            update_and_read(tb, tt, slabs - 1, sa)
            if tt + 1 < SUBLANES:
                sa = state_times_a(tb, tt + 1, 0)
            else:
                sa = state_times_a(jnp.minimum(tb + 1, nblk - 1), 0, 0)
        return sa

    lax.fori_loop(0, nblk, block, state_times_a(0, 0, 0))


def _rwkv_scan_and_moe_weight_prep(r, w, k, a, b, v, w1, w2):
    s, n, l = v.shape
    tc = min(SCAN_STEPS, s)
    assert s % tc == 0
    steps = s // tc
    assert w1.shape[-1] % W1_SPLIT_SUB == 0

    def sliced(wgt, cols_out):
        e, rows, cols = wgt.shape
        assert (e * rows) % steps == 0
        per_step = e * rows // steps
        assert rows % per_step == 0 and per_step % (2 * SUBLANES) == 0
        per_e = rows // per_step
        spec = lambda c: pl.BlockSpec((None, per_step, c), lambda i: (i // per_e, i % per_e, 0))
        return spec(cols), spec(cols_out), jax.ShapeDtypeStruct((e, rows, cols_out), BF16)

    w1_in, w1_out, w1_shape = sliced(w1, w1.shape[-1] // 2)
    w2_in, w2_out, w2_shape = sliced(w2, w2.shape[-1])
    rows = pl.BlockSpec((tc // SUBLANES, n, SUBLANES, l), lambda i: (i, 0, 0, 0))
    blk = pl.BlockSpec((tc, n, l), lambda i: (i, 0, 0))
    return pl.pallas_call(
        functools.partial(_rwkv_scan_kernel, tc=tc),
        grid=(steps,), in_specs=[rows] * 5 + [blk, w1_in, w2_in],
        out_specs=[blk, w1_out, w1_out, w2_out],
        out_shape=[jax.ShapeDtypeStruct((s, n, l), F32), w1_shape, w1_shape, w2_shape],
        scratch_shapes=[pltpu.VMEM((n, n, l), F32)],
        compiler_params=_cparams("arbitrary"), name="rwkv_scan",
    )(r, w, k, a, b, v, w1, w2)


RWKV_PREP_ROWS = 32
SUBLANES = 8


def _head_sum(x):
    cols = x.shape[-1] // LANES
    acc = x[:, :LANES]
    for c in range(1, cols):
        acc = acc + x[:, c * LANES:(c + 1) * LANES]
    acc = acc + pltpu.roll(acc, RWKV_HEADS, 1)
    acc = acc + pltpu.roll(acc, 2 * RWKV_HEADS, 1)
    return jnp.concatenate([acc] * cols, axis=-1)


def _token_shift_rows(p, last_prev_row, mix, first):
    rolled = pltpu.roll(p, 1, 0)
    row0 = jnp.where(first, jnp.zeros_like(last_prev_row), last_prev_row)
    t = lax.broadcasted_iota(jnp.int32, p.shape, 0)
    prev = jnp.where(t == 0, row0, rolled)
    return p + (prev - p) * mix


def _rwkv_prep_kernel(rkv_ref, rkvp_ref, lo_ref, lop_ref, mixr_ref, mixl_ref, w0_ref, a0_ref,
                      kk_ref, ka_ref, rk_ref, wup_ref, aup_ref, gup_ref,
                      r_ref, w_ref, k_ref, al_ref, be_ref, v_ref, g_ref, bo_ref, tiles_ref):
    first = pl.program_id(0) == 0
    d = D_MODEL
    nb, ts = rkv_ref.shape[0], rkv_ref.shape[1]
    mm = lambda a, w_ref_: jnp.dot(a.astype(BF16), w_ref_[...].astype(BF16),
                                   preferred_element_type=F32)
    for bi in range(nb):
        def seg(c):
            sl = slice(c * d, (c + 1) * d)
            return _token_shift_rows(rkv_ref[bi, :, sl], rkvp_ref[bi, SUBLANES - 1:SUBLANES, sl],
                                     mixr_ref[:, sl], first)

        r, k, v = seg(0), seg(1), seg(2)
        zl = _token_shift_rows(lo_ref[bi], lop_ref[bi, SUBLANES - 1:SUBLANES, :], mixl_ref[...],
                               first)
        wd = zl[:, :DECAY_LORA]
        ad = zl[:, DECAY_LORA:DECAY_LORA + AAA_LORA]
        gd = zl[:, DECAY_LORA + AAA_LORA:]
        z = -(w0_ref[...] + mm(jnp.tanh(wd), wup_ref))
        softplus = jnp.maximum(z, 0.0) + jnp.log1p(jnp.exp(-jnp.abs(z)))
        w = -softplus - 0.5
        a = jax.nn.sigmoid(a0_ref[...] + mm(ad, aup_ref))
        kk = k * kk_ref[...]
        kk = kk / jnp.maximum(jnp.sqrt(_head_sum(kk * kk)), 1e-12)
        k2 = k * (1.0 + (a - 1.0) * ka_ref[...])
        for op, val in enumerate((r, jnp.exp(-jnp.exp(w)), k2, -kk, kk * a)):
            for c in range(d // LANES):
                col = val[:, c * LANES:(c + 1) * LANES]
                tiles_ref[op, bi, :, c * LANES:(c + 1) * LANES] = (
                    col if bi == 0 else pltpu.roll(col, bi * RWKV_HEADS, 1))
        v_ref[bi] = v
        g_ref[bi] = mm(jax.nn.sigmoid(gd), gup_ref)
        bo_ref[bi] = _head_sum(r * k2 * rk_ref[...]) * v

    lane_group = lax.broadcasted_iota(jnp.int32, (ts, LANES), 1) // RWKV_HEADS
    for op, out_ref in enumerate((r_ref, w_ref, k_ref, al_ref, be_ref)):
        for c in range(d // LANES):
            cols = [tiles_ref[op, bi, :, c * LANES:(c + 1) * LANES] for bi in range(nb)]
            for jm in range(nb):
                sel = cols[(0 - jm) % nb]
                for g in range(1, nb):
                    sel = jnp.where(lane_group == g, cols[(g - jm) % nb], sel)
                out = sel if jm == 0 else pltpu.roll(sel, (nb - jm) * RWKV_HEADS, 1)
                for tb in range(ts // SUBLANES):
                    out_ref[tb, c * nb + jm] = out[tb * SUBLANES:(tb + 1) * SUBLANES]


def _rwkv_prep(p_rkv, p_lora, shift_mix, w0, w_up, a0, a_up, g_up, k_k, k_a, r_k):
    b, s, c_rkv = p_rkv.shape
    d, ts, c_lo = D_MODEL, min(RWKV_PREP_ROWS, s), p_lora.shape[-1]
    assert s % ts == 0 and ts % SUBLANES == 0
    assert b * RWKV_HEADS == LANES, "the scan packs exactly (batch, head) onto the 128 lanes"
    cur = lambda c: pl.BlockSpec((b, ts, c), lambda i: (0, i, 0))
    prev = lambda c: pl.BlockSpec(
        (b, SUBLANES, c), lambda i: (0, jnp.maximum(i * (ts // SUBLANES) - 1, 0), 0))
    vec = lambda c: pl.BlockSpec((1, c), lambda i: (0, 0))
    full = lambda a: pl.BlockSpec(a.shape, lambda i: (0, 0))
    rows = pl.BlockSpec((ts // SUBLANES, RWKV_HEAD, SUBLANES, LANES), lambda i: (i, 0, 0, 0))
    rows_shape = jax.ShapeDtypeStruct((s // SUBLANES, RWKV_HEAD, SUBLANES, LANES), F32)
    tok_shape = jax.ShapeDtypeStruct((b, s, d), F32)
    row = lambda t: t.reshape(1, -1)
    return pl.pallas_call(
        _rwkv_prep_kernel, grid=(s // ts,),
        in_specs=[cur(c_rkv), prev(c_rkv), cur(c_lo), prev(c_lo), vec(c_rkv), vec(c_lo),
                  vec(d), vec(d), vec(d), vec(d), vec(d), full(w_up), full(a_up), full(g_up)],
        out_specs=[rows] * 5 + [cur(d)] * 3, out_shape=[rows_shape] * 5 + [tok_shape] * 3,
        scratch_shapes=[pltpu.VMEM((5, b, ts, d), F32)],
        compiler_params=_cparams("arbitrary"), name="rwkv_prep",
    )(p_rkv, p_rkv, p_lora, p_lora, row(shift_mix[:c_rkv]), row(shift_mix[c_rkv:]), row(w0),
      row(a0), row(k_k), row(k_a), row(r_k), w_up, a_up, g_up)


def _proj_a_kernel(y_ref, bo_ref, g_ref, lg_ref, lb_ref, w_ref, o_ref):
    y = y_ref[...]
    inv_n = 1.0 / RWKV_HEAD
    c = y - _head_sum(y) * inv_n
    var = _head_sum(c * c) * inv_n
    ya = (c * lax.rsqrt(var + GN_EPS) * lg_ref[...] + lb_ref[...] + bo_ref[...]) * g_ref[...]
    o_ref[...] = jnp.dot(ya.astype(BF16), w_ref[...], preferred_element_type=F32)


def _rwkv_post_proj(y, bonus, gate, lnx_g, lnx_b, proj_a, *, tm=256):
    m, d = y.shape
    assert m % tm == 0
    rowb = pl.BlockSpec((tm, d), lambda i: (i, 0))
    vec = pl.BlockSpec((1, d), lambda i: (0, 0))
    return pl.pallas_call(
        _proj_a_kernel, grid=(m // tm,),
        in_specs=[rowb, rowb, rowb, vec, vec, pl.BlockSpec((d, d), lambda i: (0, 0))],
        out_specs=rowb, out_shape=jax.ShapeDtypeStruct((m, d), F32),
        compiler_params=_cparams("parallel"), name="rwkv_post_proj_a",
    )(y, bonus, gate, lnx_g.reshape(1, d), lnx_b.reshape(1, d), proj_a.astype(BF16))


def _cross_attn_kernel(q_ref, k_ref, v_ref, o_ref):
    scale = CA_HEAD_DIM ** -0.5
    for h in range(CA_HEADS):
        sl = slice(h * CA_HEAD_DIM, (h + 1) * CA_HEAD_DIM)
        q = q_ref[:, sl].astype(BF16)
        k = k_ref[:, sl].astype(BF16)
        s = lax.dot_general(q, k, (((1,), (1,)), ((), ())), preferred_element_type=F32) * scale
        m = jnp.max(s, axis=-1, keepdims=True)
        p = jnp.exp(s - m)
        den = jnp.sum(p, axis=-1, keepdims=True)
        o = jnp.dot(p.astype(BF16), v_ref[:, sl].astype(BF16), preferred_element_type=F32)
        o_ref[:, sl] = (o / den).astype(o_ref.dtype)


def _cross_attention(q, kv, *, tq=512):
    b, s, d = q.shape
    mlen = kv.shape[1]
    tq = min(tq, s)
    return pl.pallas_call(
        _cross_attn_kernel, grid=(b, s // tq),
        in_specs=[pl.BlockSpec((None, tq, d), lambda i, j: (i, j, 0)),
                  pl.BlockSpec((None, mlen, d), lambda i, j: (i, 0, 0)),
                  pl.BlockSpec((None, mlen, d), lambda i, j: (i, 0, 1))],
        out_specs=pl.BlockSpec((None, tq, d), lambda i, j: (i, j, 0)),
        out_shape=jax.ShapeDtypeStruct((b, s, d), BF16),
        compiler_params=_cparams("parallel", "parallel"), name="cross_attn",
    )(q, kv, kv)


def _router_kernel(x_ref, w_ref, b_ref, o_ref):
    o_ref[...] = jnp.dot(x_ref[...], w_ref[...], precision=lax.Precision.HIGHEST,
                         preferred_element_type=F32) + b_ref[...]


def _router_logits(x, w, b, *, tm=512):
    m, d = x.shape
    e = w.shape[1]
    return pl.pallas_call(
        _router_kernel, grid=(m // tm,),
        in_specs=[pl.BlockSpec((tm, d), lambda i: (i, 0)),
                  pl.BlockSpec((d, e), lambda i: (0, 0)),
                  pl.BlockSpec((1, e), lambda i: (0, 0))],
        out_specs=pl.BlockSpec((tm, e), lambda i: (i, 0)),
        out_shape=jax.ShapeDtypeStruct((m, e), F32),
        compiler_params=_cparams("parallel"), name="router",
    )(x, w, b.reshape(1, e))


def _moe_kernel(be_ref, nused_ref, x_ref, g_ref, w1g_ref, w1l_ref, b1g_ref, b1l_ref, w2_ref,
                b2_ref, o_ref):
    i = pl.program_id(0)

    @pl.when(i < nused_ref[0])
    def _():
        x = x_ref[...]
        glu = jnp.dot(x, w1g_ref[...], preferred_element_type=F32) + b1g_ref[...]
        lin = jnp.dot(x, w1l_ref[...], preferred_element_type=F32) + b1l_ref[...]
        glu = jnp.minimum(glu, SWIGLU_LIMIT)
        lin = jnp.clip(lin, -SWIGLU_LIMIT, SWIGLU_LIMIT)
        act = glu * jax.nn.sigmoid(SWIGLU_ALPHA * glu) * (lin + 1.0)
        y = jnp.dot(act.astype(BF16), w2_ref[...], preferred_element_type=F32) + b2_ref[...]
        o_ref[...] = y * g_ref[...]

    @pl.when(i >= nused_ref[0])
    def _():
        o_ref[...] = jnp.zeros_like(o_ref)


def _moe_experts(blk_exp, n_used, xs, row_gate, w1g, w1l, b1g, b1l, w2, b2):
    rows, d = xs.shape
    nblk = rows // MOE_BLK
    f = w1g.shape[-1]
    wspec = lambda shape: pl.BlockSpec(shape, lambda i, be, nu: (be[i], 0, 0))
    grid_spec = pltpu.PrefetchScalarGridSpec(
        num_scalar_prefetch=2, grid=(nblk,),
        in_specs=[pl.BlockSpec((MOE_BLK, d), lambda i, be, nu: (i, 0)),
                  pl.BlockSpec((MOE_BLK, 1), lambda i, be, nu: (i, 0)),
                  wspec((None, d, f)), wspec((None, d, f)),
                  wspec((None, 1, f)), wspec((None, 1, f)),
                  wspec((None, f, d)), wspec((None, 1, d))],
        out_specs=pl.BlockSpec((MOE_BLK, d), lambda i, be, nu: (i, 0)),
    )
    return pl.pallas_call(
        _moe_kernel, grid_spec=grid_spec,
        out_shape=jax.ShapeDtypeStruct((rows, d), F32),
        compiler_params=pltpu.CompilerParams(dimension_semantics=("arbitrary",),
                                             vmem_limit_bytes=V7X_VMEM_LIMIT_MOE_BYTES),
        name="moe_experts",
    )(blk_exp, n_used, xs, row_gate.reshape(rows, 1), w1g, w1l, b1g, b1l, w2, b2)


def _per_expert(table, idx):
    hit = idx[:, None] == jnp.arange(N_EXPERTS, dtype=idx.dtype)[None, :]
    return jnp.sum(jnp.where(hit, table[None, :], 0), axis=1)


def _moe_ffn(x_bf, logits, moe_weights, b1, b2):
    n, d = x_bf.shape
    top_val, top_idx = lax.top_k(logits, TOP_K)
    top_w = jax.nn.softmax(top_val, axis=-1)
    flat_e = top_idx.reshape(-1).astype(jnp.int32)
    order = jnp.argsort(flat_e).astype(jnp.int32)
    rank = jnp.argsort(order).astype(jnp.int32)
    counts = jnp.sum(flat_e[:, None] == jnp.arange(N_EXPERTS, dtype=jnp.int32)[None, :],
                     axis=0, dtype=jnp.int32)
    starts = jnp.cumsum(counts) - counts
    padded = (counts + MOE_BLK - 1) // MOE_BLK * MOE_BLK
    pends = jnp.cumsum(padded)
    pstarts = pends - padded
    rows = n * TOP_K + N_EXPERTS * MOE_BLK
    nblk = rows // MOE_BLK
    blk_start = jnp.arange(nblk, dtype=jnp.int32) * MOE_BLK
    blk_exp = jnp.minimum(jnp.sum(pends[None, :] <= blk_start[:, None], axis=1, dtype=jnp.int32),
                          N_EXPERTS - 1)
    n_used = (pends[-1] // MOE_BLK).astype(jnp.int32).reshape(1)
    row_exp = jnp.repeat(blk_exp, MOE_BLK)
    off = jnp.arange(rows, dtype=jnp.int32) - _per_expert(pstarts, row_exp)
    valid = off < _per_expert(counts, row_exp)
    assign = order[jnp.where(valid, _per_expert(starts, row_exp) + off, 0)]
    row_tok = jnp.where(valid, assign // TOP_K, 0)
    row_gate = jnp.where(valid, top_w.reshape(-1)[assign], 0.0)
    xs = x_bf[row_tok]
    b1g = b1[:, None, 0::2]
    b1l = b1[:, None, 1::2]
    w1g, w1l, w2 = moe_weights
    ys = _moe_experts(blk_exp, n_used, xs, row_gate, w1g, w1l, b1g, b1l, w2,
                      b2[:, None, :])
    pos = (_per_expert(pstarts, flat_e) + rank - _per_expert(starts, flat_e)).reshape(n, TOP_K)
    return [ys[pos[:, kk]] for kk in range(TOP_K)]


def _heads_minor(p):
    lead = p.shape[:-1]
    return p.reshape(lead + (RWKV_HEADS, RWKV_HEAD)).swapaxes(-1, -2).reshape(p.shape)


def _to_scan_layout(t, bsz, s):
    return t.reshape(bsz, s, RWKV_HEAD, RWKV_HEADS).transpose(1, 2, 0, 3).reshape(
        s, RWKV_HEAD, bsz * RWKV_HEADS)


def _from_scan_layout(t, bsz, s):
    return t.reshape(s, RWKV_HEAD, bsz, RWKV_HEADS).transpose(2, 0, 1, 3).reshape(bsz, s, D_MODEL)


def _rwkv7_branch_proj(p_rkv, p_lora, shift_mix, w0, w_up, a0, a_up, g_up, k_k, k_a, r_k, lnx_g,
                       lnx_b, proj_a, moe_w1, moe_w2):
    bsz, s, c_rkv = p_rkv.shape
    hm = _heads_minor
    mix = jnp.concatenate([hm(shift_mix[:c_rkv].reshape(3, D_MODEL)).reshape(-1),
                           shift_mix[c_rkv:]])
    r, decay, k, alpha, beta, v, gate, bonus = _rwkv_prep(
        p_rkv, p_lora, mix, hm(w0), hm(w_up), hm(a0), hm(a_up), hm(g_up), hm(k_k), hm(k_a),
        hm(r_k.reshape(-1)))
    y, *moe_weights = _rwkv_scan_and_moe_weight_prep(r, decay, k, alpha, beta,
                                                     _to_scan_layout(v, bsz, s), moe_w1, moe_w2)
    y = _from_scan_layout(y, bsz, s)
    flat = lambda t: t.reshape(bsz * s, D_MODEL)
    proj_a_hm = proj_a.reshape(RWKV_HEADS, RWKV_HEAD, -1).swapaxes(0, 1).reshape(proj_a.shape)
    pa = _rwkv_post_proj(flat(y), flat(bonus), flat(gate), hm(lnx_g), hm(lnx_b), proj_a_hm)
    return pa, moe_weights


def _hybrid_mixer(x, w_in, shift_mix, w0, w_up, a0, a_up, g_up, k_k, k_a, r_k, lnx_g, lnx_b,
                  proj_a, proj_b, moe_w1, moe_w2):
    bsz, s, d = x.shape
    n_tok = bsz * s
    x_bf = x.astype(BF16)
    x2d = x_bf.reshape(n_tok, d)
    c0, c1, c2 = 3 * D_MODEL, RWKV_COLS, RWKV_COLS + ATTN_COLS
    w_rkv = _heads_minor(w_in[:, :c0].reshape(d, 3, D_MODEL)).reshape(d, c0)
    p_rkv = _matmul(x2d, w_rkv, tm=1024, tn=1024, name="in_rkv").reshape(bsz, s, -1)
    p_lora = _matmul(x2d, w_in[:, c0:c1], tm=1024, tn=512, name="in_lora").reshape(bsz, s, -1)
    p_gate = _matmul(x2d, w_in[:, c2:], tm=1024, tn=1024, name="in_gate")

    pa, moe_weights = _rwkv7_branch_proj(p_rkv, p_lora, shift_mix, w0, w_up, a0, a_up, g_up, k_k,
                                         k_a, r_k, lnx_g, lnx_b, proj_a, moe_w1, moe_w2)

    cos_t, sin_t = _rotary_tables(s)
    views = _attn_in_proj(x_bf, w_in[:, c1:c2], cos_t, sin_t)
    outs, lses = [], []
    for gi, (window, dilation) in enumerate(DIL_GROUPS):
        o, l = _dilated_attention(views[gi], gi, window, dilation)
        outs.append(o)
        lses.append(l)
    return _mix_merge(outs, lses, pa, p_gate, proj_b, bsz, s), moe_weights


def _memory_cross_attention(x_bf, mem, wq, wkv, bsz, s):
    d = x_bf.shape[-1]
    q = _matmul(x_bf, wq, tm=1024, tn=1024, name="ca_q").reshape(bsz, s, d)
    kv = _matmul(mem.reshape(-1, d), wkv, tm=512, tn=1024, name="ca_kv").reshape(bsz, -1, 2 * d)
    return _cross_attention(q, kv).reshape(-1, d)


def kernel(x, mem, w_in, shift_mix, w0, w_up, a0, a_up, g_up, k_k, k_a, r_k, lnx_g, lnx_b, proj_a, proj_b, w_out, ln1_g, ln1_b, ca_wq, ca_wkv, ca_wo, ln2_g, ln2_b, router_w, router_b, moe_w1, moe_b1, moe_w2, moe_b2, ln3_g, ln3_b):
    bsz, s, d = x.shape
    n_tok = bsz * s
    for l in range(DEPTH):
        merged, moe_weights = _hybrid_mixer(x, w_in[l], shift_mix[l], w0[l], w_up[l], a0[l],
                                            a_up[l], g_up[l], k_k[l], k_a[l], r_k[l], lnx_g[l],
                                            lnx_b[l], proj_a[l], proj_b[l], moe_w1[l], moe_w2[l])
        x1, x1_bf = _matmul_res_ln(merged, w_out[l], x.reshape(n_tok, d), ln1_g[l], ln1_b[l],
                                   name="w_out_ln1")
        o = _memory_cross_attention(x1_bf, mem, ca_wq[l], ca_wkv[l], bsz, s)
        x2, x2_bf = _matmul_res_ln(o, ca_wo[l], x1, ln2_g[l], ln2_b[l], name="ca_o_ln2")
        logits = _router_logits(x2, router_w[l], router_b[l])
        hs = _moe_ffn(x2_bf, logits, moe_weights, moe_b1[l], moe_b2[l])
        x = _res_layer_norm(x2, hs, ln3_g[l], ln3_b[l], name="ln3").reshape(bsz, s, d)
    return x
```

```python
import functools

import jax
import jax.numpy as jnp
from jax import lax
from jax.experimental import pallas as pl
from jax.experimental.pallas import tpu as pltpu

F32 = jnp.float32
BF16 = jnp.bfloat16

D_MODEL = 2048
RWKV_HEAD = 64
RWKV_HEADS = D_MODEL // RWKV_HEAD
DECAY_LORA = 96
AAA_LORA = 96
GATE_LORA = 256
GN_EPS = 64e-5
DIL_GROUPS = ((128, 1), (512, 4), (2048, 16))
N_GROUPS = len(DIL_GROUPS)
DIL_HEADS = 8
DIL_HEAD_DIM = 64
DIL_GROUP_W = DIL_HEADS * DIL_HEAD_DIM
DIL_DIM = N_GROUPS * DIL_GROUP_W
BLK = 128
ROPE_THETA = 10000.0
NEG_INF = -1e30
RWKV_COLS = 3 * D_MODEL + DECAY_LORA + AAA_LORA + GATE_LORA
ATTN_COLS = 3 * DIL_DIM
CA_HEADS = 4
CA_HEAD_DIM = D_MODEL // CA_HEADS
N_EXPERTS = 32
TOP_K = 4
D_FF = D_MODEL
SWIGLU_LIMIT = 7.0
SWIGLU_ALPHA = 1.702
MOE_BLK = 128
LN_EPS = 1e-5
DEPTH = 1
DEEPNORM_ALPHA = (2 * DEPTH) ** 0.25

V7X_VMEM_LIMIT_BYTES = 56 * 1024 * 1024
V7X_VMEM_LIMIT_MOE_BYTES = 61 * 1024 * 1024


def _cparams(*sem):
    return pltpu.CompilerParams(dimension_semantics=sem, vmem_limit_bytes=V7X_VMEM_LIMIT_BYTES)


def _mm_kernel(a_ref, b_ref, o_ref):
    o_ref[...] = jnp.dot(a_ref[...].astype(BF16), b_ref[...].astype(BF16),
                         preferred_element_type=F32).astype(o_ref.dtype)


def _matmul(a, b, *, tm, tn, name, out_dtype=F32):
    m, k = a.shape
    _, n = b.shape
    tm, tn = min(tm, m), min(tn, n)
    assert m % tm == 0 and n % tn == 0, (m, n, tm, tn)
    return pl.pallas_call(
        _mm_kernel,
        grid=(n // tn, m // tm),
        in_specs=[pl.BlockSpec((tm, k), lambda j, i: (i, 0)),
                  pl.BlockSpec((k, tn), lambda j, i: (0, j))],
        out_specs=pl.BlockSpec((tm, tn), lambda j, i: (i, j)),
        out_shape=jax.ShapeDtypeStruct((m, n), out_dtype),
        compiler_params=_cparams("parallel", "parallel"),
        name=name,
    )(a, b)


def _ln_kernel(x_ref, *rest):
    *h_refs, g_ref, b_ref, o_ref = rest
    t = DEEPNORM_ALPHA * x_ref[...]
    for h_ref in h_refs:
        t = t + h_ref[...]
    mu = jnp.mean(t, axis=-1, keepdims=True)
    c = t - mu
    var = jnp.mean(c * c, axis=-1, keepdims=True)
    o_ref[...] = c * lax.rsqrt(var + LN_EPS) * g_ref[...] + b_ref[...]


def _res_layer_norm(x, hs, g, b, *, name, tm=256):
    m, d = x.shape
    assert m % tm == 0
    row = pl.BlockSpec((tm, d), lambda i: (i, 0))
    vec = pl.BlockSpec((1, d), lambda i: (0, 0))
    return pl.pallas_call(
        _ln_kernel, grid=(m // tm,), in_specs=[row] * (1 + len(hs)) + [vec, vec], out_specs=row,
        out_shape=jax.ShapeDtypeStruct((m, d), F32),
        compiler_params=_cparams("parallel"), name=name,
    )(x, *hs, g.reshape(1, d), b.reshape(1, d))


def _mm_res_ln_kernel(a_ref, w_ref, x_ref, g_ref, b_ref, o_ref, ob_ref):
    h = jnp.dot(a_ref[...].astype(BF16), w_ref[...], preferred_element_type=F32)
    t = DEEPNORM_ALPHA * x_ref[...] + h
    mu = jnp.mean(t, axis=-1, keepdims=True)
    c = t - mu
    var = jnp.mean(c * c, axis=-1, keepdims=True)
    y = c * lax.rsqrt(var + LN_EPS) * g_ref[...] + b_ref[...]
    o_ref[...] = y
    ob_ref[...] = y.astype(BF16)


def _matmul_res_ln(a, w, x, g, b, *, name, tm=512):
    m, k = a.shape
    d = w.shape[1]
    assert m % tm == 0
    row = lambda c: pl.BlockSpec((tm, c), lambda i: (i, 0))
    const = lambda r, c: pl.BlockSpec((r, c), lambda i: (0, 0))
    in_specs = [row(k), const(k, d), row(d), const(1, d), const(1, d)]
    args = [a, w.astype(BF16), x, g.reshape(1, d), b.reshape(1, d)]
    out_specs = [row(d), row(d)]
    out_shape = [jax.ShapeDtypeStruct((m, d), F32), jax.ShapeDtypeStruct((m, d), BF16)]
    return pl.pallas_call(
        _mm_res_ln_kernel,
        grid=(m // tm,), in_specs=in_specs, out_specs=out_specs, out_shape=out_shape,
        compiler_params=_cparams("parallel"), name=name,
    )(*args)


LANES = 128
ATTN_ROWS = 256
GROUP_QKV_W = 3 * DIL_GROUP_W
LANE_CHUNKS = DIL_GROUP_W // LANES


def _rotary_tables(s):
    half = DIL_HEAD_DIM // 2
    inv = ROPE_THETA ** (-jnp.arange(half, dtype=F32) * 2.0 / DIL_HEAD_DIM)
    ang = jnp.arange(s, dtype=F32)[:, None] * inv[None, :]
    cos, sin = jnp.cos(ang), jnp.sin(ang)
    return (jnp.concatenate([cos, cos, cos, cos], axis=-1),
            jnp.concatenate([-sin, sin, -sin, sin], axis=-1))


def _attn_in_kernel(x_ref, w_ref, cos_ref, sin_ref, o0_ref, o1_ref, o2_ref, scr_ref):
    tm = x_ref.shape[0]
    acc = jnp.dot(x_ref[...].astype(BF16), w_ref[...], preferred_element_type=F32)
    cos, sin = cos_ref[...], sin_ref[...]
    lane = lax.broadcasted_iota(jnp.int32, (tm, LANES), 1)
    first_half = (lane % DIL_HEAD_DIM) < (DIL_HEAD_DIM // 2)
    outs = (o0_ref, o1_ref, o2_ref)
    slab = 0
    for part in range(3):
        for gi, (_, d) in enumerate(DIL_GROUPS):
            for c in range(LANE_CHUNKS):
                col = part * DIL_DIM + gi * DIL_GROUP_W + c * LANES
                x = acc[:, col:col + LANES]
                if part < 2:
                    partner = jnp.where(first_half, pltpu.roll(x, LANES - 32, 1),
                                        pltpu.roll(x, 32, 1))
                    x = x * cos + partner * sin
                dst = part * DIL_GROUP_W + c * LANES
                if d == 1:
                    outs[gi][:, dst:dst + LANES] = x
                else:
                    scr_ref[slab] = x
                    for r in range(d):
                        outs[gi][:, r * GROUP_QKV_W + dst:r * GROUP_QKV_W + dst + LANES] = (
                            scr_ref[slab, pl.ds(r, tm // d, stride=d), :])
                    slab += 1


def _attn_in_proj(x, w_attn, cos_t, sin_t):
    b, s, dm = x.shape
    tm = min(ATTN_ROWS, s)
    dmax = max(d for _, d in DIL_GROUPS)
    assert s % tm == 0 and tm % (dmax * 8) == 0
    n_strided = sum(3 * LANE_CHUNKS for _, d in DIL_GROUPS if d > 1)
    tab = pl.BlockSpec((tm, LANES), lambda i, j: (j, 0))
    ospec = lambda d: pl.BlockSpec((None, tm // d, d * GROUP_QKV_W), lambda i, j: (i, j, 0))
    return pl.pallas_call(
        _attn_in_kernel, grid=(b, s // tm),
        in_specs=[pl.BlockSpec((None, tm, dm), lambda i, j: (i, j, 0)),
                  pl.BlockSpec(w_attn.shape, lambda i, j: (0, 0), pipeline_mode=pl.Buffered(1)),
                  tab, tab],
        out_specs=[ospec(d) for _, d in DIL_GROUPS],
        out_shape=[jax.ShapeDtypeStruct((b, s // d, d * GROUP_QKV_W), F32) for _, d in DIL_GROUPS],
        scratch_shapes=[pltpu.VMEM((n_strided, tm, LANES), F32)],
        compiler_params=_cparams("parallel", "parallel"), name="in_attn_rope",
    )(x, w_attn.astype(BF16), cos_t, sin_t)


def _dil_attn_kernel(q_ref, kp_ref, kc_ref, vp_ref, vc_ref, o_ref, l_ref, *, span):
    nb = pl.program_id(2)
    qi = lax.broadcasted_iota(jnp.int32, (BLK, 2 * BLK), 0) + BLK
    ki = lax.broadcasted_iota(jnp.int32, (BLK, 2 * BLK), 1)
    dist = qi - ki
    mask = (dist >= 0) & (dist <= span) & ((nb > 0) | (ki >= BLK))
    scale = DIL_HEAD_DIM ** -0.5
    q = q_ref[...].astype(BF16)
    k = jnp.concatenate([kp_ref[...], kc_ref[...]], axis=0).astype(BF16)
    v = jnp.concatenate([vp_ref[...], vc_ref[...]], axis=0).astype(BF16)
    for h in range(DIL_HEADS):
        sl = slice(h * DIL_HEAD_DIM, (h + 1) * DIL_HEAD_DIM)
        s = lax.dot_general(q[:, sl], k[:, sl], (((1,), (1,)), ((), ())),
                            preferred_element_type=F32) * scale
        s = jnp.where(mask, s, NEG_INF)
        m = jnp.max(s, axis=-1, keepdims=True)
        p = jnp.exp(s - m)
        den = jnp.sum(p, axis=-1, keepdims=True)
        o = jnp.dot(p.astype(BF16), v[:, sl], preferred_element_type=F32)
        o_ref[:, sl] = o / den
        l_ref[:, sl] = jnp.broadcast_to(m + jnp.log(den), (BLK, DIL_HEAD_DIM))


def _dilated_attention(qkv_view, gi, window, dilation):
    b, n, _ = qkv_view.shape
    d = dilation
    assert n % BLK == 0
    nblk = n // BLK
    blk = (None, BLK, DIL_GROUP_W)

    def col(which):
        return lambda bi, r, nb: (bi, nb, r * 3 + which)

    def col_prev(which):
        return lambda bi, r, nb: (bi, jnp.maximum(nb - 1, 0), r * 3 + which)

    out_spec = pl.BlockSpec(blk, lambda bi, r, nb: (bi, nb, r))
    return pl.pallas_call(
        functools.partial(_dil_attn_kernel, span=window // dilation),
        grid=(b, d, nblk),
        in_specs=[pl.BlockSpec(blk, col(0)),
                  pl.BlockSpec(blk, col_prev(1)), pl.BlockSpec(blk, col(1)),
                  pl.BlockSpec(blk, col_prev(2)), pl.BlockSpec(blk, col(2))],
        out_specs=[out_spec, out_spec],
        out_shape=[jax.ShapeDtypeStruct((b, n, d * DIL_GROUP_W), F32)] * 2,
        compiler_params=_cparams("parallel", "parallel", "arbitrary"),
        name=f"dil_attn_g{gi}",
    )(qkv_view, qkv_view, qkv_view, qkv_view, qkv_view)


def _mix_merge_kernel(o0_ref, l0_ref, o1_ref, l1_ref, o2_ref, l2_ref, pa_ref, pg_ref, w_ref,
                      out_ref, scr_ref):
    tm, dm = pa_ref.shape
    slab = [0]

    def natural(ref, d):
        if d == 1:
            return [ref[:, c * LANES:(c + 1) * LANES] for c in range(LANE_CHUNKS)]
        chunks = []
        for c in range(LANE_CHUNKS):
            for r in range(d):
                scr_ref[slab[0], pl.ds(r, tm // d, stride=d), :] = (
                    ref[:, r * DIL_GROUP_W + c * LANES:r * DIL_GROUP_W + (c + 1) * LANES])
            chunks.append(scr_ref[slab[0]])
            slab[0] += 1
        return chunks

    dils = [d for _, d in DIL_GROUPS]
    o = [natural(r, d) for r, d in zip((o0_ref, o1_ref, o2_ref), dils)]
    l = [natural(r, d) for r, d in zip((l0_ref, l1_ref, l2_ref), dils)]
    yb = []
    for c in range(LANE_CHUNKS):
        m = jnp.maximum(jnp.maximum(l[0][c], l[1][c]), l[2][c])
        e = [jnp.exp(l[g][c] - m) for g in range(N_GROUPS)]
        den = e[0] + e[1] + e[2]
        yb.append((e[0] / den) * o[0][c] + (e[1] / den) * o[1][c] + (e[2] / den) * o[2][c])
    pb = jnp.dot(jnp.concatenate(yb, axis=-1).astype(BF16), w_ref[...], preferred_element_type=F32)
    out_ref[...] = (jax.nn.sigmoid(pg_ref[:, :dm]) * pa_ref[...]
                    + jax.nn.sigmoid(pg_ref[:, dm:]) * pb).astype(out_ref.dtype)


def _mix_merge(outs, lses, pa, p_gate, proj_b, bsz, s):
    n_tok, dm = pa.shape
    tm = min(ATTN_ROWS, s)
    per_b = s // tm
    n_strided = sum(2 * LANE_CHUNKS for _, d in DIL_GROUPS if d > 1)
    vspec = lambda d: pl.BlockSpec((None, tm // d, d * DIL_GROUP_W),
                                   lambda i: (i // per_b, i % per_b, 0))
    views = []
    for (_, d), o, l in zip(DIL_GROUPS, outs, lses):
        views += [(o, vspec(d)), (l, vspec(d))]
    row = lambda w: pl.BlockSpec((tm, w), lambda i: (i, 0))
    return pl.pallas_call(
        _mix_merge_kernel, grid=(n_tok // tm,),
        in_specs=[sp for _, sp in views] + [row(dm), row(2 * dm),
                                            pl.BlockSpec(proj_b.shape, lambda i: (0, 0))],
        out_specs=row(dm), out_shape=jax.ShapeDtypeStruct((n_tok, dm), BF16),
        scratch_shapes=[pltpu.VMEM((n_strided, tm, LANES), F32)],
        compiler_params=_cparams("parallel"), name="attn_mix_merge",
    )(*[a for a, _ in views], pa, p_gate, proj_b.astype(BF16))


W1_SPLIT_SUB = 256
SCAN_STEPS = 32
SCAN_SLAB = 16


def _split_even_odd(w_ref, g_ref, l_ref):
    sub, half = W1_SPLIT_SUB, W1_SPLIT_SUB // 2
    src = lax.broadcasted_iota(jnp.int32, (sub, sub), 0)
    dst = lax.broadcasted_iota(jnp.int32, (sub, sub), 1)
    want = jnp.where(dst < half, 2 * dst, 2 * (dst - half) + 1)
    sel = (src == want).astype(BF16)
    for q in range(w_ref.shape[1] // sub):
        w = w_ref[:, q * sub:(q + 1) * sub].astype(BF16)
        r = jnp.dot(w, sel, preferred_element_type=F32).astype(BF16)
        g_ref[:, q * half:(q + 1) * half] = r[:, :half]
        l_ref[:, q * half:(q + 1) * half] = r[:, half:]


def _rwkv_scan_kernel(r_ref, w_ref, k_ref, a_ref, b_ref, v_ref, w1_ref, w2_ref, y_ref, w1g_ref,
                      w1l_ref, w2b_ref, s_ref, *, tc):
    n = RWKV_HEAD

    @pl.when(pl.program_id(0) == 0)
    def _():
        s_ref[...] = jnp.zeros_like(s_ref)

    _split_even_odd(w1_ref, w1g_ref, w1l_ref)
    w2b_ref[...] = w2_ref[...].astype(BF16)

    slabs = n // SCAN_SLAB
    zero = jnp.zeros((SCAN_SLAB, s_ref.shape[2]), F32)

    def row(ref, tb, tt, j):
        return ref[tb, j, tt:tt + 1, :]

    def rows_of(slab):
        if isinstance(slab, int):
            return pl.ds(slab * SCAN_SLAB, SCAN_SLAB)
        return pl.ds(pl.multiple_of(slab * SCAN_SLAB, SCAN_SLAB), SCAN_SLAB)

    def state_times_a(tb, tt, slab):
        rows = rows_of(slab)
        acc = [zero, zero]
        for j in range(n):
            acc[j % 2] = acc[j % 2] + s_ref[j, rows, :] * row(a_ref, tb, tt, j)
        return acc[0] + acc[1]

    def update_and_read(tb, tt, slab, sa):
        rows = rows_of(slab)
        t = tb * SUBLANES + tt
        vt = v_ref[t, rows, :]
        yac = [zero, zero]
        for j in range(n):
            sj = (s_ref[j, rows, :] * row(w_ref, tb, tt, j) + sa * row(b_ref, tb, tt, j)
                  + vt * row(k_ref, tb, tt, j))
            s_ref[j, rows, :] = sj
            yac[j % 2] = yac[j % 2] + sj * row(r_ref, tb, tt, j)
        y_ref[t, rows, :] = yac[0] + yac[1]

    nblk = tc // SUBLANES

    def block(tb, sa):
        for tt in range(SUBLANES):
            def trip(slab, sa, tt=tt):
                update_and_read(tb, tt, slab, sa)
                return state_times_a(tb, tt, slab + 1)

            sa = lax.fori_loop(0, slabs - 1, trip, sa)
            update_and_read(tb, tt, slabs - 1, sa)
            if tt + 1 < SUBLANES:
                sa = state_times_a(tb, tt + 1, 0)
            else:
                sa = state_times_a(jnp.minimum(tb + 1, nblk - 1), 0, 0)
        return sa

    lax.fori_loop(0, nblk, block, state_times_a(0, 0, 0))


def _rwkv_scan_and_moe_weight_prep(r, w, k, a, b, v, w1, w2):
    s, n, l = v.shape
    tc = min(SCAN_STEPS, s)
    assert s % tc == 0
    steps = s // tc
    assert w1.shape[-1] % W1_SPLIT_SUB == 0

    def sliced(wgt, cols_out):
        e, rows, cols = wgt.shape
        assert (e * rows) % steps == 0
        per_step = e * rows // steps
        assert rows % per_step == 0 and per_step % (2 * SUBLANES) == 0
        per_e = rows // per_step
        spec = lambda c: pl.BlockSpec((None, per_step, c), lambda i: (i // per_e, i % per_e, 0))
        return spec(cols), spec(cols_out), jax.ShapeDtypeStruct((e, rows, cols_out), BF16)

    w1_in, w1_out, w1_shape = sliced(w1, w1.shape[-1] // 2)
    w2_in, w2_out, w2_shape = sliced(w2, w2.shape[-1])
    rows = pl.BlockSpec((tc // SUBLANES, n, SUBLANES, l), lambda i: (i, 0, 0, 0))
    blk = pl.BlockSpec((tc, n, l), lambda i: (i, 0, 0))
    return pl.pallas_call(
        functools.partial(_rwkv_scan_kernel, tc=tc),
        grid=(steps,), in_specs=[rows] * 5 + [blk, w1_in, w2_in],
        out_specs=[blk, w1_out, w1_out, w2_out],
        out_shape=[jax.ShapeDtypeStruct((s, n, l), F32), w1_shape, w1_shape, w2_shape],
        scratch_shapes=[pltpu.VMEM((n, n, l), F32)],
        compiler_params=_cparams("arbitrary"), name="rwkv_scan",
    )(r, w, k, a, b, v, w1, w2)


RWKV_PREP_ROWS = 32
SUBLANES = 8


def _head_sum(x):
    cols = x.shape[-1] // LANES
    acc = x[:, :LANES]
    for c in range(1, cols):
        acc = acc + x[:, c * LANES:(c + 1) * LANES]
    acc = acc + pltpu.roll(acc, RWKV_HEADS, 1)
    acc = acc + pltpu.roll(acc, 2 * RWKV_HEADS, 1)
    return jnp.concatenate([acc] * cols, axis=-1)


def _token_shift_rows(p, last_prev_row, mix, first):
    rolled = pltpu.roll(p, 1, 0)
    row0 = jnp.where(first, jnp.zeros_like(last_prev_row), last_prev_row)
    t = lax.broadcasted_iota(jnp.int32, p.shape, 0)
    prev = jnp.where(t == 0, row0, rolled)
    return p + (prev - p) * mix


def _rwkv_prep_kernel(rkv_ref, rkvp_ref, lo_ref, lop_ref, mixr_ref, mixl_ref, w0_ref, a0_ref,
                      kk_ref, ka_ref, rk_ref, wup_ref, aup_ref, gup_ref,
                      r_ref, w_ref, k_ref, al_ref, be_ref, v_ref, g_ref, bo_ref, tiles_ref):
    first = pl.program_id(0) == 0
    d = D_MODEL
    nb, ts = rkv_ref.shape[0], rkv_ref.shape[1]
    mm = lambda a, w_ref_: jnp.dot(a.astype(BF16), w_ref_[...].astype(BF16),
                                   preferred_element_type=F32)
    for bi in range(nb):
        def seg(c):
            sl = slice(c * d, (c + 1) * d)
            return _token_shift_rows(rkv_ref[bi, :, sl], rkvp_ref[bi, SUBLANES - 1:SUBLANES, sl],
                                     mixr_ref[:, sl], first)

        r, k, v = seg(0), seg(1), seg(2)
        zl = _token_shift_rows(lo_ref[bi], lop_ref[bi, SUBLANES - 1:SUBLANES, :], mixl_ref[...],
                               first)
        wd = zl[:, :DECAY_LORA]
        ad = zl[:, DECAY_LORA:DECAY_LORA + AAA_LORA]
        gd = zl[:, DECAY_LORA + AAA_LORA:]
        z = -(w0_ref[...] + mm(jnp.tanh(wd), wup_ref))
        softplus = jnp.maximum(z, 0.0) + jnp.log1p(jnp.exp(-jnp.abs(z)))
        w = -softplus - 0.5
        a = jax.nn.sigmoid(a0_ref[...] + mm(ad, aup_ref))
        kk = k * kk_ref[...]
        kk = kk / jnp.maximum(jnp.sqrt(_head_sum(kk * kk)), 1e-12)
        k2 = k * (1.0 + (a - 1.0) * ka_ref[...])
        for op, val in enumerate((r, jnp.exp(-jnp.exp(w)), k2, -kk, kk * a)):
            for c in range(d // LANES):
                col = val[:, c * LANES:(c + 1) * LANES]
                tiles_ref[op, bi, :, c * LANES:(c + 1) * LANES] = (
                    col if bi == 0 else pltpu.roll(col, bi * RWKV_HEADS, 1))
        v_ref[bi] = v
        g_ref[bi] = mm(jax.nn.sigmoid(gd), gup_ref)
        bo_ref[bi] = _head_sum(r * k2 * rk_ref[...]) * v

    lane_group = lax.broadcasted_iota(jnp.int32, (ts, LANES), 1) // RWKV_HEADS
    for op, out_ref in enumerate((r_ref, w_ref, k_ref, al_ref, be_ref)):
        for c in range(d // LANES):
            cols = [tiles_ref[op, bi, :, c * LANES:(c + 1) * LANES] for bi in range(nb)]
            for jm in range(nb):
                sel = cols[(0 - jm) % nb]
                for g in range(1, nb):
                    sel = jnp.where(lane_group == g, cols[(g - jm) % nb], sel)
                out = sel if jm == 0 else pltpu.roll(sel, (nb - jm) * RWKV_HEADS, 1)
                for tb in range(ts // SUBLANES):
                    out_ref[tb, c * nb + jm] = out[tb * SUBLANES:(tb + 1) * SUBLANES]


def _rwkv_prep(p_rkv, p_lora, shift_mix, w0, w_up, a0, a_up, g_up, k_k, k_a, r_k):
    b, s, c_rkv = p_rkv.shape
    d, ts, c_lo = D_MODEL, min(RWKV_PREP_ROWS, s), p_lora.shape[-1]
    assert s % ts == 0 and ts % SUBLANES == 0
    assert b * RWKV_HEADS == LANES, "the scan packs exactly (batch, head) onto the 128 lanes"
    cur = lambda c: pl.BlockSpec((b, ts, c), lambda i: (0, i, 0))
    prev = lambda c: pl.BlockSpec(
        (b, SUBLANES, c), lambda i: (0, jnp.maximum(i * (ts // SUBLANES) - 1, 0), 0))
    vec = lambda c: pl.BlockSpec((1, c), lambda i: (0, 0))
    full = lambda a: pl.BlockSpec(a.shape, lambda i: (0, 0))
    rows = pl.BlockSpec((ts // SUBLANES, RWKV_HEAD, SUBLANES, LANES), lambda i: (i, 0, 0, 0))
    rows_shape = jax.ShapeDtypeStruct((s // SUBLANES, RWKV_HEAD, SUBLANES, LANES), F32)
    tok_shape = jax.ShapeDtypeStruct((b, s, d), F32)
    row = lambda t: t.reshape(1, -1)
    return pl.pallas_call(
        _rwkv_prep_kernel, grid=(s // ts,),
        in_specs=[cur(c_rkv), prev(c_rkv), cur(c_lo), prev(c_lo), vec(c_rkv), vec(c_lo),
                  vec(d), vec(d), vec(d), vec(d), vec(d), full(w_up), full(a_up), full(g_up)],
        out_specs=[rows] * 5 + [cur(d)] * 3, out_shape=[rows_shape] * 5 + [tok_shape] * 3,
        scratch_shapes=[pltpu.VMEM((5, b, ts, d), F32)],
        compiler_params=_cparams("arbitrary"), name="rwkv_prep",
    )(p_rkv, p_rkv, p_lora, p_lora, row(shift_mix[:c_rkv]), row(shift_mix[c_rkv:]), row(w0),
      row(a0), row(k_k), row(k_a), row(r_k), w_up, a_up, g_up)


def _proj_a_kernel(y_ref, bo_ref, g_ref, lg_ref, lb_ref, w_ref, o_ref):
    y = y_ref[...]
    inv_n = 1.0 / RWKV_HEAD
    c = y - _head_sum(y) * inv_n
    var = _head_sum(c * c) * inv_n
    ya = (c * lax.rsqrt(var + GN_EPS) * lg_ref[...] + lb_ref[...] + bo_ref[...]) * g_ref[...]
    o_ref[...] = jnp.dot(ya.astype(BF16), w_ref[...], preferred_element_type=F32)


def _rwkv_post_proj(y, bonus, gate, lnx_g, lnx_b, proj_a, *, tm=256):
    m, d = y.shape
    assert m % tm == 0
    rowb = pl.BlockSpec((tm, d), lambda i: (i, 0))
    vec = pl.BlockSpec((1, d), lambda i: (0, 0))
    return pl.pallas_call(
        _proj_a_kernel, grid=(m // tm,),
        in_specs=[rowb, rowb, rowb, vec, vec, pl.BlockSpec((d, d), lambda i: (0, 0))],
        out_specs=rowb, out_shape=jax.ShapeDtypeStruct((m, d), F32),
        compiler_params=_cparams("parallel"), name="rwkv_post_proj_a",
    )(y, bonus, gate, lnx_g.reshape(1, d), lnx_b.reshape(1, d), proj_a.astype(BF16))


def _cross_attn_kernel(q_ref, k_ref, v_ref, o_ref):
    scale = CA_HEAD_DIM ** -0.5
    for h in range(CA_HEADS):
        sl = slice(h * CA_HEAD_DIM, (h + 1) * CA_HEAD_DIM)
        q = q_ref[:, sl].astype(BF16)
        k = k_ref[:, sl].astype(BF16)
        s = lax.dot_general(q, k, (((1,), (1,)), ((), ())), preferred_element_type=F32) * scale
        m = jnp.max(s, axis=-1, keepdims=True)
        p = jnp.exp(s - m)
        den = jnp.sum(p, axis=-1, keepdims=True)
        o = jnp.dot(p.astype(BF16), v_ref[:, sl].astype(BF16), preferred_element_type=F32)
        o_ref[:, sl] = (o / den).astype(o_ref.dtype)


def _cross_attention(q, kv, *, tq=512):
    b, s, d = q.shape
    mlen = kv.shape[1]
    tq = min(tq, s)
    return pl.pallas_call(
        _cross_attn_kernel, grid=(b, s // tq),
        in_specs=[pl.BlockSpec((None, tq, d), lambda i, j: (i, j, 0)),
                  pl.BlockSpec((None, mlen, d), lambda i, j: (i, 0, 0)),
                  pl.BlockSpec((None, mlen, d), lambda i, j: (i, 0, 1))],
        out_specs=pl.BlockSpec((None, tq, d), lambda i, j: (i, j, 0)),
        out_shape=jax.ShapeDtypeStruct((b, s, d), BF16),
        compiler_params=_cparams("parallel", "parallel"), name="cross_attn",
    )(q, kv, kv)


def _router_kernel(x_ref, w_ref, b_ref, o_ref):
    o_ref[...] = jnp.dot(x_ref[...], w_ref[...], precision=lax.Precision.HIGHEST,
                         preferred_element_type=F32) + b_ref[...]


def _router_logits(x, w, b, *, tm=512):
    m, d = x.shape
    e = w.shape[1]
    return pl.pallas_call(
        _router_kernel, grid=(m // tm,),
        in_specs=[pl.BlockSpec((tm, d), lambda i: (i, 0)),
                  pl.BlockSpec((d, e), lambda i: (0, 0)),
                  pl.BlockSpec((1, e), lambda i: (0, 0))],
        out_specs=pl.BlockSpec((tm, e), lambda i: (i, 0)),
        out_shape=jax.ShapeDtypeStruct((m, e), F32),
        compiler_params=_cparams("parallel"), name="router",
    )(x, w, b.reshape(1, e))


def _moe_kernel(be_ref, nused_ref, x_ref, g_ref, w1g_ref, w1l_ref, b1g_ref, b1l_ref, w2_ref,
                b2_ref, o_ref):
    i = pl.program_id(0)

    @pl.when(i < nused_ref[0])
    def _():
        x = x_ref[...]
        glu = jnp.dot(x, w1g_ref[...], preferred_element_type=F32) + b1g_ref[...]
        lin = jnp.dot(x, w1l_ref[...], preferred_element_type=F32) + b1l_ref[...]
        glu = jnp.minimum(glu, SWIGLU_LIMIT)
        lin = jnp.clip(lin, -SWIGLU_LIMIT, SWIGLU_LIMIT)
        act = glu * jax.nn.sigmoid(SWIGLU_ALPHA * glu) * (lin + 1.0)
        y = jnp.dot(act.astype(BF16), w2_ref[...], preferred_element_type=F32) + b2_ref[...]
        o_ref[...] = y * g_ref[...]

    @pl.when(i >= nused_ref[0])
    def _():
        o_ref[...] = jnp.zeros_like(o_ref)


def _moe_experts(blk_exp, n_used, xs, row_gate, w1g, w1l, b1g, b1l, w2, b2):
    rows, d = xs.shape
    nblk = rows // MOE_BLK
    f = w1g.shape[-1]
    wspec = lambda shape: pl.BlockSpec(shape, lambda i, be, nu: (be[i], 0, 0))
    grid_spec = pltpu.PrefetchScalarGridSpec(
        num_scalar_prefetch=2, grid=(nblk,),
        in_specs=[pl.BlockSpec((MOE_BLK, d), lambda i, be, nu: (i, 0)),
                  pl.BlockSpec((MOE_BLK, 1), lambda i, be, nu: (i, 0)),
                  wspec((None, d, f)), wspec((None, d, f)),
                  wspec((None, 1, f)), wspec((None, 1, f)),
                  wspec((None, f, d)), wspec((None, 1, d))],
        out_specs=pl.BlockSpec((MOE_BLK, d), lambda i, be, nu: (i, 0)),
    )
    return pl.pallas_call(
        _moe_kernel, grid_spec=grid_spec,
        out_shape=jax.ShapeDtypeStruct((rows, d), F32),
        compiler_params=pltpu.CompilerParams(dimension_semantics=("arbitrary",),
                                             vmem_limit_bytes=V7X_VMEM_LIMIT_MOE_BYTES),
        name="moe_experts",
    )(blk_exp, n_used, xs, row_gate.reshape(rows, 1), w1g, w1l, b1g, b1l, w2, b2)


def _per_expert(table, idx):
    hit = idx[:, None] == jnp.arange(N_EXPERTS, dtype=idx.dtype)[None, :]
    return jnp.sum(jnp.where(hit, table[None, :], 0), axis=1)


def _moe_ffn(x_bf, logits, moe_weights, b1, b2):
    n, d = x_bf.shape
    top_val, top_idx = lax.top_k(logits, TOP_K)
    top_w = jax.nn.softmax(top_val, axis=-1)
    flat_e = top_idx.reshape(-1).astype(jnp.int32)
    order = jnp.argsort(flat_e).astype(jnp.int32)
    rank = jnp.argsort(order).astype(jnp.int32)
    counts = jnp.sum(flat_e[:, None] == jnp.arange(N_EXPERTS, dtype=jnp.int32)[None, :],
                     axis=0, dtype=jnp.int32)
    starts = jnp.cumsum(counts) - counts
    padded = (counts + MOE_BLK - 1) // MOE_BLK * MOE_BLK
    pends = jnp.cumsum(padded)
    pstarts = pends - padded
    rows = n * TOP_K + N_EXPERTS * MOE_BLK
    nblk = rows // MOE_BLK
    blk_start = jnp.arange(nblk, dtype=jnp.int32) * MOE_BLK
    blk_exp = jnp.minimum(jnp.sum(pends[None, :] <= blk_start[:, None], axis=1, dtype=jnp.int32),
                          N_EXPERTS - 1)
    n_used = (pends[-1] // MOE_BLK).astype(jnp.int32).reshape(1)
    row_exp = jnp.repeat(blk_exp, MOE_BLK)
    off = jnp.arange(rows, dtype=jnp.int32) - _per_expert(pstarts, row_exp)
    valid = off < _per_expert(counts, row_exp)
    assign = order[jnp.where(valid, _per_expert(starts, row_exp) + off, 0)]
    row_tok = jnp.where(valid, assign // TOP_K, 0)
    row_gate = jnp.where(valid, top_w.reshape(-1)[assign], 0.0)
    xs = x_bf[row_tok]
    b1g = b1[:, None, 0::2]
    b1l = b1[:, None, 1::2]
    w1g, w1l, w2 = moe_weights
    ys = _moe_experts(blk_exp, n_used, xs, row_gate, w1g, w1l, b1g, b1l, w2,
                      b2[:, None, :])
    pos = (_per_expert(pstarts, flat_e) + rank - _per_expert(starts, flat_e)).reshape(n, TOP_K)
    return [ys[pos[:, kk]] for kk in range(TOP_K)]


def _heads_minor(p):
    lead = p.shape[:-1]
    return p.reshape(lead + (RWKV_HEADS, RWKV_HEAD)).swapaxes(-1, -2).reshape(p.shape)


def _to_scan_layout(t, bsz, s):
    return t.reshape(bsz, s, RWKV_HEAD, RWKV_HEADS).transpose(1, 2, 0, 3).reshape(
        s, RWKV_HEAD, bsz * RWKV_HEADS)


def _from_scan_layout(t, bsz, s):
    return t.reshape(s, RWKV_HEAD, bsz, RWKV_HEADS).transpose(2, 0, 1, 3).reshape(bsz, s, D_MODEL)


def _rwkv7_branch_proj(p_rkv, p_lora, shift_mix, w0, w_up, a0, a_up, g_up, k_k, k_a, r_k, lnx_g,
                       lnx_b, proj_a, moe_w1, moe_w2):
    bsz, s, c_rkv = p_rkv.shape
    hm = _heads_minor
    mix = jnp.concatenate([hm(shift_mix[:c_rkv].reshape(3, D_MODEL)).reshape(-1),
                           shift_mix[c_rkv:]])
    r, decay, k, alpha, beta, v, gate, bonus = _rwkv_prep(
        p_rkv, p_lora, mix, hm(w0), hm(w_up), hm(a0), hm(a_up), hm(g_up), hm(k_k), hm(k_a),
        hm(r_k.reshape(-1)))
    y, *moe_weights = _rwkv_scan_and_moe_weight_prep(r, decay, k, alpha, beta,
                                                     _to_scan_layout(v, bsz, s), moe_w1, moe_w2)
    y = _from_scan_layout(y, bsz, s)
    flat = lambda t: t.reshape(bsz * s, D_MODEL)
    proj_a_hm = proj_a.reshape(RWKV_HEADS, RWKV_HEAD, -1).swapaxes(0, 1).reshape(proj_a.shape)
    pa = _rwkv_post_proj(flat(y), flat(bonus), flat(gate), hm(lnx_g), hm(lnx_b), proj_a_hm)
    return pa, moe_weights


def _hybrid_mixer(x, w_in, shift_mix, w0, w_up, a0, a_up, g_up, k_k, k_a, r_k, lnx_g, lnx_b,
                  proj_a, proj_b, moe_w1, moe_w2):
    bsz, s, d = x.shape
    n_tok = bsz * s
    x_bf = x.astype(BF16)
    x2d = x_bf.reshape(n_tok, d)
    c0, c1, c2 = 3 * D_MODEL, RWKV_COLS, RWKV_COLS + ATTN_COLS
    w_rkv = _heads_minor(w_in[:, :c0].reshape(d, 3, D_MODEL)).reshape(d, c0)
    p_rkv = _matmul(x2d, w_rkv, tm=1024, tn=1024, name="in_rkv").reshape(bsz, s, -1)
    p_lora = _matmul(x2d, w_in[:, c0:c1], tm=1024, tn=512, name="in_lora").reshape(bsz, s, -1)
    p_gate = _matmul(x2d, w_in[:, c2:], tm=1024, tn=1024, name="in_gate")

    pa, moe_weights = _rwkv7_branch_proj(p_rkv, p_lora, shift_mix, w0, w_up, a0, a_up, g_up, k_k,
                                         k_a, r_k, lnx_g, lnx_b, proj_a, moe_w1, moe_w2)

    cos_t, sin_t = _rotary_tables(s)
    views = _attn_in_proj(x_bf, w_in[:, c1:c2], cos_t, sin_t)
    outs, lses = [], []
    for gi, (window, dilation) in enumerate(DIL_GROUPS):
        o, l = _dilated_attention(views[gi], gi, window, dilation)
        outs.append(o)
        lses.append(l)
    return _mix_merge(outs, lses, pa, p_gate, proj_b, bsz, s), moe_weights


def _memory_cross_attention(x_bf, mem, wq, wkv, bsz, s):
    d = x_bf.shape[-1]
    q = _matmul(x_bf, wq, tm=1024, tn=1024, name="ca_q").reshape(bsz, s, d)
    kv = _matmul(mem.reshape(-1, d), wkv, tm=512, tn=1024, name="ca_kv").reshape(bsz, -1, 2 * d)
    return _cross_attention(q, kv).reshape(-1, d)


def kernel(x, mem, w_in, shift_mix, w0, w_up, a0, a_up, g_up, k_k, k_a, r_k, lnx_g, lnx_b, proj_a, proj_b, w_out, ln1_g, ln1_b, ca_wq, ca_wkv, ca_wo, ln2_g, ln2_b, router_w, router_b, moe_w1, moe_b1, moe_w2, moe_b2, ln3_g, ln3_b):
    bsz, s, d = x.shape
    n_tok = bsz * s
    for l in range(DEPTH):
        merged, moe_weights = _hybrid_mixer(x, w_in[l], shift_mix[l], w0[l], w_up[l], a0[l],
                                            a_up[l], g_up[l], k_k[l], k_a[l], r_k[l], lnx_g[l],
                                            lnx_b[l], proj_a[l], proj_b[l], moe_w1[l], moe_w2[l])
        x1, x1_bf = _matmul_res_ln(merged, w_out[l], x.reshape(n_tok, d), ln1_g[l], ln1_b[l],
                                   name="w_out_ln1")
        o = _memory_cross_attention(x1_bf, mem, ca_wq[l], ca_wkv[l], bsz, s)
        x2, x2_bf = _matmul_res_ln(o, ca_wo[l], x1, ln2_g[l], ln2_b[l], name="ca_o_ln2")
        logits = _router_logits(x2, router_w[l], router_b[l])
        hs = _moe_ffn(x2_bf, logits, moe_weights, moe_b1[l], moe_b2[l])
        x = _res_layer_norm(x2, hs, ln3_g[l], ln3_b[l], name="ln3").reshape(bsz, s, d)
    return x
```

```python
import functools

import jax
import jax.numpy as jnp
from jax import lax
from jax.experimental import pallas as pl
from jax.experimental.pallas import tpu as pltpu

F32 = jnp.float32
BF16 = jnp.bfloat16

D_MODEL = 2048
RWKV_HEAD = 64
RWKV_HEADS = D_MODEL // RWKV_HEAD
DECAY_LORA = 96
AAA_LORA = 96
GATE_LORA = 256
GN_EPS = 64e-5
DIL_GROUPS = ((128, 1), (512, 4), (2048, 16))
N_GROUPS = len(DIL_GROUPS)
DIL_HEADS = 8
DIL_HEAD_DIM = 64
DIL_GROUP_W = DIL_HEADS * DIL_HEAD_DIM
DIL_DIM = N_GROUPS * DIL_GROUP_W
BLK = 128
ROPE_THETA = 10000.0
NEG_INF = -1e30
RWKV_COLS = 3 * D_MODEL + DECAY_LORA + AAA_LORA + GATE_LORA
ATTN_COLS = 3 * DIL_DIM
CA_HEADS = 4
CA_HEAD_DIM = D_MODEL // CA_HEADS
N_EXPERTS = 32
TOP_K = 4
D_FF = D_MODEL
SWIGLU_LIMIT = 7.0
SWIGLU_ALPHA = 1.702
MOE_BLK = 128
LN_EPS = 1e-5
DEPTH = 1
DEEPNORM_ALPHA = (2 * DEPTH) ** 0.25

V7X_VMEM_LIMIT_BYTES = 56 * 1024 * 1024
V7X_VMEM_LIMIT_MOE_BYTES = 61 * 1024 * 1024


def _cparams(*sem):
    return pltpu.CompilerParams(dimension_semantics=sem, vmem_limit_bytes=V7X_VMEM_LIMIT_BYTES)


def _mm_kernel(a_ref, b_ref, o_ref):
    o_ref[...] = jnp.dot(a_ref[...].astype(BF16), b_ref[...].astype(BF16),
                         preferred_element_type=F32).astype(o_ref.dtype)


def _matmul(a, b, *, tm, tn, name, out_dtype=F32):
    m, k = a.shape
    _, n = b.shape
    tm, tn = min(tm, m), min(tn, n)
    assert m % tm == 0 and n % tn == 0, (m, n, tm, tn)
    return pl.pallas_call(
        _mm_kernel,
        grid=(n // tn, m // tm),
        in_specs=[pl.BlockSpec((tm, k), lambda j, i: (i, 0)),
                  pl.BlockSpec((k, tn), lambda j, i: (0, j))],
        out_specs=pl.BlockSpec((tm, tn), lambda j, i: (i, j)),
        out_shape=jax.ShapeDtypeStruct((m, n), out_dtype),
        compiler_params=_cparams("parallel", "parallel"),
        name=name,
    )(a, b)


def _ln_kernel(x_ref, *rest):
    *h_refs, g_ref, b_ref, o_ref = rest
    t = DEEPNORM_ALPHA * x_ref[...]
    for h_ref in h_refs:
        t = t + h_ref[...]
    mu = jnp.mean(t, axis=-1, keepdims=True)
    c = t - mu
    var = jnp.mean(c * c, axis=-1, keepdims=True)
    o_ref[...] = c * lax.rsqrt(var + LN_EPS) * g_ref[...] + b_ref[...]


def _res_layer_norm(x, hs, g, b, *, name, tm=256):
    m, d = x.shape
    assert m % tm == 0
    row = pl.BlockSpec((tm, d), lambda i: (i, 0))
    vec = pl.BlockSpec((1, d), lambda i: (0, 0))
    return pl.pallas_call(
        _ln_kernel, grid=(m // tm,), in_specs=[row] * (1 + len(hs)) + [vec, vec], out_specs=row,
        out_shape=jax.ShapeDtypeStruct((m, d), F32),
        compiler_params=_cparams("parallel"), name=name,
    )(x, *hs, g.reshape(1, d), b.reshape(1, d))


def _mm_res_ln_kernel(a_ref, w_ref, x_ref, g_ref, b_ref, o_ref, ob_ref):
    h = jnp.dot(a_ref[...].astype(BF16), w_ref[...], preferred_element_type=F32)
    t = DEEPNORM_ALPHA * x_ref[...] + h
    mu = jnp.mean(t, axis=-1, keepdims=True)
    c = t - mu
    var = jnp.mean(c * c, axis=-1, keepdims=True)
    y = c * lax.rsqrt(var + LN_EPS) * g_ref[...] + b_ref[...]
    o_ref[...] = y
    ob_ref[...] = y.astype(BF16)


def _matmul_res_ln(a, w, x, g, b, *, name, tm=512):
    m, k = a.shape
    d = w.shape[1]
    assert m % tm == 0
    row = lambda c: pl.BlockSpec((tm, c), lambda i: (i, 0))
    const = lambda r, c: pl.BlockSpec((r, c), lambda i: (0, 0))
    in_specs = [row(k), const(k, d), row(d), const(1, d), const(1, d)]
    args = [a, w.astype(BF16), x, g.reshape(1, d), b.reshape(1, d)]
    out_specs = [row(d), row(d)]
    out_shape = [jax.ShapeDtypeStruct((m, d), F32), jax.ShapeDtypeStruct((m, d), BF16)]
    return pl.pallas_call(
        _mm_res_ln_kernel,
        grid=(m // tm,), in_specs=in_specs, out_specs=out_specs, out_shape=out_shape,
        compiler_params=_cparams("parallel"), name=name,
    )(*args)


LANES = 128
ATTN_ROWS = 256
GROUP_QKV_W = 3 * DIL_GROUP_W
LANE_CHUNKS = DIL_GROUP_W // LANES


def _rotary_tables(s):
    half = DIL_HEAD_DIM // 2
    inv = ROPE_THETA ** (-jnp.arange(half, dtype=F32) * 2.0 / DIL_HEAD_DIM)
    ang = jnp.arange(s, dtype=F32)[:, None] * inv[None, :]
    cos, sin = jnp.cos(ang), jnp.sin(ang)
    return (jnp.concatenate([cos, cos, cos, cos], axis=-1),
            jnp.concatenate([-sin, sin, -sin, sin], axis=-1))


def _attn_in_kernel(x_ref, w_ref, cos_ref, sin_ref, o0_ref, o1_ref, o2_ref, scr_ref):
    tm = x_ref.shape[0]
    acc = jnp.dot(x_ref[...].astype(BF16), w_ref[...], preferred_element_type=F32)
    cos, sin = cos_ref[...], sin_ref[...]
    lane = lax.broadcasted_iota(jnp.int32, (tm, LANES), 1)
    first_half = (lane % DIL_HEAD_DIM) < (DIL_HEAD_DIM // 2)
    outs = (o0_ref, o1_ref, o2_ref)
    slab = 0
    for part in range(3):
        for gi, (_, d) in enumerate(DIL_GROUPS):
            for c in range(LANE_CHUNKS):
                col = part * DIL_DIM + gi * DIL_GROUP_W + c * LANES
                x = acc[:, col:col + LANES]
                if part < 2:
                    partner = jnp.where(first_half, pltpu.roll(x, LANES - 32, 1),
                                        pltpu.roll(x, 32, 1))
                    x = x * cos + partner * sin
                dst = part * DIL_GROUP_W + c * LANES
                if d == 1:
                    outs[gi][:, dst:dst + LANES] = x
                else:
                    scr_ref[slab] = x
                    for r in range(d):
                        outs[gi][:, r * GROUP_QKV_W + dst:r * GROUP_QKV_W + dst + LANES] = (
                            scr_ref[slab, pl.ds(r, tm // d, stride=d), :])
                    slab += 1


def _attn_in_proj(x, w_attn, cos_t, sin_t):
    b, s, dm = x.shape
    tm = min(ATTN_ROWS, s)
    dmax = max(d for _, d in DIL_GROUPS)
    assert s % tm == 0 and tm % (dmax * 8) == 0
    n_strided = sum(3 * LANE_CHUNKS for _, d in DIL_GROUPS if d > 1)
    tab = pl.BlockSpec((tm, LANES), lambda i, j: (j, 0))
    ospec = lambda d: pl.BlockSpec((None, tm // d, d * GROUP_QKV_W), lambda i, j: (i, j, 0))
    return pl.pallas_call(
        _attn_in_kernel, grid=(b, s // tm),
        in_specs=[pl.BlockSpec((None, tm, dm), lambda i, j: (i, j, 0)),
                  pl.BlockSpec(w_attn.shape, lambda i, j: (0, 0), pipeline_mode=pl.Buffered(1)),
                  tab, tab],
        out_specs=[ospec(d) for _, d in DIL_GROUPS],
        out_shape=[jax.ShapeDtypeStruct((b, s // d, d * GROUP_QKV_W), F32) for _, d in DIL_GROUPS],
        scratch_shapes=[pltpu.VMEM((n_strided, tm, LANES), F32)],
        compiler_params=_cparams("parallel", "parallel"), name="in_attn_rope",
    )(x, w_attn.astype(BF16), cos_t, sin_t)


DIL_Q_BLOCKS = 2


def _dil_attn_kernel(q_ref, kp_ref, kc_ref, vp_ref, vc_ref, o_ref, l_ref, *, span):
    step = pl.program_id(2)
    qi = lax.broadcasted_iota(jnp.int32, (BLK, 2 * BLK), 0) + BLK
    ki = lax.broadcasted_iota(jnp.int32, (BLK, 2 * BLK), 1)
    dist = qi - ki
    band = (dist >= 0) & (dist <= span)
    scale = DIL_HEAD_DIM ** -0.5
    for blk in range(DIL_Q_BLOCKS):
        rows = slice(blk * BLK, (blk + 1) * BLK)
        before = slice((blk - 1) * BLK, blk * BLK)
        mask = band & ((step > 0) | (ki >= BLK)) if blk == 0 else band
        q = q_ref[rows, :].astype(BF16)
        k_prev = kp_ref[...] if blk == 0 else kc_ref[before, :]
        v_prev = vp_ref[...] if blk == 0 else vc_ref[before, :]
        k = jnp.concatenate([k_prev, kc_ref[rows, :]], axis=0).astype(BF16)
        v = jnp.concatenate([v_prev, vc_ref[rows, :]], axis=0).astype(BF16)
        for h in range(DIL_HEADS):
            sl = slice(h * DIL_HEAD_DIM, (h + 1) * DIL_HEAD_DIM)
            s = lax.dot_general(q[:, sl], k[:, sl], (((1,), (1,)), ((), ())),
                                preferred_element_type=F32) * scale
            s = jnp.where(mask, s, NEG_INF)
            m = jnp.max(s, axis=-1, keepdims=True)
            p = jnp.exp(s - m)
            den = jnp.sum(p, axis=-1, keepdims=True)
            o = jnp.dot(p.astype(BF16), v[:, sl], preferred_element_type=F32)
            o_ref[rows, sl] = o / den
            l_ref[rows, sl] = jnp.broadcast_to(m + jnp.log(den), (BLK, DIL_HEAD_DIM))


def _dilated_attention(qkv_view, gi, window, dilation):
    b, n, _ = qkv_view.shape
    d = dilation
    rows = DIL_Q_BLOCKS * BLK
    assert n % rows == 0
    cur = (None, rows, DIL_GROUP_W)
    prev = (None, BLK, DIL_GROUP_W)

    def col(which):
        return lambda bi, r, nb: (bi, nb, r * 3 + which)

    def col_prev(which):
        return lambda bi, r, nb: (bi, jnp.maximum(nb * DIL_Q_BLOCKS - 1, 0), r * 3 + which)

    out_spec = pl.BlockSpec(cur, lambda bi, r, nb: (bi, nb, r))
    return pl.pallas_call(
        functools.partial(_dil_attn_kernel, span=window // dilation),
        grid=(b, d, n // rows),
        in_specs=[pl.BlockSpec(cur, col(0)),
                  pl.BlockSpec(prev, col_prev(1)), pl.BlockSpec(cur, col(1)),
                  pl.BlockSpec(prev, col_prev(2)), pl.BlockSpec(cur, col(2))],
        out_specs=[out_spec, out_spec],
        out_shape=[jax.ShapeDtypeStruct((b, n, d * DIL_GROUP_W), F32)] * 2,
        compiler_params=_cparams("parallel", "parallel", "arbitrary"),
        name=f"dil_attn_g{gi}",
    )(qkv_view, qkv_view, qkv_view, qkv_view, qkv_view)


def _mix_merge_kernel(o0_ref, l0_ref, o1_ref, l1_ref, o2_ref, l2_ref, pa_ref, pg_ref, w_ref,
                      out_ref, scr_ref):
    tm, dm = pa_ref.shape
    slab = [0]

    def natural(ref, d):
        if d == 1:
            return [ref[:, c * LANES:(c + 1) * LANES] for c in range(LANE_CHUNKS)]
        chunks = []
        for c in range(LANE_CHUNKS):
            for r in range(d):
                scr_ref[slab[0], pl.ds(r, tm // d, stride=d), :] = (
                    ref[:, r * DIL_GROUP_W + c * LANES:r * DIL_GROUP_W + (c + 1) * LANES])
            chunks.append(scr_ref[slab[0]])
            slab[0] += 1
        return chunks

    dils = [d for _, d in DIL_GROUPS]
    o = [natural(r, d) for r, d in zip((o0_ref, o1_ref, o2_ref), dils)]
    l = [natural(r, d) for r, d in zip((l0_ref, l1_ref, l2_ref), dils)]
    yb = []
    for c in range(LANE_CHUNKS):
        m = jnp.maximum(jnp.maximum(l[0][c], l[1][c]), l[2][c])
        e = [jnp.exp(l[g][c] - m) for g in range(N_GROUPS)]
        den = e[0] + e[1] + e[2]
        yb.append((e[0] / den) * o[0][c] + (e[1] / den) * o[1][c] + (e[2] / den) * o[2][c])
    pb = jnp.dot(jnp.concatenate(yb, axis=-1).astype(BF16), w_ref[...], preferred_element_type=F32)
    out_ref[...] = (jax.nn.sigmoid(pg_ref[:, :dm]) * pa_ref[...]
                    + jax.nn.sigmoid(pg_ref[:, dm:]) * pb).astype(out_ref.dtype)


def _mix_merge(outs, lses, pa, p_gate, proj_b, bsz, s):
    n_tok, dm = pa.shape
    tm = min(ATTN_ROWS, s)
    per_b = s // tm
    n_strided = sum(2 * LANE_CHUNKS for _, d in DIL_GROUPS if d > 1)
    vspec = lambda d: pl.BlockSpec((None, tm // d, d * DIL_GROUP_W),
                                   lambda i: (i // per_b, i % per_b, 0))
    views = []
    for (_, d), o, l in zip(DIL_GROUPS, outs, lses):
        views += [(o, vspec(d)), (l, vspec(d))]
    row = lambda w: pl.BlockSpec((tm, w), lambda i: (i, 0))
    return pl.pallas_call(
        _mix_merge_kernel, grid=(n_tok // tm,),
        in_specs=[sp for _, sp in views] + [row(dm), row(2 * dm),
                                            pl.BlockSpec(proj_b.shape, lambda i: (0, 0))],
        out_specs=row(dm), out_shape=jax.ShapeDtypeStruct((n_tok, dm), BF16),
        scratch_shapes=[pltpu.VMEM((n_strided, tm, LANES), F32)],
        compiler_params=_cparams("parallel"), name="attn_mix_merge",
    )(*[a for a, _ in views], pa, p_gate, proj_b.astype(BF16))


W1_SPLIT_SUB = 256
SCAN_STEPS = 32
SCAN_SLAB = 16


def _split_even_odd(w_ref, g_ref, l_ref):
    sub, half = W1_SPLIT_SUB, W1_SPLIT_SUB // 2
    src = lax.broadcasted_iota(jnp.int32, (sub, sub), 0)
    dst = lax.broadcasted_iota(jnp.int32, (sub, sub), 1)
    want = jnp.where(dst < half, 2 * dst, 2 * (dst - half) + 1)
    sel = (src == want).astype(BF16)
    for q in range(w_ref.shape[1] // sub):
        w = w_ref[:, q * sub:(q + 1) * sub].astype(BF16)
        r = jnp.dot(w, sel, preferred_element_type=F32).astype(BF16)
        g_ref[:, q * half:(q + 1) * half] = r[:, :half]
        l_ref[:, q * half:(q + 1) * half] = r[:, half:]


def _rwkv_scan_kernel(r_ref, w_ref, k_ref, a_ref, b_ref, v_ref, w1_ref, w2_ref, y_ref, w1g_ref,
                      w1l_ref, w2b_ref, s_ref, *, tc):
    n = RWKV_HEAD

    @pl.when(pl.program_id(0) == 0)
    def _():
        s_ref[...] = jnp.zeros_like(s_ref)

    _split_even_odd(w1_ref, w1g_ref, w1l_ref)
    w2b_ref[...] = w2_ref[...].astype(BF16)

    slabs = n // SCAN_SLAB
    zero = jnp.zeros((SCAN_SLAB, s_ref.shape[2]), F32)

    def row(ref, tb, tt, j):
        return ref[tb, j, tt:tt + 1, :]

    def rows_of(slab):
        if isinstance(slab, int):
            return pl.ds(slab * SCAN_SLAB, SCAN_SLAB)
        return pl.ds(pl.multiple_of(slab * SCAN_SLAB, SCAN_SLAB), SCAN_SLAB)

    def state_times_a(tb, tt, slab):
        rows = rows_of(slab)
        acc = [zero, zero]
        for j in range(n):
            acc[j % 2] = acc[j % 2] + s_ref[j, rows, :] * row(a_ref, tb, tt, j)
        return acc[0] + acc[1]

    def update_and_read(tb, tt, slab, sa):
        rows = rows_of(slab)
        t = tb * SUBLANES + tt
        vt = v_ref[t, rows, :]
        yac = [zero, zero]
        for j in range(n):
            sj = (s_ref[j, rows, :] * row(w_ref, tb, tt, j) + sa * row(b_ref, tb, tt, j)
                  + vt * row(k_ref, tb, tt, j))
            s_ref[j, rows, :] = sj
            yac[j % 2] = yac[j % 2] + sj * row(r_ref, tb, tt, j)
        y_ref[t, rows, :] = yac[0] + yac[1]

    nblk = tc // SUBLANES

    def block(tb, sa):
        for tt in range(SUBLANES):
            def trip(slab, sa, tt=tt):
                update_and_read(tb, tt, slab, sa)
                return state_times_a(tb, tt, slab + 1)

            sa = lax.fori_loop(0, slabs - 1, trip, sa)
            update_and_read(tb, tt, slabs - 1, sa)
            if tt + 1 < SUBLANES:
                sa = state_times_a(tb, tt + 1, 0)
            else:
                sa = state_times_a(jnp.minimum(tb + 1, nblk - 1), 0, 0)
        return sa

    lax.fori_loop(0, nblk, block, state_times_a(0, 0, 0))


def _rwkv_scan_and_moe_weight_prep(r, w, k, a, b, v, w1, w2):
    s, n, l = v.shape
    tc = min(SCAN_STEPS, s)
    assert s % tc == 0
    steps = s // tc
    assert w1.shape[-1] % W1_SPLIT_SUB == 0

    def sliced(wgt, cols_out):
        e, rows, cols = wgt.shape
        assert (e * rows) % steps == 0
        per_step = e * rows // steps
        assert rows % per_step == 0 and per_step % (2 * SUBLANES) == 0
        per_e = rows // per_step
        spec = lambda c: pl.BlockSpec((None, per_step, c), lambda i: (i // per_e, i % per_e, 0))
        return spec(cols), spec(cols_out), jax.ShapeDtypeStruct((e, rows, cols_out), BF16)

    w1_in, w1_out, w1_shape = sliced(w1, w1.shape[-1] // 2)
    w2_in, w2_out, w2_shape = sliced(w2, w2.shape[-1])
    rows = pl.BlockSpec((tc // SUBLANES, n, SUBLANES, l), lambda i: (i, 0, 0, 0))
    blk = pl.BlockSpec((tc, n, l), lambda i: (i, 0, 0))
    return pl.pallas_call(
        functools.partial(_rwkv_scan_kernel, tc=tc),
        grid=(steps,), in_specs=[rows] * 5 + [blk, w1_in, w2_in],
        out_specs=[blk, w1_out, w1_out, w2_out],
        out_shape=[jax.ShapeDtypeStruct((s, n, l), F32), w1_shape, w1_shape, w2_shape],
        scratch_shapes=[pltpu.VMEM((n, n, l), F32)],
        compiler_params=_cparams("arbitrary"), name="rwkv_scan",
    )(r, w, k, a, b, v, w1, w2)


RWKV_PREP_ROWS = 32
SUBLANES = 8


def _head_sum(x):
    cols = x.shape[-1] // LANES
    acc = x[:, :LANES]
    for c in range(1, cols):
        acc = acc + x[:, c * LANES:(c + 1) * LANES]
    acc = acc + pltpu.roll(acc, RWKV_HEADS, 1)
    acc = acc + pltpu.roll(acc, 2 * RWKV_HEADS, 1)
    return jnp.concatenate([acc] * cols, axis=-1)


def _token_shift_rows(p, last_prev_row, mix, first):
    rolled = pltpu.roll(p, 1, 0)
    row0 = jnp.where(first, jnp.zeros_like(last_prev_row), last_prev_row)
    t = lax.broadcasted_iota(jnp.int32, p.shape, 0)
    prev = jnp.where(t == 0, row0, rolled)
    return p + (prev - p) * mix


def _rwkv_prep_kernel(rkv_ref, rkvp_ref, lo_ref, lop_ref, mixr_ref, mixl_ref, w0_ref, a0_ref,
                      kk_ref, ka_ref, rk_ref, wup_ref, aup_ref, gup_ref,
                      r_ref, w_ref, k_ref, al_ref, be_ref, v_ref, g_ref, bo_ref, tiles_ref):
    first = pl.program_id(0) == 0
    d = D_MODEL
    nb, ts = rkv_ref.shape[0], rkv_ref.shape[1]
    mm = lambda a, w_ref_: jnp.dot(a.astype(BF16), w_ref_[...].astype(BF16),
                                   preferred_element_type=F32)
    for bi in range(nb):
        def seg(c):
            sl = slice(c * d, (c + 1) * d)
            return _token_shift_rows(rkv_ref[bi, :, sl], rkvp_ref[bi, SUBLANES - 1:SUBLANES, sl],
                                     mixr_ref[:, sl], first)

        r, k, v = seg(0), seg(1), seg(2)
        zl = _token_shift_rows(lo_ref[bi], lop_ref[bi, SUBLANES - 1:SUBLANES, :], mixl_ref[...],
                               first)
        wd = zl[:, :DECAY_LORA]
        ad = zl[:, DECAY_LORA:DECAY_LORA + AAA_LORA]
        gd = zl[:, DECAY_LORA + AAA_LORA:]
        z = -(w0_ref[...] + mm(jnp.tanh(wd), wup_ref))
        softplus = jnp.maximum(z, 0.0) + jnp.log1p(jnp.exp(-jnp.abs(z)))
        w = -softplus - 0.5
        a = jax.nn.sigmoid(a0_ref[...] + mm(ad, aup_ref))
        kk = k * kk_ref[...]
        kk = kk / jnp.maximum(jnp.sqrt(_head_sum(kk * kk)), 1e-12)
        k2 = k * (1.0 + (a - 1.0) * ka_ref[...])
        for op, val in enumerate((r, jnp.exp(-jnp.exp(w)), k2, -kk, kk * a)):
            for c in range(d // LANES):
                col = val[:, c * LANES:(c + 1) * LANES]
                tiles_ref[op, bi, :, c * LANES:(c + 1) * LANES] = (
                    col if bi == 0 else pltpu.roll(col, bi * RWKV_HEADS, 1))
        v_ref[bi] = v
        g_ref[bi] = mm(jax.nn.sigmoid(gd), gup_ref)
        bo_ref[bi] = _head_sum(r * k2 * rk_ref[...]) * v

    lane_group = lax.broadcasted_iota(jnp.int32, (ts, LANES), 1) // RWKV_HEADS
    for op, out_ref in enumerate((r_ref, w_ref, k_ref, al_ref, be_ref)):
        for c in range(d // LANES):
            cols = [tiles_ref[op, bi, :, c * LANES:(c + 1) * LANES] for bi in range(nb)]
            for jm in range(nb):
                sel = cols[(0 - jm) % nb]
                for g in range(1, nb):
                    sel = jnp.where(lane_group == g, cols[(g - jm) % nb], sel)
                out = sel if jm == 0 else pltpu.roll(sel, (nb - jm) * RWKV_HEADS, 1)
                for tb in range(ts // SUBLANES):
                    out_ref[tb, c * nb + jm] = out[tb * SUBLANES:(tb + 1) * SUBLANES]


def _rwkv_prep(p_rkv, p_lora, shift_mix, w0, w_up, a0, a_up, g_up, k_k, k_a, r_k):
    b, s, c_rkv = p_rkv.shape
    d, ts, c_lo = D_MODEL, min(RWKV_PREP_ROWS, s), p_lora.shape[-1]
    assert s % ts == 0 and ts % SUBLANES == 0
    assert b * RWKV_HEADS == LANES, "the scan packs exactly (batch, head) onto the 128 lanes"
    cur = lambda c: pl.BlockSpec((b, ts, c), lambda i: (0, i, 0))
    prev = lambda c: pl.BlockSpec(
        (b, SUBLANES, c), lambda i: (0, jnp.maximum(i * (ts // SUBLANES) - 1, 0), 0))
    vec = lambda c: pl.BlockSpec((1, c), lambda i: (0, 0))
    full = lambda a: pl.BlockSpec(a.shape, lambda i: (0, 0))
    rows = pl.BlockSpec((ts // SUBLANES, RWKV_HEAD, SUBLANES, LANES), lambda i: (i, 0, 0, 0))
    rows_shape = jax.ShapeDtypeStruct((s // SUBLANES, RWKV_HEAD, SUBLANES, LANES), F32)
    tok_shape = jax.ShapeDtypeStruct((b, s, d), F32)
    row = lambda t: t.reshape(1, -1)
    return pl.pallas_call(
        _rwkv_prep_kernel, grid=(s // ts,),
        in_specs=[cur(c_rkv), prev(c_rkv), cur(c_lo), prev(c_lo), vec(c_rkv), vec(c_lo),
                  vec(d), vec(d), vec(d), vec(d), vec(d), full(w_up), full(a_up), full(g_up)],
        out_specs=[rows] * 5 + [cur(d)] * 3, out_shape=[rows_shape] * 5 + [tok_shape] * 3,
        scratch_shapes=[pltpu.VMEM((5, b, ts, d), F32)],
        compiler_params=_cparams("arbitrary"), name="rwkv_prep",
    )(p_rkv, p_rkv, p_lora, p_lora, row(shift_mix[:c_rkv]), row(shift_mix[c_rkv:]), row(w0),
      row(a0), row(k_k), row(k_a), row(r_k), w_up, a_up, g_up)


def _proj_a_kernel(y_ref, bo_ref, g_ref, lg_ref, lb_ref, w_ref, o_ref):
    y = y_ref[...]
    inv_n = 1.0 / RWKV_HEAD
    c = y - _head_sum(y) * inv_n
    var = _head_sum(c * c) * inv_n
    ya = (c * lax.rsqrt(var + GN_EPS) * lg_ref[...] + lb_ref[...] + bo_ref[...]) * g_ref[...]
    o_ref[...] = jnp.dot(ya.astype(BF16), w_ref[...], preferred_element_type=F32)


def _rwkv_post_proj(y, bonus, gate, lnx_g, lnx_b, proj_a, *, tm=256):
    m, d = y.shape
    assert m % tm == 0
    rowb = pl.BlockSpec((tm, d), lambda i: (i, 0))
    vec = pl.BlockSpec((1, d), lambda i: (0, 0))
    return pl.pallas_call(
        _proj_a_kernel, grid=(m // tm,),
        in_specs=[rowb, rowb, rowb, vec, vec, pl.BlockSpec((d, d), lambda i: (0, 0))],
        out_specs=rowb, out_shape=jax.ShapeDtypeStruct((m, d), F32),
        compiler_params=_cparams("parallel"), name="rwkv_post_proj_a",
    )(y, bonus, gate, lnx_g.reshape(1, d), lnx_b.reshape(1, d), proj_a.astype(BF16))


def _cross_attn_kernel(q_ref, k_ref, v_ref, o_ref):
    scale = CA_HEAD_DIM ** -0.5
    for h in range(CA_HEADS):
        sl = slice(h * CA_HEAD_DIM, (h + 1) * CA_HEAD_DIM)
        q = q_ref[:, sl].astype(BF16)
        k = k_ref[:, sl].astype(BF16)
        s = lax.dot_general(q, k, (((1,), (1,)), ((), ())), preferred_element_type=F32) * scale
        m = jnp.max(s, axis=-1, keepdims=True)
        p = jnp.exp(s - m)
        den = jnp.sum(p, axis=-1, keepdims=True)
        o = jnp.dot(p.astype(BF16), v_ref[:, sl].astype(BF16), preferred_element_type=F32)
        o_ref[:, sl] = (o / den).astype(o_ref.dtype)


def _cross_attention(q, kv, *, tq=512):
    b, s, d = q.shape
    mlen = kv.shape[1]
    tq = min(tq, s)
    return pl.pallas_call(
        _cross_attn_kernel, grid=(b, s // tq),
        in_specs=[pl.BlockSpec((None, tq, d), lambda i, j: (i, j, 0)),
                  pl.BlockSpec((None, mlen, d), lambda i, j: (i, 0, 0)),
                  pl.BlockSpec((None, mlen, d), lambda i, j: (i, 0, 1))],
        out_specs=pl.BlockSpec((None, tq, d), lambda i, j: (i, j, 0)),
        out_shape=jax.ShapeDtypeStruct((b, s, d), BF16),
        compiler_params=_cparams("parallel", "parallel"), name="cross_attn",
    )(q, kv, kv)


def _router_kernel(x_ref, w_ref, b_ref, o_ref):
    o_ref[...] = jnp.dot(x_ref[...], w_ref[...], precision=lax.Precision.HIGHEST,
                         preferred_element_type=F32) + b_ref[...]


def _router_logits(x, w, b, *, tm=512):
    m, d = x.shape
    e = w.shape[1]
    return pl.pallas_call(
        _router_kernel, grid=(m // tm,),
        in_specs=[pl.BlockSpec((tm, d), lambda i: (i, 0)),
                  pl.BlockSpec((d, e), lambda i: (0, 0)),
                  pl.BlockSpec((1, e), lambda i: (0, 0))],
        out_specs=pl.BlockSpec((tm, e), lambda i: (i, 0)),
        out_shape=jax.ShapeDtypeStruct((m, e), F32),
        compiler_params=_cparams("parallel"), name="router",
    )(x, w, b.reshape(1, e))


def _moe_kernel(be_ref, nused_ref, x_ref, g_ref, w1g_ref, w1l_ref, b1g_ref, b1l_ref, w2_ref,
                b2_ref, o_ref):
    i = pl.program_id(0)

    @pl.when(i < nused_ref[0])
    def _():
        x = x_ref[...]
        glu = jnp.dot(x, w1g_ref[...], preferred_element_type=F32) + b1g_ref[...]
        lin = jnp.dot(x, w1l_ref[...], preferred_element_type=F32) + b1l_ref[...]
        glu = jnp.minimum(glu, SWIGLU_LIMIT)
        lin = jnp.clip(lin, -SWIGLU_LIMIT, SWIGLU_LIMIT)
        act = glu * jax.nn.sigmoid(SWIGLU_ALPHA * glu) * (lin + 1.0)
        y = jnp.dot(act.astype(BF16), w2_ref[...], preferred_element_type=F32) + b2_ref[...]
        o_ref[...] = y * g_ref[...]

    @pl.when(i >= nused_ref[0])
    def _():
        o_ref[...] = jnp.zeros_like(o_ref)


def _moe_experts(blk_exp, n_used, xs, row_gate, w1g, w1l, b1g, b1l, w2, b2):
    rows, d = xs.shape
    nblk = rows // MOE_BLK
    f = w1g.shape[-1]
    wspec = lambda shape: pl.BlockSpec(shape, lambda i, be, nu: (be[i], 0, 0))
    grid_spec = pltpu.PrefetchScalarGridSpec(
        num_scalar_prefetch=2, grid=(nblk,),
        in_specs=[pl.BlockSpec((MOE_BLK, d), lambda i, be, nu: (i, 0)),
                  pl.BlockSpec((MOE_BLK, 1), lambda i, be, nu: (i, 0)),
                  wspec((None, d, f)), wspec((None, d, f)),
                  wspec((None, 1, f)), wspec((None, 1, f)),
                  wspec((None, f, d)), wspec((None, 1, d))],
        out_specs=pl.BlockSpec((MOE_BLK, d), lambda i, be, nu: (i, 0)),
    )
    return pl.pallas_call(
        _moe_kernel, grid_spec=grid_spec,
        out_shape=jax.ShapeDtypeStruct((rows, d), F32),
        compiler_params=pltpu.CompilerParams(dimension_semantics=("arbitrary",),
                                             vmem_limit_bytes=V7X_VMEM_LIMIT_MOE_BYTES),
        name="moe_experts",
    )(blk_exp, n_used, xs, row_gate.reshape(rows, 1), w1g, w1l, b1g, b1l, w2, b2)


def _per_expert(table, idx):
    hit = idx[:, None] == jnp.arange(N_EXPERTS, dtype=idx.dtype)[None, :]
    return jnp.sum(jnp.where(hit, table[None, :], 0), axis=1)


def _moe_ffn(x_bf, logits, moe_weights, b1, b2):
    n, d = x_bf.shape
    top_val, top_idx = lax.top_k(logits, TOP_K)
    top_w = jax.nn.softmax(top_val, axis=-1)
    flat_e = top_idx.reshape(-1).astype(jnp.int32)
    order = jnp.argsort(flat_e).astype(jnp.int32)
    rank = jnp.argsort(order).astype(jnp.int32)
    counts = jnp.sum(flat_e[:, None] == jnp.arange(N_EXPERTS, dtype=jnp.int32)[None, :],
                     axis=0, dtype=jnp.int32)
    starts = jnp.cumsum(counts) - counts
    padded = (counts + MOE_BLK - 1) // MOE_BLK * MOE_BLK
    pends = jnp.cumsum(padded)
    pstarts = pends - padded
    rows = n * TOP_K + N_EXPERTS * MOE_BLK
    nblk = rows // MOE_BLK
    blk_start = jnp.arange(nblk, dtype=jnp.int32) * MOE_BLK
    blk_exp = jnp.minimum(jnp.sum(pends[None, :] <= blk_start[:, None], axis=1, dtype=jnp.int32),
                          N_EXPERTS - 1)
    n_used = (pends[-1] // MOE_BLK).astype(jnp.int32).reshape(1)
    row_exp = jnp.repeat(blk_exp, MOE_BLK)
    off = jnp.arange(rows, dtype=jnp.int32) - _per_expert(pstarts, row_exp)
    valid = off < _per_expert(counts, row_exp)
    assign = order[jnp.where(valid, _per_expert(starts, row_exp) + off, 0)]
    row_tok = jnp.where(valid, assign // TOP_K, 0)
    row_gate = jnp.where(valid, top_w.reshape(-1)[assign], 0.0)
    xs = x_bf[row_tok]
    b1g = b1[:, None, 0::2]
    b1l = b1[:, None, 1::2]
    w1g, w1l, w2 = moe_weights
    ys = _moe_experts(blk_exp, n_used, xs, row_gate, w1g, w1l, b1g, b1l, w2,
                      b2[:, None, :])
    pos = (_per_expert(pstarts, flat_e) + rank - _per_expert(starts, flat_e)).reshape(n, TOP_K)
    return [ys[pos[:, kk]] for kk in range(TOP_K)]


def _heads_minor(p):
    lead = p.shape[:-1]
    return p.reshape(lead + (RWKV_HEADS, RWKV_HEAD)).swapaxes(-1, -2).reshape(p.shape)


def _to_scan_layout(t, bsz, s):
    return t.reshape(bsz, s, RWKV_HEAD, RWKV_HEADS).transpose(1, 2, 0, 3).reshape(
        s, RWKV_HEAD, bsz * RWKV_HEADS)


def _from_scan_layout(t, bsz, s):
    return t.reshape(s, RWKV_HEAD, bsz, RWKV_HEADS).transpose(2, 0, 1, 3).reshape(bsz, s, D_MODEL)


def _rwkv7_branch_proj(p_rkv, p_lora, shift_mix, w0, w_up, a0, a_up, g_up, k_k, k_a, r_k, lnx_g,
                       lnx_b, proj_a, moe_w1, moe_w2):
    bsz, s, c_rkv = p_rkv.shape
    hm = _heads_minor
    mix = jnp.concatenate([hm(shift_mix[:c_rkv].reshape(3, D_MODEL)).reshape(-1),
                           shift_mix[c_rkv:]])
    r, decay, k, alpha, beta, v, gate, bonus = _rwkv_prep(
        p_rkv, p_lora, mix, hm(w0), hm(w_up), hm(a0), hm(a_up), hm(g_up), hm(k_k), hm(k_a),
        hm(r_k.reshape(-1)))
    y, *moe_weights = _rwkv_scan_and_moe_weight_prep(r, decay, k, alpha, beta,
                                                     _to_scan_layout(v, bsz, s), moe_w1, moe_w2)
    y = _from_scan_layout(y, bsz, s)
    flat = lambda t: t.reshape(bsz * s, D_MODEL)
    proj_a_hm = proj_a.reshape(RWKV_HEADS, RWKV_HEAD, -1).swapaxes(0, 1).reshape(proj_a.shape)
    pa = _rwkv_post_proj(flat(y), flat(bonus), flat(gate), hm(lnx_g), hm(lnx_b), proj_a_hm)
    return pa, moe_weights


def _hybrid_mixer(x, w_in, shift_mix, w0, w_up, a0, a_up, g_up, k_k, k_a, r_k, lnx_g, lnx_b,
                  proj_a, proj_b, moe_w1, moe_w2):
    bsz, s, d = x.shape
    n_tok = bsz * s
    x_bf = x.astype(BF16)
    x2d = x_bf.reshape(n_tok, d)
    c0, c1, c2 = 3 * D_MODEL, RWKV_COLS, RWKV_COLS + ATTN_COLS
    w_rkv = _heads_minor(w_in[:, :c0].reshape(d, 3, D_MODEL)).reshape(d, c0)
    p_rkv = _matmul(x2d, w_rkv, tm=1024, tn=1024, name="in_rkv").reshape(bsz, s, -1)
    p_lora = _matmul(x2d, w_in[:, c0:c1], tm=1024, tn=512, name="in_lora").reshape(bsz, s, -1)
    p_gate = _matmul(x2d, w_in[:, c2:], tm=1024, tn=1024, name="in_gate")

    pa, moe_weights = _rwkv7_branch_proj(p_rkv, p_lora, shift_mix, w0, w_up, a0, a_up, g_up, k_k,
                                         k_a, r_k, lnx_g, lnx_b, proj_a, moe_w1, moe_w2)

    cos_t, sin_t = _rotary_tables(s)
    views = _attn_in_proj(x_bf, w_in[:, c1:c2], cos_t, sin_t)
    outs, lses = [], []
    for gi, (window, dilation) in enumerate(DIL_GROUPS):
        o, l = _dilated_attention(views[gi], gi, window, dilation)
        outs.append(o)
        lses.append(l)
    return _mix_merge(outs, lses, pa, p_gate, proj_b, bsz, s), moe_weights


def _memory_cross_attention(x_bf, mem, wq, wkv, bsz, s):
    d = x_bf.shape[-1]
    q = _matmul(x_bf, wq, tm=1024, tn=1024, name="ca_q").reshape(bsz, s, d)
    kv = _matmul(mem.reshape(-1, d), wkv, tm=512, tn=1024, name="ca_kv").reshape(bsz, -1, 2 * d)
    return _cross_attention(q, kv).reshape(-1, d)


def kernel(x, mem, w_in, shift_mix, w0, w_up, a0, a_up, g_up, k_k, k_a, r_k, lnx_g, lnx_b, proj_a, proj_b, w_out, ln1_g, ln1_b, ca_wq, ca_wkv, ca_wo, ln2_g, ln2_b, router_w, router_b, moe_w1, moe_b1, moe_w2, moe_b2, ln3_g, ln3_b):
    bsz, s, d = x.shape
    n_tok = bsz * s
    for l in range(DEPTH):
        merged, moe_weights = _hybrid_mixer(x, w_in[l], shift_mix[l], w0[l], w_up[l], a0[l],
                                            a_up[l], g_up[l], k_k[l], k_a[l], r_k[l], lnx_g[l],
                                            lnx_b[l], proj_a[l], proj_b[l], moe_w1[l], moe_w2[l])
        x1, x1_bf = _matmul_res_ln(merged, w_out[l], x.reshape(n_tok, d), ln1_g[l], ln1_b[l],
                                   name="w_out_ln1")
        o = _memory_cross_attention(x1_bf, mem, ca_wq[l], ca_wkv[l], bsz, s)
        x2, x2_bf = _matmul_res_ln(o, ca_wo[l], x1, ln2_g[l], ln2_b[l], name="ca_o_ln2")
        logits = _router_logits(x2, router_w[l], router_b[l])
        hs = _moe_ffn(x2_bf, logits, moe_weights, moe_b1[l], moe_b2[l])
        x = _res_layer_norm(x2, hs, ln3_g[l], ln3_b[l], name="ln3").reshape(bsz, s, d)
    return x
```

```python
import functools

import jax
import jax.numpy as jnp
from jax import lax
from jax.experimental import pallas as pl
from jax.experimental.pallas import tpu as pltpu

F32 = jnp.float32
BF16 = jnp.bfloat16

D_MODEL = 2048
RWKV_HEAD = 64
RWKV_HEADS = D_MODEL // RWKV_HEAD
DECAY_LORA = 96
AAA_LORA = 96
GATE_LORA = 256
GN_EPS = 64e-5
DIL_GROUPS = ((128, 1), (512, 4), (2048, 16))
N_GROUPS = len(DIL_GROUPS)
DIL_HEADS = 8
DIL_HEAD_DIM = 64
DIL_GROUP_W = DIL_HEADS * DIL_HEAD_DIM
DIL_DIM = N_GROUPS * DIL_GROUP_W
BLK = 128
ROPE_THETA = 10000.0
NEG_INF = -1e30
RWKV_COLS = 3 * D_MODEL + DECAY_LORA + AAA_LORA + GATE_LORA
ATTN_COLS = 3 * DIL_DIM
CA_HEADS = 4
CA_HEAD_DIM = D_MODEL // CA_HEADS
N_EXPERTS = 32
TOP_K = 4
D_FF = D_MODEL
SWIGLU_LIMIT = 7.0
SWIGLU_ALPHA = 1.702
MOE_BLK = 128
LN_EPS = 1e-5
DEPTH = 1
DEEPNORM_ALPHA = (2 * DEPTH) ** 0.25

V7X_VMEM_LIMIT_BYTES = 56 * 1024 * 1024
V7X_VMEM_LIMIT_MOE_BYTES = 61 * 1024 * 1024


def _cparams(*sem):
    return pltpu.CompilerParams(dimension_semantics=sem, vmem_limit_bytes=V7X_VMEM_LIMIT_BYTES)


def _mm_kernel(a_ref, b_ref, o_ref):
    o_ref[...] = jnp.dot(a_ref[...].astype(BF16), b_ref[...].astype(BF16),
                         preferred_element_type=F32).astype(o_ref.dtype)


def _matmul(a, b, *, tm, tn, name, out_dtype=F32):
    m, k = a.shape
    _, n = b.shape
    tm, tn = min(tm, m), min(tn, n)
    assert m % tm == 0 and n % tn == 0, (m, n, tm, tn)
    return pl.pallas_call(
        _mm_kernel,
        grid=(n // tn, m // tm),
        in_specs=[pl.BlockSpec((tm, k), lambda j, i: (i, 0)),
                  pl.BlockSpec((k, tn), lambda j, i: (0, j))],
        out_specs=pl.BlockSpec((tm, tn), lambda j, i: (i, j)),
        out_shape=jax.ShapeDtypeStruct((m, n), out_dtype),
        compiler_params=_cparams("parallel", "parallel"),
        name=name,
    )(a, b)


def _ln_kernel(x_ref, *rest):
    *h_refs, g_ref, b_ref, o_ref = rest
    t = DEEPNORM_ALPHA * x_ref[...]
    for h_ref in h_refs:
        t = t + h_ref[...]
    mu = jnp.mean(t, axis=-1, keepdims=True)
    c = t - mu
    var = jnp.mean(c * c, axis=-1, keepdims=True)
    o_ref[...] = c * lax.rsqrt(var + LN_EPS) * g_ref[...] + b_ref[...]


def _res_layer_norm(x, hs, g, b, *, name, tm=256):
    m, d = x.shape
    assert m % tm == 0
    row = pl.BlockSpec((tm, d), lambda i: (i, 0))
    vec = pl.BlockSpec((1, d), lambda i: (0, 0))
    return pl.pallas_call(
        _ln_kernel, grid=(m // tm,), in_specs=[row] * (1 + len(hs)) + [vec, vec], out_specs=row,
        out_shape=jax.ShapeDtypeStruct((m, d), F32),
        compiler_params=_cparams("parallel"), name=name,
    )(x, *hs, g.reshape(1, d), b.reshape(1, d))


def _mm_res_ln_kernel(a_ref, w_ref, x_ref, g_ref, b_ref, o_ref, ob_ref):
    h = jnp.dot(a_ref[...].astype(BF16), w_ref[...], preferred_element_type=F32)
    t = DEEPNORM_ALPHA * x_ref[...] + h
    mu = jnp.mean(t, axis=-1, keepdims=True)
    c = t - mu
    var = jnp.mean(c * c, axis=-1, keepdims=True)
    y = c * lax.rsqrt(var + LN_EPS) * g_ref[...] + b_ref[...]
    o_ref[...] = y
    ob_ref[...] = y.astype(BF16)


def _matmul_res_ln(a, w, x, g, b, *, name, tm=512):
    m, k = a.shape
    d = w.shape[1]
    assert m % tm == 0
    row = lambda c: pl.BlockSpec((tm, c), lambda i: (i, 0))
    const = lambda r, c: pl.BlockSpec((r, c), lambda i: (0, 0))
    in_specs = [row(k), const(k, d), row(d), const(1, d), const(1, d)]
    args = [a, w.astype(BF16), x, g.reshape(1, d), b.reshape(1, d)]
    out_specs = [row(d), row(d)]
    out_shape = [jax.ShapeDtypeStruct((m, d), F32), jax.ShapeDtypeStruct((m, d), BF16)]
    return pl.pallas_call(
        _mm_res_ln_kernel,
        grid=(m // tm,), in_specs=in_specs, out_specs=out_specs, out_shape=out_shape,
        compiler_params=_cparams("parallel"), name=name,
    )(*args)


LANES = 128
ATTN_ROWS = 256
GROUP_QKV_W = 3 * DIL_GROUP_W
LANE_CHUNKS = DIL_GROUP_W // LANES


def _rotary_tables(s):
    half = DIL_HEAD_DIM // 2
    inv = ROPE_THETA ** (-jnp.arange(half, dtype=F32) * 2.0 / DIL_HEAD_DIM)
    ang = jnp.arange(s, dtype=F32)[:, None] * inv[None, :]
    cos, sin = jnp.cos(ang), jnp.sin(ang)
    return (jnp.concatenate([cos, cos, cos, cos], axis=-1),
            jnp.concatenate([-sin, sin, -sin, sin], axis=-1))


def _attn_in_kernel(x_ref, w_ref, cos_ref, sin_ref, o0_ref, o1_ref, o2_ref, scr_ref):
    tm = x_ref.shape[0]
    acc = jnp.dot(x_ref[...].astype(BF16), w_ref[...], preferred_element_type=F32)
    cos, sin = cos_ref[...], sin_ref[...]
    lane = lax.broadcasted_iota(jnp.int32, (tm, LANES), 1)
    first_half = (lane % DIL_HEAD_DIM) < (DIL_HEAD_DIM // 2)
    outs = (o0_ref, o1_ref, o2_ref)
    slab = 0
    for part in range(3):
        for gi, (_, d) in enumerate(DIL_GROUPS):
            for c in range(LANE_CHUNKS):
                col = part * DIL_DIM + gi * DIL_GROUP_W + c * LANES
                x = acc[:, col:col + LANES]
                if part < 2:
                    partner = jnp.where(first_half, pltpu.roll(x, LANES - 32, 1),
                                        pltpu.roll(x, 32, 1))
                    x = x * cos + partner * sin
                dst = part * DIL_GROUP_W + c * LANES
                if d == 1:
                    outs[gi][:, dst:dst + LANES] = x
                else:
                    scr_ref[slab] = x
                    for r in range(d):
                        outs[gi][:, r * GROUP_QKV_W + dst:r * GROUP_QKV_W + dst + LANES] = (
                            scr_ref[slab, pl.ds(r, tm // d, stride=d), :])
                    slab += 1


def _attn_in_proj(x, w_attn, cos_t, sin_t):
    b, s, dm = x.shape
    tm = min(ATTN_ROWS, s)
    dmax = max(d for _, d in DIL_GROUPS)
    assert s % tm == 0 and tm % (dmax * 8) == 0
    n_strided = sum(3 * LANE_CHUNKS for _, d in DIL_GROUPS if d > 1)
    tab = pl.BlockSpec((tm, LANES), lambda i, j: (j, 0))
    ospec = lambda d: pl.BlockSpec((None, tm // d, d * GROUP_QKV_W), lambda i, j: (i, j, 0))
    return pl.pallas_call(
        _attn_in_kernel, grid=(b, s // tm),
        in_specs=[pl.BlockSpec((None, tm, dm), lambda i, j: (i, j, 0)),
                  pl.BlockSpec(w_attn.shape, lambda i, j: (0, 0), pipeline_mode=pl.Buffered(1)),
                  tab, tab],
        out_specs=[ospec(d) for _, d in DIL_GROUPS],
        out_shape=[jax.ShapeDtypeStruct((b, s // d, d * GROUP_QKV_W), F32) for _, d in DIL_GROUPS],
        scratch_shapes=[pltpu.VMEM((n_strided, tm, LANES), F32)],
        compiler_params=_cparams("parallel", "parallel"), name="in_attn_rope",
    )(x, w_attn.astype(BF16), cos_t, sin_t)


DIL_Q_BLOCKS = 2


def _dil_attn_kernel(q_ref, kp_ref, kc_ref, vp_ref, vc_ref, o_ref, l_ref, *, span):
    step = pl.program_id(2)
    qi = lax.broadcasted_iota(jnp.int32, (BLK, 2 * BLK), 0) + BLK
    ki = lax.broadcasted_iota(jnp.int32, (BLK, 2 * BLK), 1)
    dist = qi - ki
    band = (dist >= 0) & (dist <= span)
    scale = DIL_HEAD_DIM ** -0.5
    for blk in range(DIL_Q_BLOCKS):
        rows = slice(blk * BLK, (blk + 1) * BLK)
        before = slice((blk - 1) * BLK, blk * BLK)
        mask = band & ((step > 0) | (ki >= BLK)) if blk == 0 else band
        q = q_ref[rows, :].astype(BF16)
        k_prev = kp_ref[...] if blk == 0 else kc_ref[before, :]
        v_prev = vp_ref[...] if blk == 0 else vc_ref[before, :]
        k = jnp.concatenate([k_prev, kc_ref[rows, :]], axis=0).astype(BF16)
        v = jnp.concatenate([v_prev, vc_ref[rows, :]], axis=0).astype(BF16)
        for h in range(DIL_HEADS):
            sl = slice(h * DIL_HEAD_DIM, (h + 1) * DIL_HEAD_DIM)
            s = lax.dot_general(q[:, sl], k[:, sl], (((1,), (1,)), ((), ())),
                                preferred_element_type=F32) * scale
            s = jnp.where(mask, s, NEG_INF)
            m = jnp.max(s, axis=-1, keepdims=True)
            p = jnp.exp(s - m)
            den = jnp.sum(p, axis=-1, keepdims=True)
            o = jnp.dot(p.astype(BF16), v[:, sl], preferred_element_type=F32)
            o_ref[rows, sl] = o / den
            l_ref[rows, sl] = jnp.broadcast_to(m + jnp.log(den), (BLK, DIL_HEAD_DIM))


def _dilated_attention(qkv_view, gi, window, dilation):
    b, n, _ = qkv_view.shape
    d = dilation
    rows = DIL_Q_BLOCKS * BLK
    assert n % rows == 0
    cur = (None, rows, DIL_GROUP_W)
    prev = (None, BLK, DIL_GROUP_W)

    def col(which):
        return lambda bi, r, nb: (bi, nb, r * 3 + which)

    def col_prev(which):
        return lambda bi, r, nb: (bi, jnp.maximum(nb * DIL_Q_BLOCKS - 1, 0), r * 3 + which)

    out_spec = pl.BlockSpec(cur, lambda bi, r, nb: (bi, nb, r))
    return pl.pallas_call(
        functools.partial(_dil_attn_kernel, span=window // dilation),
        grid=(b, d, n // rows),
        in_specs=[pl.BlockSpec(cur, col(0)),
                  pl.BlockSpec(prev, col_prev(1)), pl.BlockSpec(cur, col(1)),
                  pl.BlockSpec(prev, col_prev(2)), pl.BlockSpec(cur, col(2))],
        out_specs=[out_spec, out_spec],
        out_shape=[jax.ShapeDtypeStruct((b, n, d * DIL_GROUP_W), F32)] * 2,
        compiler_params=_cparams("parallel", "parallel", "arbitrary"),
        name=f"dil_attn_g{gi}",
    )(qkv_view, qkv_view, qkv_view, qkv_view, qkv_view)


def _mix_merge_kernel(o0_ref, l0_ref, o1_ref, l1_ref, o2_ref, l2_ref, pa_ref, pg_ref, w_ref,
                      out_ref, scr_ref):
    tm, dm = pa_ref.shape
    slab = [0]

    def natural(ref, d):
        if d == 1:
            return [ref[:, c * LANES:(c + 1) * LANES] for c in range(LANE_CHUNKS)]
        chunks = []
        for c in range(LANE_CHUNKS):
            for r in range(d):
                scr_ref[slab[0], pl.ds(r, tm // d, stride=d), :] = (
                    ref[:, r * DIL_GROUP_W + c * LANES:r * DIL_GROUP_W + (c + 1) * LANES])
            chunks.append(scr_ref[slab[0]])
            slab[0] += 1
        return chunks

    dils = [d for _, d in DIL_GROUPS]
    o = [natural(r, d) for r, d in zip((o0_ref, o1_ref, o2_ref), dils)]
    l = [natural(r, d) for r, d in zip((l0_ref, l1_ref, l2_ref), dils)]
    yb = []
    for c in range(LANE_CHUNKS):
        m = jnp.maximum(jnp.maximum(l[0][c], l[1][c]), l[2][c])
        e = [jnp.exp(l[g][c] - m) for g in range(N_GROUPS)]
        den = e[0] + e[1] + e[2]
        yb.append((e[0] / den) * o[0][c] + (e[1] / den) * o[1][c] + (e[2] / den) * o[2][c])
    pb = jnp.dot(jnp.concatenate(yb, axis=-1).astype(BF16), w_ref[...], preferred_element_type=F32)
    out_ref[...] = (jax.nn.sigmoid(pg_ref[:, :dm]) * pa_ref[...]
                    + jax.nn.sigmoid(pg_ref[:, dm:]) * pb).astype(out_ref.dtype)


def _mix_merge(outs, lses, pa, p_gate, proj_b, bsz, s):
    n_tok, dm = pa.shape
    tm = min(ATTN_ROWS, s)
    per_b = s // tm
    n_strided = sum(2 * LANE_CHUNKS for _, d in DIL_GROUPS if d > 1)
    vspec = lambda d: pl.BlockSpec((None, tm // d, d * DIL_GROUP_W),
                                   lambda i: (i // per_b, i % per_b, 0))
    views = []
    for (_, d), o, l in zip(DIL_GROUPS, outs, lses):
        views += [(o, vspec(d)), (l, vspec(d))]
    row = lambda w: pl.BlockSpec((tm, w), lambda i: (i, 0))
    return pl.pallas_call(
        _mix_merge_kernel, grid=(n_tok // tm,),
        in_specs=[sp for _, sp in views] + [row(dm), row(2 * dm),
                                            pl.BlockSpec(proj_b.shape, lambda i: (0, 0))],
        out_specs=row(dm), out_shape=jax.ShapeDtypeStruct((n_tok, dm), BF16),
        scratch_shapes=[pltpu.VMEM((n_strided, tm, LANES), F32)],
        compiler_params=_cparams("parallel"), name="attn_mix_merge",
    )(*[a for a, _ in views], pa, p_gate, proj_b.astype(BF16))


W1_SPLIT_SUB = 256
SCAN_STEPS = 32
SCAN_SLAB = 16


def _split_even_odd(w_ref, g_ref, l_ref):
    sub, half = W1_SPLIT_SUB, W1_SPLIT_SUB // 2
    src = lax.broadcasted_iota(jnp.int32, (sub, sub), 0)
    dst = lax.broadcasted_iota(jnp.int32, (sub, sub), 1)
    want = jnp.where(dst < half, 2 * dst, 2 * (dst - half) + 1)
    sel = (src == want).astype(BF16)
    for q in range(w_ref.shape[1] // sub):
        w = w_ref[:, q * sub:(q + 1) * sub].astype(BF16)
        r = jnp.dot(w, sel, preferred_element_type=F32).astype(BF16)
        g_ref[:, q * half:(q + 1) * half] = r[:, :half]
        l_ref[:, q * half:(q + 1) * half] = r[:, half:]


def _rwkv_scan_kernel(r_ref, w_ref, k_ref, a_ref, b_ref, v_ref, w1_ref, w2_ref, y_ref, w1g_ref,
                      w1l_ref, w2b_ref, s_ref, *, tc):
    n = RWKV_HEAD

    @pl.when(pl.program_id(0) == 0)
    def _():
        s_ref[...] = jnp.zeros_like(s_ref)

    _split_even_odd(w1_ref, w1g_ref, w1l_ref)
    w2b_ref[...] = w2_ref[...].astype(BF16)

    slabs = n // SCAN_SLAB
    zero = jnp.zeros((SCAN_SLAB, s_ref.shape[2]), F32)

    def row(ref, tb, tt, j):
        return ref[tb, j, tt:tt + 1, :]

    def rows_of(slab):
        if isinstance(slab, int):
            return pl.ds(slab * SCAN_SLAB, SCAN_SLAB)
        return pl.ds(pl.multiple_of(slab * SCAN_SLAB, SCAN_SLAB), SCAN_SLAB)

    def state_times_a(tb, tt, slab):
        rows = rows_of(slab)
        acc = [zero, zero]
        for j in range(n):
            acc[j % 2] = acc[j % 2] + s_ref[j, rows, :] * row(a_ref, tb, tt, j)
        return acc[0] + acc[1]

    def update_and_read(tb, tt, slab, sa):
        rows = rows_of(slab)
        t = tb * SUBLANES + tt
        vt = v_ref[t, rows, :]
        yac = [zero, zero]
        for j in range(n):
            sj = (s_ref[j, rows, :] * row(w_ref, tb, tt, j) + sa * row(b_ref, tb, tt, j)
                  + vt * row(k_ref, tb, tt, j))
            s_ref[j, rows, :] = sj
            yac[j % 2] = yac[j % 2] + sj * row(r_ref, tb, tt, j)
        y_ref[t, rows, :] = yac[0] + yac[1]

    nblk = tc // SUBLANES

    def block(tb, sa):
        for tt in range(SUBLANES):
            def trip(slab, sa, tt=tt):
                update_and_read(tb, tt, slab, sa)
                return state_times_a(tb, tt, slab + 1)

            sa = lax.fori_loop(0, slabs - 1, trip, sa)
            update_and_read(tb, tt, slabs - 1, sa)
            if tt + 1 < SUBLANES:
                sa = state_times_a(tb, tt + 1, 0)
            else:
                sa = state_times_a(jnp.minimum(tb + 1, nblk - 1), 0, 0)
        return sa

    lax.fori_loop(0, nblk, block, state_times_a(0, 0, 0))


def _rwkv_scan_and_moe_weight_prep(r, w, k, a, b, v, w1, w2):
    s, n, l = v.shape
    tc = min(SCAN_STEPS, s)
    assert s % tc == 0
    steps = s // tc
    assert w1.shape[-1] % W1_SPLIT_SUB == 0

    def sliced(wgt, cols_out):
        e, rows, cols = wgt.shape
        assert (e * rows) % steps == 0
        per_step = e * rows // steps
        assert rows % per_step == 0 and per_step % (2 * SUBLANES) == 0
        per_e = rows // per_step
        spec = lambda c: pl.BlockSpec((None, per_step, c), lambda i: (i // per_e, i % per_e, 0))
        return spec(cols), spec(cols_out), jax.ShapeDtypeStruct((e, rows, cols_out), BF16)

    w1_in, w1_out, w1_shape = sliced(w1, w1.shape[-1] // 2)
    w2_in, w2_out, w2_shape = sliced(w2, w2.shape[-1])
    rows = pl.BlockSpec((tc // SUBLANES, n, SUBLANES, l), lambda i: (i, 0, 0, 0))
    blk = pl.BlockSpec((tc, n, l), lambda i: (i, 0, 0))
    return pl.pallas_call(
        functools.partial(_rwkv_scan_kernel, tc=tc),
        grid=(steps,), in_specs=[rows] * 5 + [blk, w1_in, w2_in],
        out_specs=[blk, w1_out, w1_out, w2_out],
        out_shape=[jax.ShapeDtypeStruct((s, n, l), F32), w1_shape, w1_shape, w2_shape],
        scratch_shapes=[pltpu.VMEM((n, n, l), F32)],
        compiler_params=_cparams("arbitrary"), name="rwkv_scan",
    )(r, w, k, a, b, v, w1, w2)


RWKV_PREP_ROWS = 32
SUBLANES = 8


def _head_sum(x):
    cols = x.shape[-1] // LANES
    acc = x[:, :LANES]
    for c in range(1, cols):
        acc = acc + x[:, c * LANES:(c + 1) * LANES]
    acc = acc + pltpu.roll(acc, RWKV_HEADS, 1)
    acc = acc + pltpu.roll(acc, 2 * RWKV_HEADS, 1)
    return jnp.concatenate([acc] * cols, axis=-1)


def _token_shift_rows(p, last_prev_row, mix, first):
    rolled = pltpu.roll(p, 1, 0)
    row0 = jnp.where(first, jnp.zeros_like(last_prev_row), last_prev_row)
    t = lax.broadcasted_iota(jnp.int32, p.shape, 0)
    prev = jnp.where(t == 0, row0, rolled)
    return p + (prev - p) * mix


def _rwkv_prep_kernel(rkv_ref, rkvp_ref, lo_ref, lop_ref, mixr_ref, mixl_ref, w0_ref, a0_ref,
                      kk_ref, ka_ref, rk_ref, wup_ref, aup_ref, gup_ref,
                      r_ref, w_ref, k_ref, al_ref, be_ref, v_ref, g_ref, bo_ref, tiles_ref):
    first = pl.program_id(0) == 0
    d = D_MODEL
    nb, ts = rkv_ref.shape[0], rkv_ref.shape[1]
    mm = lambda a, w_ref_: jnp.dot(a.astype(BF16), w_ref_[...].astype(BF16),
                                   preferred_element_type=F32)
    for bi in range(nb):
        def seg(c):
            sl = slice(c * d, (c + 1) * d)
            return _token_shift_rows(rkv_ref[bi, :, sl], rkvp_ref[bi, SUBLANES - 1:SUBLANES, sl],
                                     mixr_ref[:, sl], first)

        r, k, v = seg(0), seg(1), seg(2)
        zl = _token_shift_rows(lo_ref[bi], lop_ref[bi, SUBLANES - 1:SUBLANES, :], mixl_ref[...],
                               first)
        wd = zl[:, :DECAY_LORA]
        ad = zl[:, DECAY_LORA:DECAY_LORA + AAA_LORA]
        gd = zl[:, DECAY_LORA + AAA_LORA:]
        z = -(w0_ref[...] + mm(jnp.tanh(wd), wup_ref))
        softplus = jnp.maximum(z, 0.0) + jnp.log1p(jnp.exp(-jnp.abs(z)))
        w = -softplus - 0.5
        a = jax.nn.sigmoid(a0_ref[...] + mm(ad, aup_ref))
        kk = k * kk_ref[...]
        kk = kk / jnp.maximum(jnp.sqrt(_head_sum(kk * kk)), 1e-12)
        k2 = k * (1.0 + (a - 1.0) * ka_ref[...])
        for op, val in enumerate((r, jnp.exp(-jnp.exp(w)), k2, -kk, kk * a)):
            for c in range(d // LANES):
                col = val[:, c * LANES:(c + 1) * LANES]
                tiles_ref[op, bi, :, c * LANES:(c + 1) * LANES] = (
                    col if bi == 0 else pltpu.roll(col, bi * RWKV_HEADS, 1))
        v_ref[bi] = v
        g_ref[bi] = mm(jax.nn.sigmoid(gd), gup_ref)
        bo_ref[bi] = _head_sum(r * k2 * rk_ref[...]) * v

    lane_group = lax.broadcasted_iota(jnp.int32, (ts, LANES), 1) // RWKV_HEADS
    for op, out_ref in enumerate((r_ref, w_ref, k_ref, al_ref, be_ref)):
        for c in range(d // LANES):
            cols = [tiles_ref[op, bi, :, c * LANES:(c + 1) * LANES] for bi in range(nb)]
            for jm in range(nb):
                sel = cols[(0 - jm) % nb]
                for g in range(1, nb):
                    sel = jnp.where(lane_group == g, cols[(g - jm) % nb], sel)
                out = sel if jm == 0 else pltpu.roll(sel, (nb - jm) * RWKV_HEADS, 1)
                for tb in range(ts // SUBLANES):
                    out_ref[tb, c * nb + jm] = out[tb * SUBLANES:(tb + 1) * SUBLANES]


def _rwkv_prep(p_rkv, p_lora, shift_mix, w0, w_up, a0, a_up, g_up, k_k, k_a, r_k):
    b, s, c_rkv = p_rkv.shape
    d, ts, c_lo = D_MODEL, min(RWKV_PREP_ROWS, s), p_lora.shape[-1]
    assert s % ts == 0 and ts % SUBLANES == 0
    assert b * RWKV_HEADS == LANES, "the scan packs exactly (batch, head) onto the 128 lanes"
    cur = lambda c: pl.BlockSpec((b, ts, c), lambda i: (0, i, 0))
    prev = lambda c: pl.BlockSpec(
        (b, SUBLANES, c), lambda i: (0, jnp.maximum(i * (ts // SUBLANES) - 1, 0), 0))
    vec = lambda c: pl.BlockSpec((1, c), lambda i: (0, 0))
    full = lambda a: pl.BlockSpec(a.shape, lambda i: (0, 0))
    rows = pl.BlockSpec((ts // SUBLANES, RWKV_HEAD, SUBLANES, LANES), lambda i: (i, 0, 0, 0))
    rows_shape = jax.ShapeDtypeStruct((s // SUBLANES, RWKV_HEAD, SUBLANES, LANES), F32)
    tok_shape = jax.ShapeDtypeStruct((b, s, d), F32)
    row = lambda t: t.reshape(1, -1)
    return pl.pallas_call(
        _rwkv_prep_kernel, grid=(s // ts,),
        in_specs=[cur(c_rkv), prev(c_rkv), cur(c_lo), prev(c_lo), vec(c_rkv), vec(c_lo),
                  vec(d), vec(d), vec(d), vec(d), vec(d), full(w_up), full(a_up), full(g_up)],
        out_specs=[rows] * 5 + [cur(d)] * 3, out_shape=[rows_shape] * 5 + [tok_shape] * 3,
        scratch_shapes=[pltpu.VMEM((5, b, ts, d), F32)],
        compiler_params=_cparams("arbitrary"), name="rwkv_prep",
    )(p_rkv, p_rkv, p_lora, p_lora, row(shift_mix[:c_rkv]), row(shift_mix[c_rkv:]), row(w0),
      row(a0), row(k_k), row(k_a), row(r_k), w_up, a_up, g_up)


def _proj_a_kernel(y_ref, bo_ref, g_ref, lg_ref, lb_ref, w_ref, o_ref):
    y = y_ref[...]
    inv_n = 1.0 / RWKV_HEAD
    c = y - _head_sum(y) * inv_n
    var = _head_sum(c * c) * inv_n
    ya = (c * lax.rsqrt(var + GN_EPS) * lg_ref[...] + lb_ref[...] + bo_ref[...]) * g_ref[...]
    o_ref[...] = jnp.dot(ya.astype(BF16), w_ref[...], preferred_element_type=F32)


def _rwkv_post_proj(y, bonus, gate, lnx_g, lnx_b, proj_a, *, tm=256):
    m, d = y.shape
    assert m % tm == 0
    rowb = pl.BlockSpec((tm, d), lambda i: (i, 0))
    vec = pl.BlockSpec((1, d), lambda i: (0, 0))
    return pl.pallas_call(
        _proj_a_kernel, grid=(m // tm,),
        in_specs=[rowb, rowb, rowb, vec, vec, pl.BlockSpec((d, d), lambda i: (0, 0))],
        out_specs=rowb, out_shape=jax.ShapeDtypeStruct((m, d), F32),
        compiler_params=_cparams("parallel"), name="rwkv_post_proj_a",
    )(y, bonus, gate, lnx_g.reshape(1, d), lnx_b.reshape(1, d), proj_a.astype(BF16))


def _cross_attn_kernel(q_ref, k_ref, v_ref, o_ref):
    scale = CA_HEAD_DIM ** -0.5
    for h in range(CA_HEADS):
        sl = slice(h * CA_HEAD_DIM, (h + 1) * CA_HEAD_DIM)
        q = q_ref[:, sl].astype(BF16)
        k = k_ref[:, sl].astype(BF16)
        s = lax.dot_general(q, k, (((1,), (1,)), ((), ())), preferred_element_type=F32) * scale
        m = jnp.max(s, axis=-1, keepdims=True)
        p = jnp.exp(s - m)
        den = jnp.sum(p, axis=-1, keepdims=True)
        o = jnp.dot(p.astype(BF16), v_ref[:, sl].astype(BF16), preferred_element_type=F32)
        o_ref[:, sl] = (o / den).astype(o_ref.dtype)


def _cross_attention(q, kv, *, tq=512):
    b, s, d = q.shape
    mlen = kv.shape[1]
    tq = min(tq, s)
    return pl.pallas_call(
        _cross_attn_kernel, grid=(b, s // tq),
        in_specs=[pl.BlockSpec((None, tq, d), lambda i, j: (i, j, 0)),
                  pl.BlockSpec((None, mlen, d), lambda i, j: (i, 0, 0)),
                  pl.BlockSpec((None, mlen, d), lambda i, j: (i, 0, 1))],
        out_specs=pl.BlockSpec((None, tq, d), lambda i, j: (i, j, 0)),
        out_shape=jax.ShapeDtypeStruct((b, s, d), BF16),
        compiler_params=_cparams("parallel", "parallel"), name="cross_attn",
    )(q, kv, kv)


def _router_kernel(x_ref, w_ref, b_ref, o_ref):
    o_ref[...] = jnp.dot(x_ref[...], w_ref[...], precision=lax.Precision.HIGHEST,
                         preferred_element_type=F32) + b_ref[...]


def _router_logits(x, w, b, *, tm=512):
    m, d = x.shape
    e = w.shape[1]
    return pl.pallas_call(
        _router_kernel, grid=(m // tm,),
        in_specs=[pl.BlockSpec((tm, d), lambda i: (i, 0)),
                  pl.BlockSpec((d, e), lambda i: (0, 0)),
                  pl.BlockSpec((1, e), lambda i: (0, 0))],
        out_specs=pl.BlockSpec((tm, e), lambda i: (i, 0)),
        out_shape=jax.ShapeDtypeStruct((m, e), F32),
        compiler_params=_cparams("parallel"), name="router",
    )(x, w, b.reshape(1, e))


def _moe_kernel(be_ref, nused_ref, x_ref, g_ref, w1g_ref, w1l_ref, b1g_ref, b1l_ref, w2_ref,
                b2_ref, o_ref):
    i = pl.program_id(0)

    @pl.when(i < nused_ref[0])
    def _():
        x = x_ref[...]
        glu = jnp.dot(x, w1g_ref[...], preferred_element_type=F32) + b1g_ref[...]
        lin = jnp.dot(x, w1l_ref[...], preferred_element_type=F32) + b1l_ref[...]
        glu = jnp.minimum(glu, SWIGLU_LIMIT)
        lin = jnp.clip(lin, -SWIGLU_LIMIT, SWIGLU_LIMIT)
        act = glu * jax.nn.sigmoid(SWIGLU_ALPHA * glu) * (lin + 1.0)
        y = jnp.dot(act.astype(BF16), w2_ref[...], preferred_element_type=F32) + b2_ref[...]
        o_ref[...] = y * g_ref[...]

    @pl.when(i >= nused_ref[0])
    def _():
        o_ref[...] = jnp.zeros_like(o_ref)


def _moe_experts(blk_exp, n_used, xs, row_gate, w1g, w1l, b1g, b1l, w2, b2):
    rows, d = xs.shape
    nblk = rows // MOE_BLK
    f = w1g.shape[-1]
    wspec = lambda shape: pl.BlockSpec(shape, lambda i, be, nu: (be[i], 0, 0))
    grid_spec = pltpu.PrefetchScalarGridSpec(
        num_scalar_prefetch=2, grid=(nblk,),
        in_specs=[pl.BlockSpec((MOE_BLK, d), lambda i, be, nu: (i, 0)),
                  pl.BlockSpec((MOE_BLK, 1), lambda i, be, nu: (i, 0)),
                  wspec((None, d, f)), wspec((None, d, f)),
                  wspec((None, 1, f)), wspec((None, 1, f)),
                  wspec((None, f, d)), wspec((None, 1, d))],
        out_specs=pl.BlockSpec((MOE_BLK, d), lambda i, be, nu: (i, 0)),
    )
    return pl.pallas_call(
        _moe_kernel, grid_spec=grid_spec,
        out_shape=jax.ShapeDtypeStruct((rows, d), F32),
        compiler_params=pltpu.CompilerParams(dimension_semantics=("arbitrary",),
                                             vmem_limit_bytes=V7X_VMEM_LIMIT_MOE_BYTES),
        name="moe_experts",
    )(blk_exp, n_used, xs, row_gate.reshape(rows, 1), w1g, w1l, b1g, b1l, w2, b2)


def _per_expert(table, idx):
    hit = idx[:, None] == jnp.arange(N_EXPERTS, dtype=idx.dtype)[None, :]
    return jnp.sum(jnp.where(hit, table[None, :], 0), axis=1)


def _moe_ffn(x_bf, logits, moe_weights, b1, b2):
    n, d = x_bf.shape
    top_val, top_idx = lax.top_k(logits, TOP_K)
    top_w = jax.nn.softmax(top_val, axis=-1)
    flat_e = top_idx.reshape(-1).astype(jnp.int32)
    order = jnp.argsort(flat_e).astype(jnp.int32)
    rank = jnp.argsort(order).astype(jnp.int32)
    counts = jnp.sum(flat_e[:, None] == jnp.arange(N_EXPERTS, dtype=jnp.int32)[None, :],
                     axis=0, dtype=jnp.int32)
    starts = jnp.cumsum(counts) - counts
    padded = (counts + MOE_BLK - 1) // MOE_BLK * MOE_BLK
    pends = jnp.cumsum(padded)
    pstarts = pends - padded
    rows = n * TOP_K + N_EXPERTS * MOE_BLK
    nblk = rows // MOE_BLK
    blk_start = jnp.arange(nblk, dtype=jnp.int32) * MOE_BLK
    blk_exp = jnp.minimum(jnp.sum(pends[None, :] <= blk_start[:, None], axis=1, dtype=jnp.int32),
                          N_EXPERTS - 1)
    n_used = (pends[-1] // MOE_BLK).astype(jnp.int32).reshape(1)
    row_exp = jnp.repeat(blk_exp, MOE_BLK)
    off = jnp.arange(rows, dtype=jnp.int32) - _per_expert(pstarts, row_exp)
    valid = off < _per_expert(counts, row_exp)
    assign = order[jnp.where(valid, _per_expert(starts, row_exp) + off, 0)]
    row_tok = jnp.where(valid, assign // TOP_K, jnp.arange(rows, dtype=jnp.int32) % n)
    row_gate = jnp.where(valid, top_w.reshape(-1)[assign], 0.0)
    xs = x_bf[row_tok]
    b1g = b1[:, None, 0::2]
    b1l = b1[:, None, 1::2]
    w1g, w1l, w2 = moe_weights
    ys = _moe_experts(blk_exp, n_used, xs, row_gate, w1g, w1l, b1g, b1l, w2,
                      b2[:, None, :])
    pos = (_per_expert(pstarts, flat_e) + rank - _per_expert(starts, flat_e)).reshape(n, TOP_K)
    return [ys[pos[:, kk]] for kk in range(TOP_K)]


def _heads_minor(p):
    lead = p.shape[:-1]
    return p.reshape(lead + (RWKV_HEADS, RWKV_HEAD)).swapaxes(-1, -2).reshape(p.shape)


def _to_scan_layout(t, bsz, s):
    return t.reshape(bsz, s, RWKV_HEAD, RWKV_HEADS).transpose(1, 2, 0, 3).reshape(
        s, RWKV_HEAD, bsz * RWKV_HEADS)


def _from_scan_layout(t, bsz, s):
    return t.reshape(s, RWKV_HEAD, bsz, RWKV_HEADS).transpose(2, 0, 1, 3).reshape(bsz, s, D_MODEL)


def _rwkv7_branch_proj(p_rkv, p_lora, shift_mix, w0, w_up, a0, a_up, g_up, k_k, k_a, r_k, lnx_g,
                       lnx_b, proj_a, moe_w1, moe_w2):
    bsz, s, c_rkv = p_rkv.shape
    hm = _heads_minor
    mix = jnp.concatenate([hm(shift_mix[:c_rkv].reshape(3, D_MODEL)).reshape(-1),
                           shift_mix[c_rkv:]])
    r, decay, k, alpha, beta, v, gate, bonus = _rwkv_prep(
        p_rkv, p_lora, mix, hm(w0), hm(w_up), hm(a0), hm(a_up), hm(g_up), hm(k_k), hm(k_a),
        hm(r_k.reshape(-1)))
    y, *moe_weights = _rwkv_scan_and_moe_weight_prep(r, decay, k, alpha, beta,
                                                     _to_scan_layout(v, bsz, s), moe_w1, moe_w2)
    y = _from_scan_layout(y, bsz, s)
    flat = lambda t: t.reshape(bsz * s, D_MODEL)
    proj_a_hm = proj_a.reshape(RWKV_HEADS, RWKV_HEAD, -1).swapaxes(0, 1).reshape(proj_a.shape)
    pa = _rwkv_post_proj(flat(y), flat(bonus), flat(gate), hm(lnx_g), hm(lnx_b), proj_a_hm)
    return pa, moe_weights


def _hybrid_mixer(x, w_in, shift_mix, w0, w_up, a0, a_up, g_up, k_k, k_a, r_k, lnx_g, lnx_b,
                  proj_a, proj_b, moe_w1, moe_w2):
    bsz, s, d = x.shape
    n_tok = bsz * s
    x_bf = x.astype(BF16)
    x2d = x_bf.reshape(n_tok, d)
    c0, c1, c2 = 3 * D_MODEL, RWKV_COLS, RWKV_COLS + ATTN_COLS
    w_rkv = _heads_minor(w_in[:, :c0].reshape(d, 3, D_MODEL)).reshape(d, c0)
    p_rkv = _matmul(x2d, w_rkv, tm=1024, tn=1024, name="in_rkv").reshape(bsz, s, -1)
    p_lora = _matmul(x2d, w_in[:, c0:c1], tm=1024, tn=512, name="in_lora").reshape(bsz, s, -1)
    p_gate = _matmul(x2d, w_in[:, c2:], tm=1024, tn=1024, name="in_gate")

    pa, moe_weights = _rwkv7_branch_proj(p_rkv, p_lora, shift_mix, w0, w_up, a0, a_up, g_up, k_k,
                                         k_a, r_k, lnx_g, lnx_b, proj_a, moe_w1, moe_w2)

    cos_t, sin_t = _rotary_tables(s)
    views = _attn_in_proj(x_bf, w_in[:, c1:c2], cos_t, sin_t)
    outs, lses = [], []
    for gi, (window, dilation) in enumerate(DIL_GROUPS):
        o, l = _dilated_attention(views[gi], gi, window, dilation)
        outs.append(o)
        lses.append(l)
    return _mix_merge(outs, lses, pa, p_gate, proj_b, bsz, s), moe_weights


def _memory_cross_attention(x_bf, mem, wq, wkv, bsz, s):
    d = x_bf.shape[-1]
    q = _matmul(x_bf, wq, tm=1024, tn=1024, name="ca_q").reshape(bsz, s, d)
    kv = _matmul(mem.reshape(-1, d), wkv, tm=512, tn=1024, name="ca_kv").reshape(bsz, -1, 2 * d)
    return _cross_attention(q, kv).reshape(-1, d)


def kernel(x, mem, w_in, shift_mix, w0, w_up, a0, a_up, g_up, k_k, k_a, r_k, lnx_g, lnx_b, proj_a, proj_b, w_out, ln1_g, ln1_b, ca_wq, ca_wkv, ca_wo, ln2_g, ln2_b, router_w, router_b, moe_w1, moe_b1, moe_w2, moe_b2, ln3_g, ln3_b):
    bsz, s, d = x.shape
    n_tok = bsz * s
    for l in range(DEPTH):
        merged, moe_weights = _hybrid_mixer(x, w_in[l], shift_mix[l], w0[l], w_up[l], a0[l],
                                            a_up[l], g_up[l], k_k[l], k_a[l], r_k[l], lnx_g[l],
                                            lnx_b[l], proj_a[l], proj_b[l], moe_w1[l], moe_w2[l])
        x1, x1_bf = _matmul_res_ln(merged, w_out[l], x.reshape(n_tok, d), ln1_g[l], ln1_b[l],
                                   name="w_out_ln1")
        o = _memory_cross_attention(x1_bf, mem, ca_wq[l], ca_wkv[l], bsz, s)
        x2, x2_bf = _matmul_res_ln(o, ca_wo[l], x1, ln2_g[l], ln2_b[l], name="ca_o_ln2")
        logits = _router_logits(x2, router_w[l], router_b[l])
        hs = _moe_ffn(x2_bf, logits, moe_weights, moe_b1[l], moe_b2[l])
        x = _res_layer_norm(x2, hs, ln3_g[l], ln3_b[l], name="ln3").reshape(bsz, s, d)
    return x
```

```python
import functools

import jax
import jax.numpy as jnp
from jax import lax
from jax.experimental import pallas as pl
from jax.experimental.pallas import tpu as pltpu

F32 = jnp.float32
BF16 = jnp.bfloat16

D_MODEL = 2048
RWKV_HEAD = 64
RWKV_HEADS = D_MODEL // RWKV_HEAD
DECAY_LORA = 96
AAA_LORA = 96
GATE_LORA = 256
GN_EPS = 64e-5
DIL_GROUPS = ((128, 1), (512, 4), (2048, 16))
N_GROUPS = len(DIL_GROUPS)
DIL_HEADS = 8
DIL_HEAD_DIM = 64
DIL_GROUP_W = DIL_HEADS * DIL_HEAD_DIM
DIL_DIM = N_GROUPS * DIL_GROUP_W
BLK = 128
ROPE_THETA = 10000.0
NEG_INF = -1e30
RWKV_COLS = 3 * D_MODEL + DECAY_LORA + AAA_LORA + GATE_LORA
ATTN_COLS = 3 * DIL_DIM
CA_HEADS = 4
CA_HEAD_DIM = D_MODEL // CA_HEADS
N_EXPERTS = 32
TOP_K = 4
D_FF = D_MODEL
SWIGLU_LIMIT = 7.0
SWIGLU_ALPHA = 1.702
MOE_BLK = 128
LN_EPS = 1e-5
DEPTH = 1
DEEPNORM_ALPHA = (2 * DEPTH) ** 0.25

V7X_VMEM_LIMIT_BYTES = 56 * 1024 * 1024
V7X_VMEM_LIMIT_MOE_BYTES = 61 * 1024 * 1024


def _cparams(*sem):
    return pltpu.CompilerParams(dimension_semantics=sem, vmem_limit_bytes=V7X_VMEM_LIMIT_BYTES)


def _mm_kernel(a_ref, b_ref, o_ref):
    o_ref[...] = jnp.dot(a_ref[...].astype(BF16), b_ref[...].astype(BF16),
                         preferred_element_type=F32).astype(o_ref.dtype)


def _matmul(a, b, *, tm, tn, name, out_dtype=F32):
    m, k = a.shape
    _, n = b.shape
    tm, tn = min(tm, m), min(tn, n)
    assert m % tm == 0 and n % tn == 0, (m, n, tm, tn)
    return pl.pallas_call(
        _mm_kernel,
        grid=(n // tn, m // tm),
        in_specs=[pl.BlockSpec((tm, k), lambda j, i: (i, 0)),
                  pl.BlockSpec((k, tn), lambda j, i: (0, j))],
        out_specs=pl.BlockSpec((tm, tn), lambda j, i: (i, j)),
        out_shape=jax.ShapeDtypeStruct((m, n), out_dtype),
        compiler_params=_cparams("parallel", "parallel"),
        name=name,
    )(a, b)


def _ln_kernel(x_ref, *rest):
    *h_refs, g_ref, b_ref, o_ref = rest
    t = DEEPNORM_ALPHA * x_ref[...]
    for h_ref in h_refs:
        t = t + h_ref[...]
    mu = jnp.mean(t, axis=-1, keepdims=True)
    c = t - mu
    var = jnp.mean(c * c, axis=-1, keepdims=True)
    o_ref[...] = c * lax.rsqrt(var + LN_EPS) * g_ref[...] + b_ref[...]


def _res_layer_norm(x, h, g, b, *, name, tm=256):
    m, d = x.shape
    assert m % tm == 0 and h.shape[0] % m == 0
    per = m // tm
    row = pl.BlockSpec((tm, d), lambda i: (i, 0))
    part = lambda k: pl.BlockSpec((tm, d), lambda i: (i + k * per, 0))
    vec = pl.BlockSpec((1, d), lambda i: (0, 0))
    nparts = h.shape[0] // m
    return pl.pallas_call(
        _ln_kernel, grid=(m // tm,),
        in_specs=[row] + [part(k) for k in range(nparts)] + [vec, vec], out_specs=row,
        out_shape=jax.ShapeDtypeStruct((m, d), F32),
        compiler_params=_cparams("parallel"), name=name,
    )(x, *([h] * nparts), g.reshape(1, d), b.reshape(1, d))


def _mm_res_ln_kernel(a_ref, w_ref, x_ref, g_ref, b_ref, o_ref, ob_ref):
    h = jnp.dot(a_ref[...].astype(BF16), w_ref[...], preferred_element_type=F32)
    t = DEEPNORM_ALPHA * x_ref[...] + h
    mu = jnp.mean(t, axis=-1, keepdims=True)
    c = t - mu
    var = jnp.mean(c * c, axis=-1, keepdims=True)
    y = c * lax.rsqrt(var + LN_EPS) * g_ref[...] + b_ref[...]
    o_ref[...] = y
    ob_ref[...] = y.astype(BF16)


def _matmul_res_ln(a, w, x, g, b, *, name, tm=512):
    m, k = a.shape
    d = w.shape[1]
    assert m % tm == 0
    row = lambda c: pl.BlockSpec((tm, c), lambda i: (i, 0))
    const = lambda r, c: pl.BlockSpec((r, c), lambda i: (0, 0))
    in_specs = [row(k), const(k, d), row(d), const(1, d), const(1, d)]
    args = [a, w.astype(BF16), x, g.reshape(1, d), b.reshape(1, d)]
    out_specs = [row(d), row(d)]
    out_shape = [jax.ShapeDtypeStruct((m, d), F32), jax.ShapeDtypeStruct((m, d), BF16)]
    return pl.pallas_call(
        _mm_res_ln_kernel,
        grid=(m // tm,), in_specs=in_specs, out_specs=out_specs, out_shape=out_shape,
        compiler_params=_cparams("parallel"), name=name,
    )(*args)


LANES = 128
ATTN_ROWS = 256
GROUP_QKV_W = 3 * DIL_GROUP_W
LANE_CHUNKS = DIL_GROUP_W // LANES


def _rotary_tables(s):
    half = DIL_HEAD_DIM // 2
    inv = ROPE_THETA ** (-jnp.arange(half, dtype=F32) * 2.0 / DIL_HEAD_DIM)
    ang = jnp.arange(s, dtype=F32)[:, None] * inv[None, :]
    cos, sin = jnp.cos(ang), jnp.sin(ang)
    return (jnp.concatenate([cos, cos, cos, cos], axis=-1),
            jnp.concatenate([-sin, sin, -sin, sin], axis=-1))


def _attn_in_kernel(x_ref, w_ref, cos_ref, sin_ref, o0_ref, o1_ref, o2_ref, scr_ref):
    tm = x_ref.shape[0]
    acc = jnp.dot(x_ref[...].astype(BF16), w_ref[...], preferred_element_type=F32)
    cos, sin = cos_ref[...], sin_ref[...]
    lane = lax.broadcasted_iota(jnp.int32, (tm, LANES), 1)
    first_half = (lane % DIL_HEAD_DIM) < (DIL_HEAD_DIM // 2)
    outs = (o0_ref, o1_ref, o2_ref)
    slab = 0
    for part in range(3):
        for gi, (_, d) in enumerate(DIL_GROUPS):
            for c in range(LANE_CHUNKS):
                col = part * DIL_DIM + gi * DIL_GROUP_W + c * LANES
                x = acc[:, col:col + LANES]
                if part < 2:
                    partner = jnp.where(first_half, pltpu.roll(x, LANES - 32, 1),
                                        pltpu.roll(x, 32, 1))
                    x = x * cos + partner * sin
                dst = part * DIL_GROUP_W + c * LANES
                if d == 1:
                    outs[gi][:, dst:dst + LANES] = x
                else:
                    scr_ref[slab] = x
                    for r in range(d):
                        outs[gi][:, r * GROUP_QKV_W + dst:r * GROUP_QKV_W + dst + LANES] = (
                            scr_ref[slab, pl.ds(r, tm // d, stride=d), :])
                    slab += 1


def _attn_in_proj(x, w_attn, cos_t, sin_t):
    b, s, dm = x.shape
    tm = min(ATTN_ROWS, s)
    dmax = max(d for _, d in DIL_GROUPS)
    assert s % tm == 0 and tm % (dmax * 8) == 0
    n_strided = sum(3 * LANE_CHUNKS for _, d in DIL_GROUPS if d > 1)
    tab = pl.BlockSpec((tm, LANES), lambda i, j: (j, 0))
    ospec = lambda d: pl.BlockSpec((None, tm // d, d * GROUP_QKV_W), lambda i, j: (i, j, 0))
    return pl.pallas_call(
        _attn_in_kernel, grid=(b, s // tm),
        in_specs=[pl.BlockSpec((None, tm, dm), lambda i, j: (i, j, 0)),
                  pl.BlockSpec(w_attn.shape, lambda i, j: (0, 0), pipeline_mode=pl.Buffered(1)),
                  tab, tab],
        out_specs=[ospec(d) for _, d in DIL_GROUPS],
        out_shape=[jax.ShapeDtypeStruct((b, s // d, d * GROUP_QKV_W), F32) for _, d in DIL_GROUPS],
        scratch_shapes=[pltpu.VMEM((n_strided, tm, LANES), F32)],
        compiler_params=_cparams("parallel", "parallel"), name="in_attn_rope",
    )(x, w_attn.astype(BF16), cos_t, sin_t)


DIL_Q_BLOCKS = 2


def _dil_attn_kernel(q_ref, kp_ref, kc_ref, vp_ref, vc_ref, o_ref, l_ref, *, span):
    step = pl.program_id(2)
    qi = lax.broadcasted_iota(jnp.int32, (BLK, 2 * BLK), 0) + BLK
    ki = lax.broadcasted_iota(jnp.int32, (BLK, 2 * BLK), 1)
    dist = qi - ki
    band = (dist >= 0) & (dist <= span)
    scale = DIL_HEAD_DIM ** -0.5
    for blk in range(DIL_Q_BLOCKS):
        rows = slice(blk * BLK, (blk + 1) * BLK)
        before = slice((blk - 1) * BLK, blk * BLK)
        mask = band & ((step > 0) | (ki >= BLK)) if blk == 0 else band
        q = q_ref[rows, :].astype(BF16)
        k_prev = kp_ref[...] if blk == 0 else kc_ref[before, :]
        v_prev = vp_ref[...] if blk == 0 else vc_ref[before, :]
        k = jnp.concatenate([k_prev, kc_ref[rows, :]], axis=0).astype(BF16)
        v = jnp.concatenate([v_prev, vc_ref[rows, :]], axis=0).astype(BF16)
        for h in range(DIL_HEADS):
            sl = slice(h * DIL_HEAD_DIM, (h + 1) * DIL_HEAD_DIM)
            s = lax.dot_general(q[:, sl], k[:, sl], (((1,), (1,)), ((), ())),
                                preferred_element_type=F32) * scale
            s = jnp.where(mask, s, NEG_INF)
            m = jnp.max(s, axis=-1, keepdims=True)
            p = jnp.exp(s - m)
            den = jnp.sum(p, axis=-1, keepdims=True)
            o = jnp.dot(p.astype(BF16), v[:, sl], preferred_element_type=F32)
            o_ref[rows, sl] = o / den
            l_ref[rows, sl] = jnp.broadcast_to(m + jnp.log(den), (BLK, DIL_HEAD_DIM))


def _dilated_attention(qkv_view, gi, window, dilation):
    b, n, _ = qkv_view.shape
    d = dilation
    rows = DIL_Q_BLOCKS * BLK
    assert n % rows == 0
    cur = (None, rows, DIL_GROUP_W)
    prev = (None, BLK, DIL_GROUP_W)

    def col(which):
        return lambda bi, r, nb: (bi, nb, r * 3 + which)

    def col_prev(which):
        return lambda bi, r, nb: (bi, jnp.maximum(nb * DIL_Q_BLOCKS - 1, 0), r * 3 + which)

    out_spec = pl.BlockSpec(cur, lambda bi, r, nb: (bi, nb, r))
    return pl.pallas_call(
        functools.partial(_dil_attn_kernel, span=window // dilation),
        grid=(b, d, n // rows),
        in_specs=[pl.BlockSpec(cur, col(0)),
                  pl.BlockSpec(prev, col_prev(1)), pl.BlockSpec(cur, col(1)),
                  pl.BlockSpec(prev, col_prev(2)), pl.BlockSpec(cur, col(2))],
        out_specs=[out_spec, out_spec],
        out_shape=[jax.ShapeDtypeStruct((b, n, d * DIL_GROUP_W), F32)] * 2,
        compiler_params=_cparams("parallel", "parallel", "arbitrary"),
        name=f"dil_attn_g{gi}",
    )(qkv_view, qkv_view, qkv_view, qkv_view, qkv_view)


def _mix_merge_kernel(o0_ref, l0_ref, o1_ref, l1_ref, o2_ref, l2_ref, pa_ref, pg_ref, w_ref,
                      out_ref, scr_ref):
    tm, dm = pa_ref.shape
    slab = [0]

    def natural(ref, d):
        if d == 1:
            return [ref[:, c * LANES:(c + 1) * LANES] for c in range(LANE_CHUNKS)]
        chunks = []
        for c in range(LANE_CHUNKS):
            for r in range(d):
                scr_ref[slab[0], pl.ds(r, tm // d, stride=d), :] = (
                    ref[:, r * DIL_GROUP_W + c * LANES:r * DIL_GROUP_W + (c + 1) * LANES])
            chunks.append(scr_ref[slab[0]])
            slab[0] += 1
        return chunks

    dils = [d for _, d in DIL_GROUPS]
    o = [natural(r, d) for r, d in zip((o0_ref, o1_ref, o2_ref), dils)]
    l = [natural(r, d) for r, d in zip((l0_ref, l1_ref, l2_ref), dils)]
    yb = []
    for c in range(LANE_CHUNKS):
        m = jnp.maximum(jnp.maximum(l[0][c], l[1][c]), l[2][c])
        e = [jnp.exp(l[g][c] - m) for g in range(N_GROUPS)]
        den = e[0] + e[1] + e[2]
        yb.append((e[0] / den) * o[0][c] + (e[1] / den) * o[1][c] + (e[2] / den) * o[2][c])
    pb = jnp.dot(jnp.concatenate(yb, axis=-1).astype(BF16), w_ref[...], preferred_element_type=F32)
    out_ref[...] = (jax.nn.sigmoid(pg_ref[:, :dm]) * pa_ref[...]
                    + jax.nn.sigmoid(pg_ref[:, dm:]) * pb).astype(out_ref.dtype)


def _mix_merge(outs, lses, pa, p_gate, proj_b, bsz, s):
    n_tok, dm = pa.shape
    tm = min(ATTN_ROWS, s)
    per_b = s // tm
    n_strided = sum(2 * LANE_CHUNKS for _, d in DIL_GROUPS if d > 1)
    vspec = lambda d: pl.BlockSpec((None, tm // d, d * DIL_GROUP_W),
                                   lambda i: (i // per_b, i % per_b, 0))
    views = []
    for (_, d), o, l in zip(DIL_GROUPS, outs, lses):
        views += [(o, vspec(d)), (l, vspec(d))]
    row = lambda w: pl.BlockSpec((tm, w), lambda i: (i, 0))
    return pl.pallas_call(
        _mix_merge_kernel, grid=(n_tok // tm,),
        in_specs=[sp for _, sp in views] + [row(dm), row(2 * dm),
                                            pl.BlockSpec(proj_b.shape, lambda i: (0, 0))],
        out_specs=row(dm), out_shape=jax.ShapeDtypeStruct((n_tok, dm), BF16),
        scratch_shapes=[pltpu.VMEM((n_strided, tm, LANES), F32)],
        compiler_params=_cparams("parallel"), name="attn_mix_merge",
    )(*[a for a, _ in views], pa, p_gate, proj_b.astype(BF16))


W1_SPLIT_SUB = 256
SCAN_STEPS = 32
SCAN_SLAB = 16


def _split_even_odd(w_ref, g_ref, l_ref):
    sub, half = W1_SPLIT_SUB, W1_SPLIT_SUB // 2
    src = lax.broadcasted_iota(jnp.int32, (sub, sub), 0)
    dst = lax.broadcasted_iota(jnp.int32, (sub, sub), 1)
    want = jnp.where(dst < half, 2 * dst, 2 * (dst - half) + 1)
    sel = (src == want).astype(BF16)
    for q in range(w_ref.shape[1] // sub):
        w = w_ref[:, q * sub:(q + 1) * sub].astype(BF16)
        r = jnp.dot(w, sel, preferred_element_type=F32).astype(BF16)
        g_ref[:, q * half:(q + 1) * half] = r[:, :half]
        l_ref[:, q * half:(q + 1) * half] = r[:, half:]


def _rwkv_scan_kernel(r_ref, w_ref, k_ref, a_ref, b_ref, v_ref, w1_ref, w2_ref, y_ref, w1g_ref,
                      w1l_ref, w2b_ref, s_ref, *, tc):
    n = RWKV_HEAD

    @pl.when(pl.program_id(0) == 0)
    def _():
        s_ref[...] = jnp.zeros_like(s_ref)

    _split_even_odd(w1_ref, w1g_ref, w1l_ref)
    w2b_ref[...] = w2_ref[...].astype(BF16)

    slabs = n // SCAN_SLAB
    zero = jnp.zeros((SCAN_SLAB, s_ref.shape[2]), F32)

    def row(ref, tb, tt, j):
        return ref[tb, j, tt:tt + 1, :]

    def rows_of(slab):
        if isinstance(slab, int):
            return pl.ds(slab * SCAN_SLAB, SCAN_SLAB)
        return pl.ds(pl.multiple_of(slab * SCAN_SLAB, SCAN_SLAB), SCAN_SLAB)

    def state_times_a(tb, tt, slab):
        rows = rows_of(slab)
        acc = [zero, zero]
        for j in range(n):
            acc[j % 2] = acc[j % 2] + s_ref[j, rows, :] * row(a_ref, tb, tt, j)
        return acc[0] + acc[1]

    def update_and_read(tb, tt, slab, sa):
        rows = rows_of(slab)
        t = tb * SUBLANES + tt
        vt = v_ref[t, rows, :]
        yac = [zero, zero]
        for j in range(n):
            sj = (s_ref[j, rows, :] * row(w_ref, tb, tt, j) + sa * row(b_ref, tb, tt, j)
                  + vt * row(k_ref, tb, tt, j))
            s_ref[j, rows, :] = sj
            yac[j % 2] = yac[j % 2] + sj * row(r_ref, tb, tt, j)
        y_ref[t, rows, :] = yac[0] + yac[1]

    nblk = tc // SUBLANES

    def block(tb, sa):
        for tt in range(SUBLANES):
            def trip(slab, sa, tt=tt):
                update_and_read(tb, tt, slab, sa)
                return state_times_a(tb, tt, slab + 1)

            sa = lax.fori_loop(0, slabs - 1, trip, sa)
            update_and_read(tb, tt, slabs - 1, sa)
            if tt + 1 < SUBLANES:
                sa = state_times_a(tb, tt + 1, 0)
            else:
                sa = state_times_a(jnp.minimum(tb + 1, nblk - 1), 0, 0)
        return sa

    lax.fori_loop(0, nblk, block, state_times_a(0, 0, 0))


def _rwkv_scan_and_moe_weight_prep(r, w, k, a, b, v, w1, w2):
    s, n, l = v.shape
    tc = min(SCAN_STEPS, s)
    assert s % tc == 0
    steps = s // tc
    assert w1.shape[-1] % W1_SPLIT_SUB == 0

    def sliced(wgt, cols_out):
        e, rows, cols = wgt.shape
        assert (e * rows) % steps == 0
        per_step = e * rows // steps
        assert rows % per_step == 0 and per_step % (2 * SUBLANES) == 0
        per_e = rows // per_step
        spec = lambda c: pl.BlockSpec((None, per_step, c), lambda i: (i // per_e, i % per_e, 0))
        return spec(cols), spec(cols_out), jax.ShapeDtypeStruct((e, rows, cols_out), BF16)

    w1_in, w1_out, w1_shape = sliced(w1, w1.shape[-1] // 2)
    w2_in, w2_out, w2_shape = sliced(w2, w2.shape[-1])
    rows = pl.BlockSpec((tc // SUBLANES, n, SUBLANES, l), lambda i: (i, 0, 0, 0))
    blk = pl.BlockSpec((tc, n, l), lambda i: (i, 0, 0))
    return pl.pallas_call(
        functools.partial(_rwkv_scan_kernel, tc=tc),
        grid=(steps,), in_specs=[rows] * 5 + [blk, w1_in, w2_in],
        out_specs=[blk, w1_out, w1_out, w2_out],
        out_shape=[jax.ShapeDtypeStruct((s, n, l), F32), w1_shape, w1_shape, w2_shape],
        scratch_shapes=[pltpu.VMEM((n, n, l), F32)],
        compiler_params=_cparams("arbitrary"), name="rwkv_scan",
    )(r, w, k, a, b, v, w1, w2)


RWKV_PREP_ROWS = 32
SUBLANES = 8


def _head_sum(x):
    cols = x.shape[-1] // LANES
    acc = x[:, :LANES]
    for c in range(1, cols):
        acc = acc + x[:, c * LANES:(c + 1) * LANES]
    acc = acc + pltpu.roll(acc, RWKV_HEADS, 1)
    acc = acc + pltpu.roll(acc, 2 * RWKV_HEADS, 1)
    return jnp.concatenate([acc] * cols, axis=-1)


def _token_shift_rows(p, last_prev_row, mix, first):
    rolled = pltpu.roll(p, 1, 0)
    row0 = jnp.where(first, jnp.zeros_like(last_prev_row), last_prev_row)
    t = lax.broadcasted_iota(jnp.int32, p.shape, 0)
    prev = jnp.where(t == 0, row0, rolled)
    return p + (prev - p) * mix


def _rwkv_prep_kernel(rkv_ref, rkvp_ref, lo_ref, lop_ref, mixr_ref, mixl_ref, w0_ref, a0_ref,
                      kk_ref, ka_ref, rk_ref, wup_ref, aup_ref, gup_ref,
                      r_ref, w_ref, k_ref, al_ref, be_ref, v_ref, g_ref, bo_ref, tiles_ref):
    first = pl.program_id(0) == 0
    d = D_MODEL
    nb, ts = rkv_ref.shape[0], rkv_ref.shape[1]
    mm = lambda a, w_ref_: jnp.dot(a.astype(BF16), w_ref_[...].astype(BF16),
                                   preferred_element_type=F32)
    for bi in range(nb):
        def seg(c):
            sl = slice(c * d, (c + 1) * d)
            return _token_shift_rows(rkv_ref[bi, :, sl], rkvp_ref[bi, SUBLANES - 1:SUBLANES, sl],
                                     mixr_ref[:, sl], first)

        r, k, v = seg(0), seg(1), seg(2)
        zl = _token_shift_rows(lo_ref[bi], lop_ref[bi, SUBLANES - 1:SUBLANES, :], mixl_ref[...],
                               first)
        wd = zl[:, :DECAY_LORA]
        ad = zl[:, DECAY_LORA:DECAY_LORA + AAA_LORA]
        gd = zl[:, DECAY_LORA + AAA_LORA:]
        z = -(w0_ref[...] + mm(jnp.tanh(wd), wup_ref))
        softplus = jnp.maximum(z, 0.0) + jnp.log1p(jnp.exp(-jnp.abs(z)))
        w = -softplus - 0.5
        a = jax.nn.sigmoid(a0_ref[...] + mm(ad, aup_ref))
        kk = k * kk_ref[...]
        kk = kk / jnp.maximum(jnp.sqrt(_head_sum(kk * kk)), 1e-12)
        k2 = k * (1.0 + (a - 1.0) * ka_ref[...])
        for op, val in enumerate((r, jnp.exp(-jnp.exp(w)), k2, -kk, kk * a)):
            for c in range(d // LANES):
                col = val[:, c * LANES:(c + 1) * LANES]
                tiles_ref[op, bi, :, c * LANES:(c + 1) * LANES] = (
                    col if bi == 0 else pltpu.roll(col, bi * RWKV_HEADS, 1))
        v_ref[bi] = v
        g_ref[bi] = mm(jax.nn.sigmoid(gd), gup_ref)
        bo_ref[bi] = _head_sum(r * k2 * rk_ref[...]) * v

    lane_group = lax.broadcasted_iota(jnp.int32, (ts, LANES), 1) // RWKV_HEADS
    for op, out_ref in enumerate((r_ref, w_ref, k_ref, al_ref, be_ref)):
        for c in range(d // LANES):
            cols = [tiles_ref[op, bi, :, c * LANES:(c + 1) * LANES] for bi in range(nb)]
            for jm in range(nb):
                sel = cols[(0 - jm) % nb]
                for g in range(1, nb):
                    sel = jnp.where(lane_group == g, cols[(g - jm) % nb], sel)
                out = sel if jm == 0 else pltpu.roll(sel, (nb - jm) * RWKV_HEADS, 1)
                for tb in range(ts // SUBLANES):
                    out_ref[tb, c * nb + jm] = out[tb * SUBLANES:(tb + 1) * SUBLANES]


def _rwkv_prep(p_rkv, p_lora, shift_mix, w0, w_up, a0, a_up, g_up, k_k, k_a, r_k):
    b, s, c_rkv = p_rkv.shape
    d, ts, c_lo = D_MODEL, min(RWKV_PREP_ROWS, s), p_lora.shape[-1]
    assert s % ts == 0 and ts % SUBLANES == 0
    assert b * RWKV_HEADS == LANES, "the scan packs exactly (batch, head) onto the 128 lanes"
    cur = lambda c: pl.BlockSpec((b, ts, c), lambda i: (0, i, 0))
    prev = lambda c: pl.BlockSpec(
        (b, SUBLANES, c), lambda i: (0, jnp.maximum(i * (ts // SUBLANES) - 1, 0), 0))
    vec = lambda c: pl.BlockSpec((1, c), lambda i: (0, 0))
    full = lambda a: pl.BlockSpec(a.shape, lambda i: (0, 0))
    rows = pl.BlockSpec((ts // SUBLANES, RWKV_HEAD, SUBLANES, LANES), lambda i: (i, 0, 0, 0))
    rows_shape = jax.ShapeDtypeStruct((s // SUBLANES, RWKV_HEAD, SUBLANES, LANES), F32)
    tok_shape = jax.ShapeDtypeStruct((b, s, d), F32)
    row = lambda t: t.reshape(1, -1)
    return pl.pallas_call(
        _rwkv_prep_kernel, grid=(s // ts,),
        in_specs=[cur(c_rkv), prev(c_rkv), cur(c_lo), prev(c_lo), vec(c_rkv), vec(c_lo),
                  vec(d), vec(d), vec(d), vec(d), vec(d), full(w_up), full(a_up), full(g_up)],
        out_specs=[rows] * 5 + [cur(d)] * 3, out_shape=[rows_shape] * 5 + [tok_shape] * 3,
        scratch_shapes=[pltpu.VMEM((5, b, ts, d), F32)],
        compiler_params=_cparams("arbitrary"), name="rwkv_prep",
    )(p_rkv, p_rkv, p_lora, p_lora, row(shift_mix[:c_rkv]), row(shift_mix[c_rkv:]), row(w0),
      row(a0), row(k_k), row(k_a), row(r_k), w_up, a_up, g_up)


def _proj_a_kernel(y_ref, bo_ref, g_ref, lg_ref, lb_ref, w_ref, o_ref):
    y = y_ref[...]
    inv_n = 1.0 / RWKV_HEAD
    c = y - _head_sum(y) * inv_n
    var = _head_sum(c * c) * inv_n
    ya = (c * lax.rsqrt(var + GN_EPS) * lg_ref[...] + lb_ref[...] + bo_ref[...]) * g_ref[...]
    o_ref[...] = jnp.dot(ya.astype(BF16), w_ref[...], preferred_element_type=F32)


def _rwkv_post_proj(y, bonus, gate, lnx_g, lnx_b, proj_a, *, tm=256):
    m, d = y.shape
    assert m % tm == 0
    rowb = pl.BlockSpec((tm, d), lambda i: (i, 0))
    vec = pl.BlockSpec((1, d), lambda i: (0, 0))
    return pl.pallas_call(
        _proj_a_kernel, grid=(m // tm,),
        in_specs=[rowb, rowb, rowb, vec, vec, pl.BlockSpec((d, d), lambda i: (0, 0))],
        out_specs=rowb, out_shape=jax.ShapeDtypeStruct((m, d), F32),
        compiler_params=_cparams("parallel"), name="rwkv_post_proj_a",
    )(y, bonus, gate, lnx_g.reshape(1, d), lnx_b.reshape(1, d), proj_a.astype(BF16))


def _cross_attn_kernel(q_ref, k_ref, v_ref, o_ref):
    scale = CA_HEAD_DIM ** -0.5
    for h in range(CA_HEADS):
        sl = slice(h * CA_HEAD_DIM, (h + 1) * CA_HEAD_DIM)
        q = q_ref[:, sl].astype(BF16)
        k = k_ref[:, sl].astype(BF16)
        s = lax.dot_general(q, k, (((1,), (1,)), ((), ())), preferred_element_type=F32) * scale
        m = jnp.max(s, axis=-1, keepdims=True)
        p = jnp.exp(s - m)
        den = jnp.sum(p, axis=-1, keepdims=True)
        o = jnp.dot(p.astype(BF16), v_ref[:, sl].astype(BF16), preferred_element_type=F32)
        o_ref[:, sl] = (o / den).astype(o_ref.dtype)


def _cross_attention(q, kv, *, tq=512):
    b, s, d = q.shape
    mlen = kv.shape[1]
    tq = min(tq, s)
    return pl.pallas_call(
        _cross_attn_kernel, grid=(b, s // tq),
        in_specs=[pl.BlockSpec((None, tq, d), lambda i, j: (i, j, 0)),
                  pl.BlockSpec((None, mlen, d), lambda i, j: (i, 0, 0)),
                  pl.BlockSpec((None, mlen, d), lambda i, j: (i, 0, 1))],
        out_specs=pl.BlockSpec((None, tq, d), lambda i, j: (i, j, 0)),
        out_shape=jax.ShapeDtypeStruct((b, s, d), BF16),
        compiler_params=_cparams("parallel", "parallel"), name="cross_attn",
    )(q, kv, kv)


def _router_kernel(x_ref, w_ref, b_ref, o_ref):
    o_ref[...] = jnp.dot(x_ref[...], w_ref[...], precision=lax.Precision.HIGHEST,
                         preferred_element_type=F32) + b_ref[...]


def _router_logits(x, w, b, *, tm=512):
    m, d = x.shape
    e = w.shape[1]
    return pl.pallas_call(
        _router_kernel, grid=(m // tm,),
        in_specs=[pl.BlockSpec((tm, d), lambda i: (i, 0)),
                  pl.BlockSpec((d, e), lambda i: (0, 0)),
                  pl.BlockSpec((1, e), lambda i: (0, 0))],
        out_specs=pl.BlockSpec((tm, e), lambda i: (i, 0)),
        out_shape=jax.ShapeDtypeStruct((m, e), F32),
        compiler_params=_cparams("parallel"), name="router",
    )(x, w, b.reshape(1, e))


def _moe_kernel(be_ref, nused_ref, x_ref, g_ref, w1g_ref, w1l_ref, b1g_ref, b1l_ref, w2_ref,
                b2_ref, o_ref):
    i = pl.program_id(0)

    @pl.when(i < nused_ref[0])
    def _():
        x = x_ref[...]
        glu = jnp.dot(x, w1g_ref[...], preferred_element_type=F32) + b1g_ref[...]
        lin = jnp.dot(x, w1l_ref[...], preferred_element_type=F32) + b1l_ref[...]
        glu = jnp.minimum(glu, SWIGLU_LIMIT)
        lin = jnp.clip(lin, -SWIGLU_LIMIT, SWIGLU_LIMIT)
        act = glu * jax.nn.sigmoid(SWIGLU_ALPHA * glu) * (lin + 1.0)
        y = jnp.dot(act.astype(BF16), w2_ref[...], preferred_element_type=F32) + b2_ref[...]
        o_ref[...] = y * g_ref[...]

    @pl.when(i >= nused_ref[0])
    def _():
        o_ref[...] = jnp.zeros_like(o_ref)


def _moe_experts(blk_exp, n_used, xs, row_gate, w1g, w1l, b1g, b1l, w2, b2):
    rows, d = xs.shape
    nblk = rows // MOE_BLK
    f = w1g.shape[-1]
    wspec = lambda shape: pl.BlockSpec(shape, lambda i, be, nu: (be[i], 0, 0))
    grid_spec = pltpu.PrefetchScalarGridSpec(
        num_scalar_prefetch=2, grid=(nblk,),
        in_specs=[pl.BlockSpec((MOE_BLK, d), lambda i, be, nu: (i, 0)),
                  pl.BlockSpec((MOE_BLK, 1), lambda i, be, nu: (i, 0)),
                  wspec((None, d, f)), wspec((None, d, f)),
                  wspec((None, 1, f)), wspec((None, 1, f)),
                  wspec((None, f, d)), wspec((None, 1, d))],
        out_specs=pl.BlockSpec((MOE_BLK, d), lambda i, be, nu: (i, 0)),
    )
    return pl.pallas_call(
        _moe_kernel, grid_spec=grid_spec,
        out_shape=jax.ShapeDtypeStruct((rows, d), F32),
        compiler_params=pltpu.CompilerParams(dimension_semantics=("arbitrary",),
                                             vmem_limit_bytes=V7X_VMEM_LIMIT_MOE_BYTES),
        name="moe_experts",
    )(blk_exp, n_used, xs, row_gate.reshape(rows, 1), w1g, w1l, b1g, b1l, w2, b2)


def _per_expert(table, idx):
    hit = idx[:, None] == jnp.arange(N_EXPERTS, dtype=idx.dtype)[None, :]
    return jnp.sum(jnp.where(hit, table[None, :], 0), axis=1)


def _moe_ffn(x_bf, logits, moe_weights, b1, b2):
    n, d = x_bf.shape
    top_val, top_idx = lax.top_k(logits, TOP_K)
    top_w = jax.nn.softmax(top_val, axis=-1)
    flat_e = top_idx.reshape(-1).astype(jnp.int32)
    order = jnp.argsort(flat_e).astype(jnp.int32)
    rank = jnp.argsort(order).astype(jnp.int32)
    counts = jnp.sum(flat_e[:, None] == jnp.arange(N_EXPERTS, dtype=jnp.int32)[None, :],
                     axis=0, dtype=jnp.int32)
    starts = jnp.cumsum(counts) - counts
    padded = (counts + MOE_BLK - 1) // MOE_BLK * MOE_BLK
    pends = jnp.cumsum(padded)
    pstarts = pends - padded
    rows = n * TOP_K + N_EXPERTS * MOE_BLK
    nblk = rows // MOE_BLK
    blk_start = jnp.arange(nblk, dtype=jnp.int32) * MOE_BLK
    blk_exp = jnp.minimum(jnp.sum(pends[None, :] <= blk_start[:, None], axis=1, dtype=jnp.int32),
                          N_EXPERTS - 1)
    n_used = (pends[-1] // MOE_BLK).astype(jnp.int32).reshape(1)
    row_exp = jnp.repeat(blk_exp, MOE_BLK)
    off = jnp.arange(rows, dtype=jnp.int32) - _per_expert(pstarts, row_exp)
    valid = off < _per_expert(counts, row_exp)
    assign = order[jnp.where(valid, _per_expert(starts, row_exp) + off, 0)]
    row_tok = jnp.where(valid, assign // TOP_K, jnp.arange(rows, dtype=jnp.int32) % n)
    row_gate = jnp.where(valid, top_w.reshape(-1)[assign], 0.0)
    xs = x_bf[row_tok]
    b1g = b1[:, None, 0::2]
    b1l = b1[:, None, 1::2]
    w1g, w1l, w2 = moe_weights
    ys = _moe_experts(blk_exp, n_used, xs, row_gate, w1g, w1l, b1g, b1l, w2,
                      b2[:, None, :])
    pos = (_per_expert(pstarts, flat_e) + rank - _per_expert(starts, flat_e)).reshape(n, TOP_K)
    return ys[pos.T.reshape(-1)]


def _heads_minor(p):
    lead = p.shape[:-1]
    return p.reshape(lead + (RWKV_HEADS, RWKV_HEAD)).swapaxes(-1, -2).reshape(p.shape)


def _to_scan_layout(t, bsz, s):
    return t.reshape(bsz, s, RWKV_HEAD, RWKV_HEADS).transpose(1, 2, 0, 3).reshape(
        s, RWKV_HEAD, bsz * RWKV_HEADS)


def _from_scan_layout(t, bsz, s):
    return t.reshape(s, RWKV_HEAD, bsz, RWKV_HEADS).transpose(2, 0, 1, 3).reshape(bsz, s, D_MODEL)


def _rwkv7_branch_proj(p_rkv, p_lora, shift_mix, w0, w_up, a0, a_up, g_up, k_k, k_a, r_k, lnx_g,
                       lnx_b, proj_a, moe_w1, moe_w2):
    bsz, s, c_rkv = p_rkv.shape
    hm = _heads_minor
    mix = jnp.concatenate([hm(shift_mix[:c_rkv].reshape(3, D_MODEL)).reshape(-1),
                           shift_mix[c_rkv:]])
    r, decay, k, alpha, beta, v, gate, bonus = _rwkv_prep(
        p_rkv, p_lora, mix, hm(w0), hm(w_up), hm(a0), hm(a_up), hm(g_up), hm(k_k), hm(k_a),
        hm(r_k.reshape(-1)))
    y, *moe_weights = _rwkv_scan_and_moe_weight_prep(r, decay, k, alpha, beta,
                                                     _to_scan_layout(v, bsz, s), moe_w1, moe_w2)
    y = _from_scan_layout(y, bsz, s)
    flat = lambda t: t.reshape(bsz * s, D_MODEL)
    proj_a_hm = proj_a.reshape(RWKV_HEADS, RWKV_HEAD, -1).swapaxes(0, 1).reshape(proj_a.shape)
    pa = _rwkv_post_proj(flat(y), flat(bonus), flat(gate), hm(lnx_g), hm(lnx_b), proj_a_hm)
    return pa, moe_weights


def _hybrid_mixer(x, w_in, shift_mix, w0, w_up, a0, a_up, g_up, k_k, k_a, r_k, lnx_g, lnx_b,
                  proj_a, proj_b, moe_w1, moe_w2):
    bsz, s, d = x.shape
    n_tok = bsz * s
    x_bf = x.astype(BF16)
    x2d = x_bf.reshape(n_tok, d)
    c0, c1, c2 = 3 * D_MODEL, RWKV_COLS, RWKV_COLS + ATTN_COLS
    w_rkv = _heads_minor(w_in[:, :c0].reshape(d, 3, D_MODEL)).reshape(d, c0)
    p_rkv = _matmul(x2d, w_rkv, tm=1024, tn=1024, name="in_rkv").reshape(bsz, s, -1)
    p_lora = _matmul(x2d, w_in[:, c0:c1], tm=1024, tn=512, name="in_lora").reshape(bsz, s, -1)
    p_gate = _matmul(x2d, w_in[:, c2:], tm=1024, tn=1024, name="in_gate")

    pa, moe_weights = _rwkv7_branch_proj(p_rkv, p_lora, shift_mix, w0, w_up, a0, a_up, g_up, k_k,
                                         k_a, r_k, lnx_g, lnx_b, proj_a, moe_w1, moe_w2)

    cos_t, sin_t = _rotary_tables(s)
    views = _attn_in_proj(x_bf, w_in[:, c1:c2], cos_t, sin_t)
    outs, lses = [], []
    for gi, (window, dilation) in enumerate(DIL_GROUPS):
        o, l = _dilated_attention(views[gi], gi, window, dilation)
        outs.append(o)
        lses.append(l)
    return _mix_merge(outs, lses, pa, p_gate, proj_b, bsz, s), moe_weights


def _memory_cross_attention(x_bf, mem, wq, wkv, bsz, s):
    d = x_bf.shape[-1]
    q = _matmul(x_bf, wq, tm=1024, tn=1024, name="ca_q").reshape(bsz, s, d)
    kv = _matmul(mem.reshape(-1, d), wkv, tm=512, tn=1024, name="ca_kv").reshape(bsz, -1, 2 * d)
    return _cross_attention(q, kv).reshape(-1, d)


def kernel(x, mem, w_in, shift_mix, w0, w_up, a0, a_up, g_up, k_k, k_a, r_k, lnx_g, lnx_b, proj_a, proj_b, w_out, ln1_g, ln1_b, ca_wq, ca_wkv, ca_wo, ln2_g, ln2_b, router_w, router_b, moe_w1, moe_b1, moe_w2, moe_b2, ln3_g, ln3_b):
    bsz, s, d = x.shape
    n_tok = bsz * s
    for l in range(DEPTH):
        merged, moe_weights = _hybrid_mixer(x, w_in[l], shift_mix[l], w0[l], w_up[l], a0[l],
                                            a_up[l], g_up[l], k_k[l], k_a[l], r_k[l], lnx_g[l],
                                            lnx_b[l], proj_a[l], proj_b[l], moe_w1[l], moe_w2[l])
        x1, x1_bf = _matmul_res_ln(merged, w_out[l], x.reshape(n_tok, d), ln1_g[l], ln1_b[l],
                                   name="w_out_ln1")
        o = _memory_cross_attention(x1_bf, mem, ca_wq[l], ca_wkv[l], bsz, s)
        x2, x2_bf = _matmul_res_ln(o, ca_wo[l], x1, ln2_g[l], ln2_b[l], name="ca_o_ln2")
        logits = _router_logits(x2, router_w[l], router_b[l])
        hs = _moe_ffn(x2_bf, logits, moe_weights, moe_b1[l], moe_b2[l])
        x = _res_layer_norm(x2, hs, ln3_g[l], ln3_b[l], name="ln3").reshape(bsz, s, d)
    return x
```
